```python
import math
import jax, jax.numpy as jnp
from jax import lax
import numpy as np

D_MODEL = 1024
BATCH = 2
SEQ = 8192
DEPTH = 1

N_META = 16
RWKV_HEAD_DIM = 64
RWKV_HEADS = D_MODEL // RWKV_HEAD_DIM
RWKV_WIDTH = RWKV_HEADS * RWKV_HEAD_DIM
DECAY_LORA = 64
AAA_LORA = 64
GATE_LORA = 160
RWKV_LN_EPS = 64e-5
DIFF_HEAD_DIM = 64
DIFF_HEADS = D_MODEL // (2 * DIFF_HEAD_DIM)
DIFF_V_DIM = 2 * DIFF_HEAD_DIM
DIFF_QK_WIDTH = 2 * DIFF_HEADS * DIFF_HEAD_DIM
DIFF_V_WIDTH = DIFF_HEADS * DIFF_V_DIM
ROPE_THETA = 500000.0
ROPE_DIM = DIFF_HEAD_DIM // 4
D_FF = 4 * D_MODEL
N_BRANCH = 2
Q_BLOCK = 128
NORM_EPS = 1e-5
SUBLN_EPS = 1e-5

RWKV_COLS = 3 * RWKV_WIDTH + DECAY_LORA + AAA_LORA + GATE_LORA
DIFF_COLS = 2 * DIFF_QK_WIDTH + DIFF_V_WIDTH
GATE_COLS = N_BRANCH * D_MODEL
N_IN = RWKV_COLS + DIFF_COLS + GATE_COLS

kernel_name = "hybrid_rwkv7_diffattn_gated_block"


def rms_norm(x, w, eps=NORM_EPS):
    xf = x.astype(jnp.float32)
    y = xf * lax.rsqrt(jnp.mean(xf * xf, axis=-1, keepdims=True) + eps)
    return (y * w.astype(jnp.float32)).astype(x.dtype)


def token_shift(p):
    return jnp.pad(p, ((0, 0), (1, 0), (0, 0)))[:, :-1]


def rwkv7_mix(p, mu, w0, w2, a0, a2, g2, k_k, k_a, r_k, ln_w, ln_b):
    f32 = jnp.float32
    bsz, seqlen, _ = p.shape
    p = p + (token_shift(p) - p) * mu
    c1 = RWKV_WIDTH
    c2 = 2 * RWKV_WIDTH
    c3 = 3 * RWKV_WIDTH
    c4 = c3 + DECAY_LORA
    c5 = c4 + AAA_LORA
    r, k, v, wd, ad, gd = jnp.split(p, [c1, c2, c3, c4, c5], axis=-1)
    w = -jax.nn.softplus(-(w0 + jnp.tanh(wd) @ w2).astype(f32)) - 0.5
    decay = jnp.exp(-jnp.exp(w))
    a = jax.nn.sigmoid((a0 + ad @ a2).astype(f32))
    g = jax.nn.sigmoid(gd) @ g2

    def heads(t):
        return t.astype(f32).reshape(bsz, seqlen, RWKV_HEADS, RWKV_HEAD_DIM)

    kk = heads(k * k_k)
    kk = kk / jnp.maximum(jnp.sqrt(jnp.sum(kk * kk, axis=-1, keepdims=True)), 1e-12)
    k = k.astype(f32) * (1.0 + (a - 1.0) * k_a.astype(f32))
    r_h, k_h, v_h, a_h, w_h = heads(r), heads(k), heads(v), heads(a), heads(decay)

    def tm(t):
        return jnp.moveaxis(t, 1, 0)

    xs = (tm(r_h), tm(w_h), tm(k_h), tm(v_h), tm(-kk), tm(kk * a_h))

    def step(S, inp):
        rt, wt, kt, vt, at, bt = inp
        sa = jnp.einsum('bhvk,bhk->bhv', S, at)
        S = S * wt[:, :, None, :] + sa[..., None] * bt[:, :, None, :] + vt[..., None] * kt[:, :, None, :]
        y = jnp.einsum('bhvk,bhk->bhv', S, rt)
        return S, y

    S0 = jnp.zeros((bsz, RWKV_HEADS, RWKV_HEAD_DIM, RWKV_HEAD_DIM), f32)
    _, ys = lax.scan(step, S0, xs)
    y = jnp.moveaxis(ys, 0, 1)
    mean = jnp.mean(y, axis=-1, keepdims=True)
    var = jnp.mean(jnp.square(y - mean), axis=-1, keepdims=True)
    y = (y - mean) * lax.rsqrt(var + RWKV_LN_EPS)
    y = y.reshape(bsz, seqlen, RWKV_WIDTH) * ln_w.astype(f32) + ln_b.astype(f32)
    bonus = jnp.sum(r_h * k_h * r_k.astype(f32), axis=-1, keepdims=True) * v_h
    y = y + bonus.reshape(bsz, seqlen, RWKV_WIDTH)
    return (y * g.astype(f32)).astype(p.dtype)


def partial_rope(x, pos):
    half = ROPE_DIM // 2
    inv = ROPE_THETA ** (-jnp.arange(0, ROPE_DIM, 2, dtype=jnp.float32) / ROPE_DIM)
    ang = pos.astype(jnp.float32)[:, None] * inv[None, :]
    cos = jnp.cos(ang)[None, :, None, :]
    sin = jnp.sin(ang)[None, :, None, :]
    x1 = x[..., :half].astype(jnp.float32)
    x2 = x[..., half:ROPE_DIM].astype(jnp.float32)
    rot = jnp.concatenate([x1 * cos - x2 * sin, x2 * cos + x1 * sin], axis=-1).astype(x.dtype)
    return jnp.concatenate([rot, x[..., ROPE_DIM:]], axis=-1)


def diff_attention(p, lq1, lk1, lq2, lk2, subln_w, lambda_init):
    f32 = jnp.float32
    bsz, seqlen, _ = p.shape
    q, k, v = jnp.split(p, [DIFF_QK_WIDTH, 2 * DIFF_QK_WIDTH], axis=-1)
    pos = jnp.arange(seqlen)
    q = partial_rope(q.reshape(bsz, seqlen, 2 * DIFF_HEADS, DIFF_HEAD_DIM), pos)
    k = partial_rope(k.reshape(bsz, seqlen, 2 * DIFF_HEADS, DIFF_HEAD_DIM), pos)
    q = jnp.transpose(q, (0, 2, 1, 3)) * (DIFF_HEAD_DIM ** -0.5)
    k = jnp.transpose(k, (0, 2, 1, 3))
    v = jnp.transpose(v.reshape(bsz, seqlen, DIFF_HEADS, DIFF_V_DIM), (0, 2, 1, 3))
    lam = (jnp.exp(jnp.sum(lq1.astype(f32) * lk1.astype(f32)))
           - jnp.exp(jnp.sum(lq2.astype(f32) * lk2.astype(f32))) + lambda_init)
    neg = jnp.finfo(f32).min

    def attend(qb, qpos):
        nq = qb.shape[2]
        s = jnp.einsum('bhqd,bhkd->bhqk', qb, k, preferred_element_type=f32)
        s = jnp.where(pos[None, :] <= qpos[:, None], s, neg)
        pr = jax.nn.softmax(s, axis=-1).reshape(bsz, DIFF_HEADS, 2, nq, seqlen)
        attn_w = pr[:, :, 0] - lam * pr[:, :, 1]
        return jnp.einsum('bhqk,bhkd->bhqd', attn_w.astype(v.dtype), v)

    meta_out = attend(q[:, :, :N_META], pos[:N_META])
    n_blk = (seqlen - N_META) // Q_BLOCK

    def blk(i):
        start = N_META + i * Q_BLOCK
        qb = lax.dynamic_slice_in_dim(q, start, Q_BLOCK, axis=2)
        return attend(qb, start + jnp.arange(Q_BLOCK))

    outs = lax.map(blk, jnp.arange(n_blk))
    outs = jnp.transpose(outs, (1, 2, 0, 3, 4)).reshape(bsz, DIFF_HEADS, n_blk * Q_BLOCK, DIFF_V_DIM)
    o = jnp.concatenate([meta_out, outs], axis=2)
    o = rms_norm(o, subln_w, SUBLN_EPS) * (1.0 - lambda_init)
    return jnp.transpose(o, (0, 2, 1, 3)).reshape(bsz, seqlen, DIFF_V_WIDTH)


def setup_inputs(seed: int = 0) -> dict:
    key = jax.random.key(seed)
    ks = jax.random.split(key, 32)
    f32 = jnp.float32

    def nrm(k, shape, scale):
        return jax.random.normal(k, shape, f32) * scale

    L = DEPTH
    return {
        "x": nrm(ks[0], (BATCH, SEQ, D_MODEL), 1.0),
        "meta_tokens": nrm(ks[1], (N_META, D_MODEL), 1.0),
        "norm_mix_w": 1.0 + nrm(ks[2], (L, D_MODEL), 0.02),
        "w_in": nrm(ks[3], (L, D_MODEL, N_IN), D_MODEL ** -0.5),
        "rwkv_mu": jax.random.uniform(ks[4], (L, RWKV_COLS), f32, 0.0, 1.0),
        "rwkv_w0": jax.random.uniform(ks[5], (L, RWKV_WIDTH), f32, -6.5, -1.5),
        "rwkv_w2": nrm(ks[6], (L, DECAY_LORA, RWKV_WIDTH), 0.1 * DECAY_LORA ** -0.5),
        "rwkv_a0": nrm(ks[7], (L, RWKV_WIDTH), 0.01),
        "rwkv_a2": nrm(ks[8], (L, AAA_LORA, RWKV_WIDTH), 0.5 * AAA_LORA ** -0.5),
        "rwkv_g2": nrm(ks[9], (L, GATE_LORA, RWKV_WIDTH), GATE_LORA ** -0.5),
        "rwkv_k_k": 0.85 + nrm(ks[10], (L, RWKV_WIDTH), 0.02),
        "rwkv_k_a": 1.0 + nrm(ks[11], (L, RWKV_WIDTH), 0.02),
        "rwkv_r_k": nrm(ks[12], (L, RWKV_HEADS, RWKV_HEAD_DIM), 0.1),
        "rwkv_ln_w": 1.0 + nrm(ks[13], (L, RWKV_WIDTH), 0.02),
        "rwkv_ln_b": nrm(ks[14], (L, RWKV_WIDTH), 0.01),
        "rwkv_w_o": nrm(ks[15], (L, RWKV_WIDTH, D_MODEL), RWKV_WIDTH ** -0.5),
        "diff_lq1": nrm(ks[16], (L, DIFF_HEAD_DIM), 0.1),
        "diff_lk1": nrm(ks[17], (L, DIFF_HEAD_DIM), 0.1),
        "diff_lq2": nrm(ks[18], (L, DIFF_HEAD_DIM), 0.1),
        "diff_lk2": nrm(ks[19], (L, DIFF_HEAD_DIM), 0.1),
        "diff_subln_w": 1.0 + nrm(ks[20], (L, DIFF_V_DIM), 0.02),
        "diff_w_o": nrm(ks[21], (L, DIFF_V_WIDTH, D_MODEL), DIFF_V_WIDTH ** -0.5),
        "w_out": nrm(ks[22], (L, D_MODEL, D_MODEL), D_MODEL ** -0.5),
        "norm_mlp_w": 1.0 + nrm(ks[23], (L, D_MODEL), 0.02),
        "mlp_w1": nrm(ks[24], (L, D_MODEL, D_FF), D_MODEL ** -0.5),
        "mlp_w2": nrm(ks[25], (L, D_FF, D_MODEL), D_FF ** -0.5),
        "final_norm_w": 1.0 + nrm(ks[26], (D_MODEL,), 0.02),
    }


def reference(x, meta_tokens, norm_mix_w, w_in, rwkv_mu, rwkv_w0, rwkv_w2, rwkv_a0, rwkv_a2, rwkv_g2,
              rwkv_k_k, rwkv_k_a, rwkv_r_k, rwkv_ln_w, rwkv_ln_b, rwkv_w_o, diff_lq1, diff_lk1, diff_lq2,
              diff_lk2, diff_subln_w, diff_w_o, w_out, norm_mlp_w, mlp_w1, mlp_w2, final_norm_w):
    bsz = x.shape[0]
    meta = jnp.broadcast_to(meta_tokens[None].astype(x.dtype), (bsz, N_META, D_MODEL))
    h_res = jnp.concatenate([meta, x], axis=1)
    seqlen = h_res.shape[1]
    for layer in range(DEPTH):
        lambda_init = 0.8 - 0.6 * math.exp(-0.3 * layer)
        h = rms_norm(h_res, norm_mix_w[layer])
        proj = h @ w_in[layer]
        p_rwkv, p_diff, p_gate = jnp.split(proj, [RWKV_COLS, RWKV_COLS + DIFF_COLS], axis=-1)
        o_rwkv = rwkv7_mix(p_rwkv, rwkv_mu[layer], rwkv_w0[layer], rwkv_w2[layer], rwkv_a0[layer],
                           rwkv_a2[layer], rwkv_g2[layer], rwkv_k_k[layer], rwkv_k_a[layer],
                           rwkv_r_k[layer], rwkv_ln_w[layer], rwkv_ln_b[layer]) @ rwkv_w_o[layer]
        o_diff = diff_attention(p_diff, diff_lq1[layer], diff_lk1[layer], diff_lq2[layer],
                                diff_lk2[layer], diff_subln_w[layer], lambda_init) @ diff_w_o[layer]
        gates = jax.nn.sigmoid(p_gate).reshape(bsz, seqlen, N_BRANCH, D_MODEL)
        merged = gates[:, :, 0] * o_rwkv + gates[:, :, 1] * o_diff
        h_res = h_res + merged @ w_out[layer]
        h = rms_norm(h_res, norm_mlp_w[layer])
        h_res = h_res + jnp.square(jax.nn.relu(h @ mlp_w1[layer])) @ mlp_w2[layer]
    out = rms_norm(h_res, final_norm_w)
    return out[:, N_META:]
```

```python
import functools
import math

import jax
import jax.numpy as jnp
from jax import lax
from jax.experimental import pallas as pl
from jax.experimental.pallas import tpu as pltpu

F32 = jnp.float32
BF16 = jnp.bfloat16

D_MODEL = 1024
N_META_TOKENS = 16
HEAD_DIM = 64
PAIR = 2 * HEAD_DIM
N_PAIRS = D_MODEL // PAIR
DECAY_LORA = 64
AAA_LORA = 64
GATE_LORA = 160
RWKV_LN_EPS = 64e-5
ROPE_THETA = 500000.0
ROPE_DIM = HEAD_DIM // 4
ROPE_HALF = ROPE_DIM // 2
D_FF = 4 * D_MODEL
NORM_EPS = 1e-5
SUBLN_EPS = 1e-5
LAMBDA_INIT = 0.8 - 0.6 * math.exp(-0.3 * 0)

LANE = 128
SUBLANE = 8
CHUNK = 64
SEQ_ALIGN = 256
VMEM_LIMIT = 56 * 1024 * 1024

LORA_W_OFF = 3 * D_MODEL
LORA_A_OFF = LORA_W_OFF + LANE
LORA_G_OFF = LORA_A_OFF + LANE
GATE_LORA_PAD = 2 * LANE
RWKV_COLS_PAD = LORA_G_OFF + GATE_LORA_PAD
DIFF_GATE_COLS = 3 * D_MODEL + 2 * D_MODEL


def _params(sem):
    return pltpu.CompilerParams(dimension_semantics=sem, vmem_limit_bytes=VMEM_LIMIT)


def _rms(x, w, eps):
    return x * lax.rsqrt(jnp.mean(x * x, axis=-1, keepdims=True) + eps) * w


def _sigmoid(x):
    return 1.0 / (1.0 + jnp.exp(-x))


def _dot(a, b):
    return jnp.dot(a.astype(BF16), b.astype(BF16), preferred_element_type=F32)


def _dot_nt(a, b):
    return lax.dot_general(a.astype(BF16), b.astype(BF16), (((1,), (1,)), ((), ())),
                           preferred_element_type=F32)


def _dot_tn(a, b):
    return lax.dot_general(a.astype(BF16), b.astype(BF16), (((0,), (0,)), ((), ())),
                           preferred_element_type=F32)


def _in_proj_kernel(*refs, shift, n_rope_tiles, n_q_tiles, tiles_per_seq):
    if shift:
        x_ref, xprev_ref, nw_ref, w_ref, mu_ref, o_ref, xn_scr, xpn_scr = refs
    else:
        x_ref, nw_ref, w_ref, cos_ref, s1_ref, s2_ref, o_ref, xn_scr = refs
    i = pl.program_id(0)
    j = pl.program_id(1)

    @pl.when(j == 0)
    def _():
        xn_scr[...] = _rms(x_ref[...], nw_ref[...], NORM_EPS).astype(BF16)
        if shift:
            xpn_scr[...] = _rms(xprev_ref[...], nw_ref[...], NORM_EPS).astype(BF16)

    p = jnp.dot(xn_scr[...], w_ref[...], preferred_element_type=F32)
    tm, tn = p.shape
    if shift:
        pp = jnp.dot(xpn_scr[...], w_ref[...], preferred_element_type=F32)
        first = (i % tiles_per_seq) == 0
        prev_row = jnp.where(first, 0.0, pp[SUBLANE - 1:SUBLANE, :])
        row = lax.broadcasted_iota(jnp.int32, (tm, tn), 0)
        shifted = jnp.where(row == 0, prev_row, pltpu.roll(p, 1, axis=0))
        o_ref[...] = (p + (shifted - p) * mu_ref[...]).astype(o_ref.dtype)
    else:
        @pl.when(j >= n_rope_tiles)
        def _():
            o_ref[...] = p.astype(o_ref.dtype)

        @pl.when(j < n_rope_tiles)
        def _():
            reps = tn // LANE
            cos = jnp.tile(cos_ref[...], (1, reps))
            s1 = jnp.tile(s1_ref[...], (1, reps))
            s2 = jnp.tile(s2_ref[...], (1, reps))
            nxt = pltpu.roll(p, tn - ROPE_HALF, axis=1)
            prv = pltpu.roll(p, ROPE_HALF, axis=1)
            scale = jnp.where(j < n_q_tiles, HEAD_DIM ** -0.5, 1.0)
            o_ref[...] = ((p * cos + nxt * s1 + prv * s2) * scale).astype(o_ref.dtype)


def _in_proj(xf, nw, w, *, mu=None, rope=None, out_dtype, tm, tn, tiles_per_seq):
    t_tokens = xf.shape[0]
    n_cols = w.shape[1]
    shift = mu is not None
    grid = (t_tokens // tm, n_cols // tn)
    x_spec = pl.BlockSpec((tm, D_MODEL), lambda i, j: (i, 0))
    nw_spec = pl.BlockSpec((1, D_MODEL), lambda i, j: (0, 0))
    w_spec = pl.BlockSpec((D_MODEL, tn), lambda i, j: (0, j))
    scratch = [pltpu.VMEM((tm, D_MODEL), BF16)]
    if shift:
        rows8 = tm // SUBLANE
        prev_spec = pl.BlockSpec((SUBLANE, D_MODEL), lambda i, j: (jnp.maximum(i * rows8 - 1, 0), 0))
        mu_spec = pl.BlockSpec((1, tn), lambda i, j: (0, j))
        in_specs = [x_spec, prev_spec, nw_spec, w_spec, mu_spec]
        args = (xf, xf, nw, w, mu)
        scratch.append(pltpu.VMEM((SUBLANE, D_MODEL), BF16))
        n_rope_tiles = n_q_tiles = 0
    else:
        tab_spec = pl.BlockSpec((tm, LANE), lambda i, j: (i % tiles_per_seq, 0))
        in_specs = [x_spec, nw_spec, w_spec, tab_spec, tab_spec, tab_spec]
        args = (xf, nw, w) + tuple(rope)
        n_q_tiles = D_MODEL // tn
        n_rope_tiles = 2 * n_q_tiles
    kern = functools.partial(_in_proj_kernel, shift=shift, n_rope_tiles=n_rope_tiles,
                             n_q_tiles=n_q_tiles, tiles_per_seq=tiles_per_seq)
    return pl.pallas_call(
        kern,
        grid=grid,
        in_specs=in_specs,
        out_specs=pl.BlockSpec((tm, tn), lambda i, j: (i, j)),
        out_shape=jax.ShapeDtypeStruct((t_tokens, n_cols), out_dtype),
        scratch_shapes=scratch,
        compiler_params=_params(("arbitrary", "arbitrary")),
        name="in_proj_rwkv" if shift else "in_proj_diff",
    )(*args)


def _rwkv_prep_kernel(p_ref, w0_ref, a0_ref, kk_ref, ka_ref, rk_ref, w2_ref, a2_ref, g2_ref,
                      ones_ref, r_o, k_o, v_o, lw_o, kn_o, b_o, bonus_o, g_o):
    r = p_ref[0, :, 0:D_MODEL]
    k = p_ref[0, :, D_MODEL:2 * D_MODEL]
    v = p_ref[0, :, 2 * D_MODEL:3 * D_MODEL]
    wd = p_ref[0, :, LORA_W_OFF:LORA_W_OFF + LANE]
    ad = p_ref[0, :, LORA_A_OFF:LORA_A_OFF + LANE]
    gd = p_ref[0, :, LORA_G_OFF:LORA_G_OFF + GATE_LORA_PAD]
    ones_bd = ones_ref[...]

    z = w0_ref[...] + _dot(jnp.tanh(wd), w2_ref[...])
    softplus_neg = jnp.maximum(-z, 0.0) + jnp.log(1.0 + jnp.exp(-jnp.abs(z)))
    log_decay = -jnp.exp(-softplus_neg - 0.5)
    a_rate = _sigmoid(a0_ref[...] + _dot(ad, a2_ref[...]))
    gate = _dot(_sigmoid(gd), g2_ref[...])

    kraw = k * kk_ref[...]
    sumsq = _dot(kraw * kraw, ones_bd)
    knorm = kraw / jnp.maximum(jnp.sqrt(sumsq), 1e-12)
    kmod = k * (1.0 + (a_rate - 1.0) * ka_ref[...])
    bonus = _dot(r * kmod * rk_ref[...], ones_bd) * v

    bonus_o[0] = bonus
    g_o[0] = gate
    for pair in range(N_PAIRS):
        sl = slice(pair * PAIR, (pair + 1) * PAIR)
        r_o[0, pair] = r[:, sl]
        k_o[0, pair] = kmod[:, sl]
        v_o[0, pair] = v[:, sl]
        lw_o[0, pair] = log_decay[:, sl]
        kn_o[0, pair] = knorm[:, sl]
        b_o[0, pair] = (knorm * a_rate)[:, sl]


def _rwkv_prep(p3, vecs, w2p, a2p, g2p, ones_bd, *, tm):
    bsz, lp, _ = p3.shape
    grid = (bsz, lp // tm)
    vec_spec = pl.BlockSpec((1, D_MODEL), lambda b, i: (0, 0))
    pair_spec = pl.BlockSpec((1, N_PAIRS, tm, PAIR), lambda b, i: (b, 0, i, 0))
    tok_spec = pl.BlockSpec((1, tm, D_MODEL), lambda b, i: (b, i, 0))
    pair_shape = jax.ShapeDtypeStruct((bsz, N_PAIRS, lp, PAIR), F32)
    tok_shape = jax.ShapeDtypeStruct((bsz, lp, D_MODEL), F32)
    return pl.pallas_call(
        _rwkv_prep_kernel,
        grid=grid,
        in_specs=[pl.BlockSpec((1, tm, RWKV_COLS_PAD), lambda b, i: (b, i, 0))]
        + [vec_spec] * 5
        + [pl.BlockSpec((LANE, D_MODEL), lambda b, i: (0, 0)),
           pl.BlockSpec((LANE, D_MODEL), lambda b, i: (0, 0)),
           pl.BlockSpec((GATE_LORA_PAD, D_MODEL), lambda b, i: (0, 0)),
           pl.BlockSpec((D_MODEL, D_MODEL), lambda b, i: (0, 0))],
        out_specs=[pair_spec] * 6 + [tok_spec] * 2,
        out_shape=[pair_shape] * 6 + [tok_shape] * 2,
        compiler_params=_params(("arbitrary", "arbitrary")),
        name="rwkv_prep",
    )(p3, *vecs, w2p, a2p, g2p, ones_bd)


def _block_diag(x, lane_head0):
    return jnp.concatenate([jnp.where(lane_head0, x, 0.0), jnp.where(lane_head0, 0.0, x)], axis=0)


def _chunk_unit(r, k, v, lw, kn, b, state):
    n = 2 * CHUNK
    row = lax.broadcasted_iota(jnp.int32, (CHUNK, PAIR), 0)
    lane_head0 = lax.broadcasted_iota(jnp.int32, (CHUNK, PAIR), 1) < HEAD_DIM
    c = lw
    for s in (1, 2, 4, 8, 16, 32):
        c = c + jnp.where(row >= s, pltpu.roll(c, s, axis=0), 0.0)
    c_end = c[CHUNK - 1:CHUNK, :]
    e_neg = jnp.exp(-c)
    e_end = jnp.exp(c_end - c)
    a_t = _block_diag(-kn * jnp.exp(c - lw), lane_head0)
    r_t = _block_diag(r * jnp.exp(c), lane_head0)
    b_t = _block_diag(b * e_neg, lane_head0)
    k_t = _block_diag(k * e_neg, lane_head0)
    b_e = _block_diag(b * e_end, lane_head0)
    k_e = _block_diag(k * e_end, lane_head0)
    v_bd = _block_diag(v, lane_head0)

    g = _dot_nt(jnp.concatenate([a_t, r_t], axis=0), jnp.concatenate([b_t, k_t], axis=0))
    ri = lax.broadcasted_iota(jnp.int32, (n, n), 0)
    ci = lax.broadcasted_iota(jnp.int32, (n, n), 1)
    m_ab = jnp.where(ci < ri, g[:n, :n], 0.0)
    m_ak = jnp.where(ci < ri, g[:n, n:], 0.0)
    a_rb = jnp.where(ci <= ri, g[n:, :n], 0.0)
    a_rk = jnp.where(ci <= ri, g[n:, n:], 0.0)

    t_inv = jnp.where(ci == ri, 1.0, m_ab)
    power = _dot(m_ab, m_ab)
    for _ in range(4):
        prod = _dot(power, jnp.concatenate([power, t_inv], axis=1))
        power = prod[:, :n]
        t_inv = t_inv + prod[:, n:]
    t_inv = t_inv + _dot(power, t_inv)

    w1 = _dot(m_ak, v_bd)
    tw = _dot(t_inv, jnp.concatenate([w1, a_t], axis=1))
    u0 = tw[:, :n]
    a_hat = tw[:, n:]
    rhs = jnp.concatenate([jnp.concatenate([a_hat, u0], axis=1),
                           jnp.concatenate([jnp.zeros_like(v_bd), v_bd], axis=1)], axis=0)
    pq = _dot_tn(jnp.concatenate([b_e, k_e], axis=0), rhs)
    ry = _dot(jnp.concatenate([a_rb, a_rk], axis=1), rhs)
    decay_end = jnp.exp(jnp.broadcast_to(c_end, (n, n)))
    p_mat = pq[:, :n] + jnp.where(ci == ri, decay_end, 0.0)
    y_bd = _dot(r_t + ry[:, :n], state) + ry[:, n:]
    y = y_bd[:CHUNK] + y_bd[CHUNK:]
    new_state = _dot(p_mat, state) + pq[:, n:]
    return y, new_state


def _rwkv_chunk_kernel(r_ref, k_ref, v_ref, lw_ref, kn_ref, b_ref, y_ref, state_scr, *, n_units):
    @pl.when(pl.program_id(0) == 0)
    def _():
        state_scr[...] = jnp.zeros_like(state_scr)

    def body(u, carry):
        bi = u // N_PAIRS
        pi = u % N_PAIRS
        y, new_state = _chunk_unit(r_ref[bi, pi], k_ref[bi, pi], v_ref[bi, pi], lw_ref[bi, pi],
                                   kn_ref[bi, pi], b_ref[bi, pi], state_scr[u])
        y_ref[bi, pi] = y
        state_scr[u] = new_state
        return carry

    lax.fori_loop(0, n_units, body, 0)


def _rwkv_chunk(r, k, v, lw, kn, b):
    bsz, _, lp, _ = r.shape
    n_units = bsz * N_PAIRS
    spec = pl.BlockSpec((bsz, N_PAIRS, CHUNK, PAIR), lambda c: (0, 0, c, 0))
    return pl.pallas_call(
        functools.partial(_rwkv_chunk_kernel, n_units=n_units),
        grid=(lp // CHUNK,),
        in_specs=[spec] * 6,
        out_specs=spec,
        out_shape=jax.ShapeDtypeStruct(r.shape, F32),
        scratch_shapes=[pltpu.VMEM((n_units, PAIR, PAIR), F32)],
        compiler_params=_params(("arbitrary",)),
        name="rwkv_chunk",
    )(r, k, v, lw, kn, b)


def _diff_attn_kernel(q_ref, k_ref, v_ref, lq1_ref, lk1_ref, lq2_ref, lk2_ref, subw_ref, o_ref,
                      m_scr, l_scr, acc_scr, *, tq):
    i = pl.program_id(2)
    q = q_ref[0]
    head0 = lax.broadcasted_iota(jnp.int32, (tq, PAIR), 1) < HEAD_DIM
    zero = jnp.zeros_like(q)
    qq = jnp.concatenate([jnp.where(head0, q, zero), jnp.where(head0, zero, q)], axis=0)
    m_scr[...] = jnp.full_like(m_scr, -1e30)
    l_scr[...] = jnp.zeros_like(l_scr)
    acc_scr[...] = jnp.zeros_like(acc_scr)

    def step(j, masked):
        start = pl.multiple_of(j * tq, tq)
        kb = k_ref[0, pl.ds(start, tq), :]
        vb = v_ref[0, pl.ds(start, tq), :]
        s = _dot_nt(qq, kb)
        if masked:
            rr = lax.broadcasted_iota(jnp.int32, (2 * tq, tq), 0) % tq
            cc = lax.broadcasted_iota(jnp.int32, (2 * tq, tq), 1)
            s = jnp.where(cc <= rr, s, -1e30)
        m_prev = m_scr[...]
        m_next = jnp.maximum(m_prev, jnp.max(s, axis=1, keepdims=True))
        p = jnp.exp(s - jnp.tile(m_next, (1, tq // LANE)))
        alpha = jnp.exp(m_prev - m_next)
        l_scr[...] = alpha * l_scr[...] + jnp.sum(p, axis=1, keepdims=True)
        acc_scr[...] = alpha * acc_scr[...] + _dot(p, vb)
        m_scr[...] = m_next

    def body(j, carry):
        step(j, False)
        return carry

    lax.fori_loop(0, i, body, 0)
    step(i, True)

    o = acc_scr[...] / l_scr[...]
    lam = (jnp.exp(jnp.sum(lq1_ref[...] * lk1_ref[...], axis=1, keepdims=True))
           - jnp.exp(jnp.sum(lq2_ref[...] * lk2_ref[...], axis=1, keepdims=True)) + LAMBDA_INIT)
    od = o[:tq] - lam * o[tq:]
    o_ref[0] = (_rms(od, subw_ref[...], SUBLN_EPS) * (1.0 - LAMBDA_INIT)).astype(o_ref.dtype)


def _diff_attn(a3, lams, subw, *, tq):
    bsz, lp, _ = a3.shape
    grid = (bsz, N_PAIRS, lp // tq)
    kv_spec_k = pl.BlockSpec((1, lp, PAIR), lambda b, h, i: (b, 0, N_PAIRS + h))
    kv_spec_v = pl.BlockSpec((1, lp, PAIR), lambda b, h, i: (b, 0, 2 * N_PAIRS + h))
    lam_spec = pl.BlockSpec((1, HEAD_DIM), lambda b, h, i: (0, 0))
    return pl.pallas_call(
        functools.partial(_diff_attn_kernel, tq=tq),
        grid=grid,
        in_specs=[pl.BlockSpec((1, tq, PAIR), lambda b, h, i: (b, i, h)), kv_spec_k, kv_spec_v,
                  lam_spec, lam_spec, lam_spec, lam_spec,
                  pl.BlockSpec((1, PAIR), lambda b, h, i: (0, 0))],
        out_specs=pl.BlockSpec((1, tq, PAIR), lambda b, h, i: (b, i, h)),
        out_shape=jax.ShapeDtypeStruct((bsz, lp, D_MODEL), BF16),
        scratch_shapes=[pltpu.VMEM((2 * tq, PAIR), F32)] * 3,
        compiler_params=_params(("arbitrary", "arbitrary", "arbitrary")),
        name="diff_attn",
    )(a3, a3, a3, *lams, subw)


def _merge_kernel(y_ref, bonus_ref, g_ref, od_ref, gate_r_ref, gate_d_ref, res_ref, lnw_ref, lnb_ref,
                  ones_ref, wo_r_ref, wo_d_ref, wout_ref, o_ref):
    ones_pair = ones_ref[...]
    parts = []
    for pair in range(N_PAIRS):
        y = y_ref[0, pair]
        mean = _dot(y, ones_pair) * (1.0 / HEAD_DIM)
        yc = y - mean
        var = _dot(yc * yc, ones_pair) * (1.0 / HEAD_DIM)
        parts.append(yc * lax.rsqrt(var + RWKV_LN_EPS))
    yn = jnp.concatenate(parts, axis=1)
    o_rwkv = (yn * lnw_ref[...] + lnb_ref[...] + bonus_ref[0]) * g_ref[0]
    br_rwkv = _dot(o_rwkv, wo_r_ref[...])
    br_diff = jnp.dot(od_ref[0], wo_d_ref[...], preferred_element_type=F32)
    merged = (_sigmoid(gate_r_ref[0].astype(F32)) * br_rwkv
              + _sigmoid(gate_d_ref[0].astype(F32)) * br_diff)
    o_ref[0] = res_ref[0] + _dot(merged, wout_ref[...])


def _merge(y, bonus, g, od, a3, res, lnw, lnb, ones_pair, wo_r, wo_d, wout, *, tm):
    bsz, lp, _ = res.shape
    grid = (bsz, lp // tm)
    tok = lambda c: pl.BlockSpec((1, tm, c), lambda b, i: (b, i, 0))
    vec = pl.BlockSpec((1, D_MODEL), lambda b, i: (0, 0))
    mat = pl.BlockSpec((D_MODEL, D_MODEL), lambda b, i: (0, 0))
    gate_r_spec = pl.BlockSpec((1, tm, D_MODEL), lambda b, i: (b, i, 3))
    gate_d_spec = pl.BlockSpec((1, tm, D_MODEL), lambda b, i: (b, i, 4))
    return pl.pallas_call(
        _merge_kernel,
        grid=grid,
        in_specs=[pl.BlockSpec((1, N_PAIRS, tm, PAIR), lambda b, i: (b, 0, i, 0)),
                  tok(D_MODEL), tok(D_MODEL), tok(D_MODEL), gate_r_spec, gate_d_spec, tok(D_MODEL), vec, vec,
                  pl.BlockSpec((PAIR, PAIR), lambda b, i: (0, 0)), mat, mat, mat],
        out_specs=tok(D_MODEL),
        out_shape=jax.ShapeDtypeStruct(res.shape, F32),
        compiler_params=_params(("arbitrary", "arbitrary")),
        name="merge",
    )(y, bonus, g, od, a3, a3, res, lnw, lnb, ones_pair, wo_r, wo_d, wout)


def _mlp_kernel(x_ref, nw_ref, w1_ref, w2_ref, fw_ref, o_ref, *, n_ff_chunks):
    x = x_ref[...]
    h = _rms(x, nw_ref[...], NORM_EPS).astype(BF16)
    acc = x
    ff = D_FF // n_ff_chunks
    for c in range(n_ff_chunks):
        z = jnp.dot(h, w1_ref[:, c * ff:(c + 1) * ff], preferred_element_type=F32)
        act = jnp.square(jnp.maximum(z, 0.0))
        acc = acc + _dot(act, w2_ref[c * ff:(c + 1) * ff, :])
    o_ref[...] = _rms(acc, fw_ref[...], NORM_EPS)


def _mlp(xf, nw, w1, w2, fw, *, tm):
    t_tokens = xf.shape[0]
    tok = pl.BlockSpec((tm, D_MODEL), lambda i: (i, 0))
    vec = pl.BlockSpec((1, D_MODEL), lambda i: (0, 0))
    return pl.pallas_call(
        functools.partial(_mlp_kernel, n_ff_chunks=4),
        grid=(t_tokens // tm,),
        in_specs=[tok, vec, pl.BlockSpec((D_MODEL, D_FF), lambda i: (0, 0)),
                  pl.BlockSpec((D_FF, D_MODEL), lambda i: (0, 0)), vec],
        out_specs=tok,
        out_shape=jax.ShapeDtypeStruct(xf.shape, F32),
        compiler_params=_params(("arbitrary",)),
        name="mlp",
    )(xf, nw, w1, w2, fw)


def _pad_cols(a, width):
    return jnp.pad(a, ((0, 0), (0, width - a.shape[1])))


def _pad_rows(a, height):
    return jnp.pad(a, ((0, height - a.shape[0]), (0, 0)))


def _rope_tables(lp):
    inv = ROPE_THETA ** (-jnp.arange(0, ROPE_DIM, 2, dtype=F32) / ROPE_DIM)
    ang = jnp.arange(lp, dtype=F32)[:, None] * inv[None, :]
    cos, sin = jnp.cos(ang), jnp.sin(ang)
    rest = HEAD_DIM - ROPE_DIM
    ones = jnp.ones((lp, rest), F32)
    zeros = jnp.zeros((lp, rest), F32)
    zh = jnp.zeros_like(sin)
    c = jnp.concatenate([cos, cos, ones], axis=1)
    s1 = jnp.concatenate([-sin, zh, zeros], axis=1)
    s2 = jnp.concatenate([zh, sin, zeros], axis=1)
    return tuple(jnp.tile(t, (1, LANE // HEAD_DIM)) for t in (c, s1, s2))


def _pick(n, candidates):
    for c in candidates:
        if n % c == 0:
            return c
    raise ValueError(n)


def kernel(x, meta_tokens, norm_mix_w, w_in, rwkv_mu, rwkv_w0, rwkv_w2, rwkv_a0, rwkv_a2, rwkv_g2, rwkv_k_k, rwkv_k_a, rwkv_r_k, rwkv_ln_w, rwkv_ln_b, rwkv_w_o, diff_lq1, diff_lk1, diff_lq2, diff_lk2, diff_subln_w, diff_w_o, w_out, norm_mlp_w, mlp_w1, mlp_w2, final_norm_w):
    bsz, seq, _ = x.shape
    seqlen = seq + N_META_TOKENS
    lp = -(-seqlen // SEQ_ALIGN) * SEQ_ALIGN
    tm_proj = _pick(lp, (768, 512, 256))
    tm_tok = 256

    meta = jnp.broadcast_to(meta_tokens[None].astype(x.dtype), (bsz, N_META_TOKENS, D_MODEL))
    h_res = jnp.concatenate([meta, x, jnp.zeros((bsz, lp - seqlen, D_MODEL), x.dtype)], axis=1)
    hf = h_res.reshape(bsz * lp, D_MODEL)

    w = w_in[0]
    c3 = 3 * D_MODEL
    w_rwkv = jnp.concatenate(
        [w[:, :c3], _pad_cols(w[:, c3:c3 + DECAY_LORA], LANE),
         _pad_cols(w[:, c3 + DECAY_LORA:c3 + DECAY_LORA + AAA_LORA], LANE),
         _pad_cols(w[:, c3 + DECAY_LORA + AAA_LORA:c3 + DECAY_LORA + AAA_LORA + GATE_LORA], GATE_LORA_PAD)],
        axis=1).astype(BF16)
    rwkv_cols = c3 + DECAY_LORA + AAA_LORA + GATE_LORA
    w_diff = w[:, rwkv_cols:].astype(BF16)
    mu = rwkv_mu[0][None, :]
    mu_p = jnp.concatenate(
        [mu[:, :c3], _pad_cols(mu[:, c3:c3 + DECAY_LORA], LANE),
         _pad_cols(mu[:, c3 + DECAY_LORA:c3 + DECAY_LORA + AAA_LORA], LANE),
         _pad_cols(mu[:, c3 + DECAY_LORA + AAA_LORA:], GATE_LORA_PAD)], axis=1)
    nw = norm_mix_w[0][None, :]

    tiles_per_seq = lp // tm_proj
    p_rwkv = _in_proj(hf, nw, w_rwkv, mu=mu_p, out_dtype=F32, tm=tm_proj, tn=512,
                      tiles_per_seq=tiles_per_seq)
    a_diff = _in_proj(hf, nw, w_diff, rope=_rope_tables(lp), out_dtype=BF16, tm=tm_proj, tn=512,
                      tiles_per_seq=tiles_per_seq)

    head_id = jnp.arange(D_MODEL) // HEAD_DIM
    ones_bd = (head_id[:, None] == head_id[None, :]).astype(BF16)
    vecs = (rwkv_w0[0][None, :], rwkv_a0[0][None, :], rwkv_k_k[0][None, :], rwkv_k_a[0][None, :],
            rwkv_r_k[0].reshape(1, D_MODEL))
    r, kmod, v, lw, kn, b, bonus, g = _rwkv_prep(
        p_rwkv.reshape(bsz, lp, RWKV_COLS_PAD), vecs,
        _pad_rows(rwkv_w2[0], LANE).astype(BF16), _pad_rows(rwkv_a2[0], LANE).astype(BF16),
        _pad_rows(rwkv_g2[0], GATE_LORA_PAD).astype(BF16), ones_bd, tm=tm_tok)
    y = _rwkv_chunk(r, kmod, v, lw, kn, b)

    a3 = a_diff.reshape(bsz, lp, DIFF_GATE_COLS)
    o_diff = _diff_attn(a3, (diff_lq1, diff_lk1, diff_lq2, diff_lk2), diff_subln_w[0][None, :], tq=256)

    h_mid = _merge(y, bonus, g, o_diff, a3, h_res, rwkv_ln_w[0][None, :], rwkv_ln_b[0][None, :],
                   ones_bd[:PAIR, :PAIR], rwkv_w_o[0].astype(BF16), diff_w_o[0].astype(BF16),
                   w_out[0].astype(BF16), tm=tm_tok)
    out = _mlp(h_mid.reshape(bsz * lp, D_MODEL), norm_mlp_w[0][None, :], mlp_w1[0].astype(BF16),
               mlp_w2[0].astype(BF16), final_norm_w[None, :], tm=tm_tok)
    return out.reshape(bsz, lp, D_MODEL)[:, N_META_TOKENS:seqlen]
```

```python
import functools
import math

import jax
import jax.numpy as jnp
from jax import lax
from jax.experimental import pallas as pl
from jax.experimental.pallas import tpu as pltpu

F32 = jnp.float32
BF16 = jnp.bfloat16

D_MODEL = 1024
N_META_TOKENS = 16
HEAD_DIM = 64
PAIR = 2 * HEAD_DIM
N_PAIRS = D_MODEL // PAIR
DECAY_LORA = 64
AAA_LORA = 64
GATE_LORA = 160
RWKV_LN_EPS = 64e-5
ROPE_THETA = 500000.0
ROPE_DIM = HEAD_DIM // 4
ROPE_HALF = ROPE_DIM // 2
D_FF = 4 * D_MODEL
NORM_EPS = 1e-5
SUBLN_EPS = 1e-5
LAMBDA_INIT = 0.8 - 0.6 * math.exp(-0.3 * 0)

LANE = 128
SUBLANE = 8
CHUNK = 64
UNIT_UNROLL = 8
SEQ_ALIGN = 256
VMEM_LIMIT = 56 * 1024 * 1024

LORA_W_OFF = 3 * D_MODEL
LORA_A_OFF = LORA_W_OFF + LANE
LORA_G_OFF = LORA_A_OFF + LANE
GATE_LORA_PAD = 2 * LANE
RWKV_COLS_PAD = LORA_G_OFF + GATE_LORA_PAD
DIFF_GATE_COLS = 3 * D_MODEL + 2 * D_MODEL


def _params(sem):
    return pltpu.CompilerParams(dimension_semantics=sem, vmem_limit_bytes=VMEM_LIMIT)


def _rms(x, w, eps):
    return x * lax.rsqrt(jnp.mean(x * x, axis=-1, keepdims=True) + eps) * w


def _sigmoid(x):
    return 1.0 / (1.0 + jnp.exp(-x))


def _dot(a, b):
    return jnp.dot(a.astype(BF16), b.astype(BF16), preferred_element_type=F32)


def _dot_nt(a, b):
    return lax.dot_general(a.astype(BF16), b.astype(BF16), (((1,), (1,)), ((), ())),
                           preferred_element_type=F32)


def _dot_tn(a, b):
    return lax.dot_general(a.astype(BF16), b.astype(BF16), (((0,), (0,)), ((), ())),
                           preferred_element_type=F32)


def _in_proj_kernel(*refs, shift, n_rope_tiles, n_q_tiles, tiles_per_seq):
    if shift:
        x_ref, xprev_ref, nw_ref, w_ref, mu_ref, o_ref, xn_scr, xpn_scr = refs
    else:
        x_ref, nw_ref, w_ref, cos_ref, s1_ref, s2_ref, o_ref, xn_scr = refs
    i = pl.program_id(0)
    j = pl.program_id(1)

    @pl.when(j == 0)
    def _():
        xn_scr[...] = _rms(x_ref[...], nw_ref[...], NORM_EPS).astype(BF16)
        if shift:
            xpn_scr[...] = _rms(xprev_ref[...], nw_ref[...], NORM_EPS).astype(BF16)

    p = jnp.dot(xn_scr[...], w_ref[...], preferred_element_type=F32)
    tm, tn = p.shape
    if shift:
        pp = jnp.dot(xpn_scr[...], w_ref[...], preferred_element_type=F32)
        first = (i % tiles_per_seq) == 0
        prev_row = jnp.where(first, 0.0, pp[SUBLANE - 1:SUBLANE, :])
        row = lax.broadcasted_iota(jnp.int32, (tm, tn), 0)
        shifted = jnp.where(row == 0, prev_row, pltpu.roll(p, 1, axis=0))
        o_ref[...] = (p + (shifted - p) * mu_ref[...]).astype(o_ref.dtype)
    else:
        @pl.when(j >= n_rope_tiles)
        def _():
            o_ref[...] = p.astype(o_ref.dtype)

        @pl.when(j < n_rope_tiles)
        def _():
            reps = tn // LANE
            cos = jnp.tile(cos_ref[...], (1, reps))
            s1 = jnp.tile(s1_ref[...], (1, reps))
            s2 = jnp.tile(s2_ref[...], (1, reps))
            nxt = pltpu.roll(p, tn - ROPE_HALF, axis=1)
            prv = pltpu.roll(p, ROPE_HALF, axis=1)
            scale = jnp.where(j < n_q_tiles, HEAD_DIM ** -0.5, 1.0)
            o_ref[...] = ((p * cos + nxt * s1 + prv * s2) * scale).astype(o_ref.dtype)


def _in_proj(xf, nw, w, *, mu=None, rope=None, out_dtype, tm, tn, tiles_per_seq):
    t_tokens = xf.shape[0]
    n_cols = w.shape[1]
    shift = mu is not None
    grid = (t_tokens // tm, n_cols // tn)
    x_spec = pl.BlockSpec((tm, D_MODEL), lambda i, j: (i, 0))
    nw_spec = pl.BlockSpec((1, D_MODEL), lambda i, j: (0, 0))
    w_spec = pl.BlockSpec((D_MODEL, tn), lambda i, j: (0, j))
    scratch = [pltpu.VMEM((tm, D_MODEL), BF16)]
    if shift:
        rows8 = tm // SUBLANE
        prev_spec = pl.BlockSpec((SUBLANE, D_MODEL), lambda i, j: (jnp.maximum(i * rows8 - 1, 0), 0))
        mu_spec = pl.BlockSpec((1, tn), lambda i, j: (0, j))
        in_specs = [x_spec, prev_spec, nw_spec, w_spec, mu_spec]
        args = (xf, xf, nw, w, mu)
        scratch.append(pltpu.VMEM((SUBLANE, D_MODEL), BF16))
        n_rope_tiles = n_q_tiles = 0
    else:
        tab_spec = pl.BlockSpec((tm, LANE), lambda i, j: (i % tiles_per_seq, 0))
        in_specs = [x_spec, nw_spec, w_spec, tab_spec, tab_spec, tab_spec]
        args = (xf, nw, w) + tuple(rope)
        n_q_tiles = D_MODEL // tn
        n_rope_tiles = 2 * n_q_tiles
    kern = functools.partial(_in_proj_kernel, shift=shift, n_rope_tiles=n_rope_tiles,
                             n_q_tiles=n_q_tiles, tiles_per_seq=tiles_per_seq)
    return pl.pallas_call(
        kern,
        grid=grid,
        in_specs=in_specs,
        out_specs=pl.BlockSpec((tm, tn), lambda i, j: (i, j)),
        out_shape=jax.ShapeDtypeStruct((t_tokens, n_cols), out_dtype),
        scratch_shapes=scratch,
        compiler_params=_params(("arbitrary", "arbitrary")),
        name="in_proj_rwkv" if shift else "in_proj_diff",
    )(*args)


def _rwkv_prep_kernel(p_ref, w0_ref, a0_ref, kk_ref, ka_ref, rk_ref, w2_ref, a2_ref, g2_ref,
                      ones_ref, r_o, k_o, v_o, lw_o, kn_o, b_o, bonus_o, g_o):
    r = p_ref[0, :, 0:D_MODEL]
    k = p_ref[0, :, D_MODEL:2 * D_MODEL]
    v = p_ref[0, :, 2 * D_MODEL:3 * D_MODEL]
    wd = p_ref[0, :, LORA_W_OFF:LORA_W_OFF + LANE]
    ad = p_ref[0, :, LORA_A_OFF:LORA_A_OFF + LANE]
    gd = p_ref[0, :, LORA_G_OFF:LORA_G_OFF + GATE_LORA_PAD]
    ones_bd = ones_ref[...]

    z = w0_ref[...] + _dot(jnp.tanh(wd), w2_ref[...])
    softplus_neg = jnp.maximum(-z, 0.0) + jnp.log(1.0 + jnp.exp(-jnp.abs(z)))
    log_decay = -jnp.exp(-softplus_neg - 0.5)
    a_rate = _sigmoid(a0_ref[...] + _dot(ad, a2_ref[...]))
    gate = _dot(_sigmoid(gd), g2_ref[...])

    kraw = k * kk_ref[...]
    sumsq = _dot(kraw * kraw, ones_bd)
    knorm = kraw / jnp.maximum(jnp.sqrt(sumsq), 1e-12)
    kmod = k * (1.0 + (a_rate - 1.0) * ka_ref[...])
    bonus = _dot(r * kmod * rk_ref[...], ones_bd) * v

    bonus_o[0] = bonus
    g_o[0] = gate
    for pair in range(N_PAIRS):
        sl = slice(pair * PAIR, (pair + 1) * PAIR)
        r_o[0, pair] = r[:, sl]
        k_o[0, pair] = kmod[:, sl]
        v_o[0, pair] = v[:, sl]
        lw_o[0, pair] = log_decay[:, sl]
        kn_o[0, pair] = knorm[:, sl]
        b_o[0, pair] = (knorm * a_rate)[:, sl]


def _rwkv_prep(p3, vecs, w2p, a2p, g2p, ones_bd, *, tm):
    bsz, lp, _ = p3.shape
    grid = (bsz, lp // tm)
    vec_spec = pl.BlockSpec((1, D_MODEL), lambda b, i: (0, 0))
    pair_spec = pl.BlockSpec((1, N_PAIRS, tm, PAIR), lambda b, i: (b, 0, i, 0))
    tok_spec = pl.BlockSpec((1, tm, D_MODEL), lambda b, i: (b, i, 0))
    pair_shape = jax.ShapeDtypeStruct((bsz, N_PAIRS, lp, PAIR), F32)
    tok_shape = jax.ShapeDtypeStruct((bsz, lp, D_MODEL), F32)
    return pl.pallas_call(
        _rwkv_prep_kernel,
        grid=grid,
        in_specs=[pl.BlockSpec((1, tm, RWKV_COLS_PAD), lambda b, i: (b, i, 0))]
        + [vec_spec] * 5
        + [pl.BlockSpec((LANE, D_MODEL), lambda b, i: (0, 0)),
           pl.BlockSpec((LANE, D_MODEL), lambda b, i: (0, 0)),
           pl.BlockSpec((GATE_LORA_PAD, D_MODEL), lambda b, i: (0, 0)),
           pl.BlockSpec((D_MODEL, D_MODEL), lambda b, i: (0, 0))],
        out_specs=[pair_spec] * 6 + [tok_spec] * 2,
        out_shape=[pair_shape] * 6 + [tok_shape] * 2,
        compiler_params=_params(("arbitrary", "arbitrary")),
        name="rwkv_prep",
    )(p3, *vecs, w2p, a2p, g2p, ones_bd)


def _block_diag(x, lane_head0):
    return jnp.concatenate([jnp.where(lane_head0, x, 0.0), jnp.where(lane_head0, 0.0, x)], axis=0)


def _chunk_unit(r, k, v, lw, kn, b, state):
    n = 2 * CHUNK
    row = lax.broadcasted_iota(jnp.int32, (CHUNK, PAIR), 0)
    lane_head0 = lax.broadcasted_iota(jnp.int32, (CHUNK, PAIR), 1) < HEAD_DIM
    c = lw
    for s in (1, 2, 4, 8, 16, 32):
        c = c + jnp.where(row >= s, pltpu.roll(c, s, axis=0), 0.0)
    c_end = c[CHUNK - 1:CHUNK, :]
    e_neg = jnp.exp(-c)
    e_end = jnp.exp(c_end - c)
    a_t = _block_diag(-kn * jnp.exp(c - lw), lane_head0)
    r_t = _block_diag(r * jnp.exp(c), lane_head0)
    b_t = _block_diag(b * e_neg, lane_head0)
    k_t = _block_diag(k * e_neg, lane_head0)
    b_e = _block_diag(b * e_end, lane_head0)
    k_e = _block_diag(k * e_end, lane_head0)
    v_bd = _block_diag(v, lane_head0)

    yield
    g = _dot_nt(jnp.concatenate([a_t, r_t], axis=0), jnp.concatenate([b_t, k_t], axis=0))
    ri = lax.broadcasted_iota(jnp.int32, (n, n), 0)
    ci = lax.broadcasted_iota(jnp.int32, (n, n), 1)
    m_ab = jnp.where(ci < ri, g[:n, :n], 0.0)
    m_ak = jnp.where(ci < ri, g[:n, n:], 0.0)
    a_rb = jnp.where(ci <= ri, g[n:, :n], 0.0)
    a_rk = jnp.where(ci <= ri, g[n:, n:], 0.0)

    t_inv = jnp.where(ci == ri, 1.0, m_ab)
    yield
    power = _dot(m_ab, m_ab)
    w1 = _dot(m_ak, v_bd)
    for _ in range(4):
        yield
        prod = _dot(power, jnp.concatenate([power, t_inv], axis=1))
        power = prod[:, :n]
        t_inv = t_inv + prod[:, n:]
    yield
    t_inv = t_inv + _dot(power, t_inv)

    yield
    tw = _dot(t_inv, jnp.concatenate([w1, a_t], axis=1))
    u0 = tw[:, :n]
    a_hat = tw[:, n:]
    rhs = jnp.concatenate([jnp.concatenate([a_hat, u0], axis=1),
                           jnp.concatenate([jnp.zeros_like(v_bd), v_bd], axis=1)], axis=0)
    yield
    pq = _dot_tn(jnp.concatenate([b_e, k_e], axis=0), rhs)
    ry = _dot(jnp.concatenate([a_rb, a_rk], axis=1), rhs)
    decay_end = jnp.exp(jnp.broadcast_to(c_end, (n, n)))
    p_mat = pq[:, :n] + jnp.where(ci == ri, decay_end, 0.0)
    yield
    y_bd = _dot(r_t + ry[:, :n], state) + ry[:, n:]
    y = y_bd[:CHUNK] + y_bd[CHUNK:]
    new_state = _dot(p_mat, state) + pq[:, n:]
    return y, new_state


def _interleave(generators):
    results = [None] * len(generators)
    pending = list(range(len(generators)))
    while pending:
        still = []
        for idx in pending:
            try:
                next(generators[idx])
                still.append(idx)
            except StopIteration as stop:
                results[idx] = stop.value
        pending = still
    return results


def _rwkv_chunk_kernel(r_ref, k_ref, v_ref, lw_ref, kn_ref, b_ref, y_ref, state_scr, *, n_units):
    @pl.when(pl.program_id(0) == 0)
    def _():
        state_scr[...] = jnp.zeros_like(state_scr)

    def body(g, carry):
        units = [g * UNIT_UNROLL + j for j in range(UNIT_UNROLL)]
        idx = [(u // N_PAIRS, u % N_PAIRS) for u in units]
        args = [(r_ref[bi, pi], k_ref[bi, pi], v_ref[bi, pi], lw_ref[bi, pi], kn_ref[bi, pi],
                 b_ref[bi, pi], state_scr[u]) for u, (bi, pi) in zip(units, idx)]
        outs = _interleave([_chunk_unit(*a) for a in args])
        for u, (bi, pi), (y, new_state) in zip(units, idx, outs):
            y_ref[bi, pi] = y
            state_scr[u] = new_state
        return carry

    lax.fori_loop(0, n_units // UNIT_UNROLL, body, 0)


def _rwkv_chunk(r, k, v, lw, kn, b):
    bsz, _, lp, _ = r.shape
    n_units = bsz * N_PAIRS
    spec = pl.BlockSpec((bsz, N_PAIRS, CHUNK, PAIR), lambda c: (0, 0, c, 0))
    return pl.pallas_call(
        functools.partial(_rwkv_chunk_kernel, n_units=n_units),
        grid=(lp // CHUNK,),
        in_specs=[spec] * 6,
        out_specs=spec,
        out_shape=jax.ShapeDtypeStruct(r.shape, F32),
        scratch_shapes=[pltpu.VMEM((n_units, PAIR, PAIR), F32)],
        compiler_params=_params(("arbitrary",)),
        name="rwkv_chunk",
    )(r, k, v, lw, kn, b)


def _diff_attn_kernel(q_ref, k_ref, v_ref, lq1_ref, lk1_ref, lq2_ref, lk2_ref, subw_ref, o_ref,
                      m_scr, l_scr, acc_scr, *, tq, kv_tiles):
    i = pl.program_id(2)
    q = q_ref[0]
    head0 = lax.broadcasted_iota(jnp.int32, (tq, PAIR), 1) < HEAD_DIM
    zero = jnp.zeros_like(q)
    qq = jnp.concatenate([jnp.where(head0, q, zero), jnp.where(head0, zero, q)], axis=0)
    m_scr[...] = jnp.full_like(m_scr, -1e30)
    l_scr[...] = jnp.zeros_like(l_scr)
    acc_scr[...] = jnp.zeros_like(acc_scr)

    def step(j, width, masked):
        start = pl.multiple_of(j * tq, tq)
        kb = k_ref[0, pl.ds(start, width), :]
        vb = v_ref[0, pl.ds(start, width), :]
        s = _dot_nt(qq, kb)
        if masked:
            rr = lax.broadcasted_iota(jnp.int32, (2 * tq, width), 0) % tq
            cc = lax.broadcasted_iota(jnp.int32, (2 * tq, width), 1)
            s = jnp.where(cc <= rr, s, -1e30)
        m_prev = m_scr[...]
        m_next = jnp.maximum(m_prev, jnp.max(s, axis=1, keepdims=True))
        p = jnp.exp(s - jnp.tile(m_next, (1, width // LANE)))
        alpha = jnp.exp(m_prev - m_next)
        psum = p[:, :LANE]
        for c in range(1, width // LANE):
            psum = psum + p[:, c * LANE:(c + 1) * LANE]
        l_scr[...] = alpha * l_scr[...] + psum
        acc_scr[...] = alpha * acc_scr[...] + _dot(p, vb)
        m_scr[...] = m_next

    def big_body(j, carry):
        step(j * kv_tiles, kv_tiles * tq, False)
        return carry

    def small_body(j, carry):
        step(j, tq, False)
        return carry

    n_big = i // kv_tiles
    lax.fori_loop(0, n_big, big_body, 0)
    lax.fori_loop(n_big * kv_tiles, i, small_body, 0)
    step(i, tq, True)

    o = acc_scr[...] / jnp.sum(l_scr[...], axis=1, keepdims=True)
    lam = (jnp.exp(jnp.sum(lq1_ref[...] * lk1_ref[...], axis=1, keepdims=True))
           - jnp.exp(jnp.sum(lq2_ref[...] * lk2_ref[...], axis=1, keepdims=True)) + LAMBDA_INIT)
    od = o[:tq] - lam * o[tq:]
    o_ref[0] = (_rms(od, subw_ref[...], SUBLN_EPS) * (1.0 - LAMBDA_INIT)).astype(o_ref.dtype)


def _diff_attn(a3, lams, subw, *, tq, kv_tiles):
    bsz, lp, _ = a3.shape
    grid = (bsz, N_PAIRS, lp // tq)
    kv_spec_k = pl.BlockSpec((1, lp, PAIR), lambda b, h, i: (b, 0, N_PAIRS + h))
    kv_spec_v = pl.BlockSpec((1, lp, PAIR), lambda b, h, i: (b, 0, 2 * N_PAIRS + h))
    lam_spec = pl.BlockSpec((1, HEAD_DIM), lambda b, h, i: (0, 0))
    return pl.pallas_call(
        functools.partial(_diff_attn_kernel, tq=tq, kv_tiles=kv_tiles),
        grid=grid,
        in_specs=[pl.BlockSpec((1, tq, PAIR), lambda b, h, i: (b, i, h)), kv_spec_k, kv_spec_v,
                  lam_spec, lam_spec, lam_spec, lam_spec,
                  pl.BlockSpec((1, PAIR), lambda b, h, i: (0, 0))],
        out_specs=pl.BlockSpec((1, tq, PAIR), lambda b, h, i: (b, i, h)),
        out_shape=jax.ShapeDtypeStruct((bsz, lp, D_MODEL), BF16),
        scratch_shapes=[pltpu.VMEM((2 * tq, PAIR), F32)] * 3,
        compiler_params=_params(("arbitrary", "arbitrary", "arbitrary")),
        name="diff_attn",
    )(a3, a3, a3, *lams, subw)


def _merge_kernel(y_ref, bonus_ref, g_ref, od_ref, gate_r_ref, gate_d_ref, res_ref, lnw_ref, lnb_ref,
                  ones_ref, wo_r_ref, wo_d_ref, wout_ref, o_ref):
    ones_pair = ones_ref[...]
    parts = []
    for pair in range(N_PAIRS):
        y = y_ref[0, pair]
        mean = _dot(y, ones_pair) * (1.0 / HEAD_DIM)
        yc = y - mean
        var = _dot(yc * yc, ones_pair) * (1.0 / HEAD_DIM)
        parts.append(yc * lax.rsqrt(var + RWKV_LN_EPS))
    yn = jnp.concatenate(parts, axis=1)
    o_rwkv = (yn * lnw_ref[...] + lnb_ref[...] + bonus_ref[0]) * g_ref[0]
    br_rwkv = _dot(o_rwkv, wo_r_ref[...])
    br_diff = jnp.dot(od_ref[0], wo_d_ref[...], preferred_element_type=F32)
    merged = (_sigmoid(gate_r_ref[0].astype(F32)) * br_rwkv
              + _sigmoid(gate_d_ref[0].astype(F32)) * br_diff)
    o_ref[0] = res_ref[0] + _dot(merged, wout_ref[...])


def _merge(y, bonus, g, od, a3, res, lnw, lnb, ones_pair, wo_r, wo_d, wout, *, tm):
    bsz, lp, _ = res.shape
    grid = (bsz, lp // tm)
    tok = lambda c: pl.BlockSpec((1, tm, c), lambda b, i: (b, i, 0))
    vec = pl.BlockSpec((1, D_MODEL), lambda b, i: (0, 0))
    mat = pl.BlockSpec((D_MODEL, D_MODEL), lambda b, i: (0, 0))
    gate_r_spec = pl.BlockSpec((1, tm, D_MODEL), lambda b, i: (b, i, 3))
    gate_d_spec = pl.BlockSpec((1, tm, D_MODEL), lambda b, i: (b, i, 4))
    return pl.pallas_call(
        _merge_kernel,
        grid=grid,
        in_specs=[pl.BlockSpec((1, N_PAIRS, tm, PAIR), lambda b, i: (b, 0, i, 0)),
                  tok(D_MODEL), tok(D_MODEL), tok(D_MODEL), gate_r_spec, gate_d_spec, tok(D_MODEL), vec, vec,
                  pl.BlockSpec((PAIR, PAIR), lambda b, i: (0, 0)), mat, mat, mat],
        out_specs=tok(D_MODEL),
        out_shape=jax.ShapeDtypeStruct(res.shape, F32),
        compiler_params=_params(("arbitrary", "arbitrary")),
        name="merge",
    )(y, bonus, g, od, a3, a3, res, lnw, lnb, ones_pair, wo_r, wo_d, wout)


def _mlp_kernel(x_ref, nw_ref, w1_ref, w2_ref, fw_ref, o_ref, *, n_ff_chunks):
    x = x_ref[...]
    h = _rms(x, nw_ref[...], NORM_EPS).astype(BF16)
    acc = x
    ff = D_FF // n_ff_chunks
    for c in range(n_ff_chunks):
        z = jnp.dot(h, w1_ref[:, c * ff:(c + 1) * ff], preferred_element_type=F32)
        act = jnp.square(jnp.maximum(z, 0.0))
        acc = acc + _dot(act, w2_ref[c * ff:(c + 1) * ff, :])
    o_ref[...] = _rms(acc, fw_ref[...], NORM_EPS)


def _mlp(xf, nw, w1, w2, fw, *, tm):
    t_tokens = xf.shape[0]
    tok = pl.BlockSpec((tm, D_MODEL), lambda i: (i, 0))
    vec = pl.BlockSpec((1, D_MODEL), lambda i: (0, 0))
    return pl.pallas_call(
        functools.partial(_mlp_kernel, n_ff_chunks=4),
        grid=(t_tokens // tm,),
        in_specs=[tok, vec, pl.BlockSpec((D_MODEL, D_FF), lambda i: (0, 0)),
                  pl.BlockSpec((D_FF, D_MODEL), lambda i: (0, 0)), vec],
        out_specs=tok,
        out_shape=jax.ShapeDtypeStruct(xf.shape, F32),
        compiler_params=_params(("arbitrary",)),
        name="mlp",
    )(xf, nw, w1, w2, fw)


def _pad_cols(a, width):
    return jnp.pad(a, ((0, 0), (0, width - a.shape[1])))


def _pad_rows(a, height):
    return jnp.pad(a, ((0, height - a.shape[0]), (0, 0)))


def _rope_tables(lp):
    inv = ROPE_THETA ** (-jnp.arange(0, ROPE_DIM, 2, dtype=F32) / ROPE_DIM)
    ang = jnp.arange(lp, dtype=F32)[:, None] * inv[None, :]
    cos, sin = jnp.cos(ang), jnp.sin(ang)
    rest = HEAD_DIM - ROPE_DIM
    ones = jnp.ones((lp, rest), F32)
    zeros = jnp.zeros((lp, rest), F32)
    zh = jnp.zeros_like(sin)
    c = jnp.concatenate([cos, cos, ones], axis=1)
    s1 = jnp.concatenate([-sin, zh, zeros], axis=1)
    s2 = jnp.concatenate([zh, sin, zeros], axis=1)
    return tuple(jnp.tile(t, (1, LANE // HEAD_DIM)) for t in (c, s1, s2))


def _pick(n, candidates):
    for c in candidates:
        if n % c == 0:
            return c
    raise ValueError(n)


def kernel(x, meta_tokens, norm_mix_w, w_in, rwkv_mu, rwkv_w0, rwkv_w2, rwkv_a0, rwkv_a2, rwkv_g2, rwkv_k_k, rwkv_k_a, rwkv_r_k, rwkv_ln_w, rwkv_ln_b, rwkv_w_o, diff_lq1, diff_lk1, diff_lq2, diff_lk2, diff_subln_w, diff_w_o, w_out, norm_mlp_w, mlp_w1, mlp_w2, final_norm_w):
    bsz, seq, _ = x.shape
    seqlen = seq + N_META_TOKENS
    lp = -(-seqlen // SEQ_ALIGN) * SEQ_ALIGN
    tm_proj = _pick(lp, (768, 512, 256))
    tm_tok = 256

    meta = jnp.broadcast_to(meta_tokens[None].astype(x.dtype), (bsz, N_META_TOKENS, D_MODEL))
    h_res = jnp.concatenate([meta, x, jnp.zeros((bsz, lp - seqlen, D_MODEL), x.dtype)], axis=1)
    hf = h_res.reshape(bsz * lp, D_MODEL)

    w = w_in[0]
    c3 = 3 * D_MODEL
    w_rwkv = jnp.concatenate(
        [w[:, :c3], _pad_cols(w[:, c3:c3 + DECAY_LORA], LANE),
         _pad_cols(w[:, c3 + DECAY_LORA:c3 + DECAY_LORA + AAA_LORA], LANE),
         _pad_cols(w[:, c3 + DECAY_LORA + AAA_LORA:c3 + DECAY_LORA + AAA_LORA + GATE_LORA], GATE_LORA_PAD)],
        axis=1).astype(BF16)
    rwkv_cols = c3 + DECAY_LORA + AAA_LORA + GATE_LORA
    w_diff = w[:, rwkv_cols:].astype(BF16)
    mu = rwkv_mu[0][None, :]
    mu_p = jnp.concatenate(
        [mu[:, :c3], _pad_cols(mu[:, c3:c3 + DECAY_LORA], LANE),
         _pad_cols(mu[:, c3 + DECAY_LORA:c3 + DECAY_LORA + AAA_LORA], LANE),
         _pad_cols(mu[:, c3 + DECAY_LORA + AAA_LORA:], GATE_LORA_PAD)], axis=1)
    nw = norm_mix_w[0][None, :]

    tiles_per_seq = lp // tm_proj
    p_rwkv = _in_proj(hf, nw, w_rwkv, mu=mu_p, out_dtype=F32, tm=tm_proj, tn=512,
                      tiles_per_seq=tiles_per_seq)
    a_diff = _in_proj(hf, nw, w_diff, rope=_rope_tables(lp), out_dtype=BF16, tm=tm_proj, tn=512,
                      tiles_per_seq=tiles_per_seq)

    head_id = jnp.arange(D_MODEL) // HEAD_DIM
    ones_bd = (head_id[:, None] == head_id[None, :]).astype(BF16)
    vecs = (rwkv_w0[0][None, :], rwkv_a0[0][None, :], rwkv_k_k[0][None, :], rwkv_k_a[0][None, :],
            rwkv_r_k[0].reshape(1, D_MODEL))
    r, kmod, v, lw, kn, b, bonus, g = _rwkv_prep(
        p_rwkv.reshape(bsz, lp, RWKV_COLS_PAD), vecs,
        _pad_rows(rwkv_w2[0], LANE).astype(BF16), _pad_rows(rwkv_a2[0], LANE).astype(BF16),
        _pad_rows(rwkv_g2[0], GATE_LORA_PAD).astype(BF16), ones_bd, tm=tm_tok)
    y = _rwkv_chunk(r, kmod, v, lw, kn, b)

    a3 = a_diff.reshape(bsz, lp, DIFF_GATE_COLS)
    o_diff = _diff_attn(a3, (diff_lq1, diff_lk1, diff_lq2, diff_lk2), diff_subln_w[0][None, :], tq=256,
                        kv_tiles=2)

    h_mid = _merge(y, bonus, g, o_diff, a3, h_res, rwkv_ln_w[0][None, :], rwkv_ln_b[0][None, :],
                   ones_bd[:PAIR, :PAIR], rwkv_w_o[0].astype(BF16), diff_w_o[0].astype(BF16),
                   w_out[0].astype(BF16), tm=tm_tok)
    out = _mlp(h_mid.reshape(bsz * lp, D_MODEL), norm_mlp_w[0][None, :], mlp_w1[0].astype(BF16),
               mlp_w2[0].astype(BF16), final_norm_w[None, :], tm=tm_tok)
    return out.reshape(bsz, lp, D_MODEL)[:, N_META_TOKENS:seqlen]
```

```python
import functools
import math

import jax
import jax.numpy as jnp
from jax import lax
from jax.experimental import pallas as pl
from jax.experimental.pallas import tpu as pltpu

F32 = jnp.float32
BF16 = jnp.bfloat16

D_MODEL = 1024
N_META_TOKENS = 16
HEAD_DIM = 64
PAIR = 2 * HEAD_DIM
N_PAIRS = D_MODEL // PAIR
DECAY_LORA = 64
AAA_LORA = 64
GATE_LORA = 160
RWKV_LN_EPS = 64e-5
ROPE_THETA = 500000.0
ROPE_DIM = HEAD_DIM // 4
ROPE_HALF = ROPE_DIM // 2
D_FF = 4 * D_MODEL
NORM_EPS = 1e-5
SUBLN_EPS = 1e-5
LAMBDA_INIT = 0.8 - 0.6 * math.exp(-0.3 * 0)

LANE = 128
SUBLANE = 8
CHUNK = 64
UNIT_UNROLL = 8
ATTN_HEADS_PER_STEP = 4
Q_SCALE = HEAD_DIM ** -0.5 * math.log2(math.e)
SEQ_ALIGN = 256
VMEM_LIMIT = 56 * 1024 * 1024

LORA_W_OFF = 3 * D_MODEL
LORA_A_OFF = LORA_W_OFF + LANE
LORA_G_OFF = LORA_A_OFF + LANE
GATE_LORA_PAD = 2 * LANE
RWKV_COLS_PAD = LORA_G_OFF + GATE_LORA_PAD
DIFF_GATE_COLS = 3 * D_MODEL + 2 * D_MODEL


def _params(sem):
    return pltpu.CompilerParams(dimension_semantics=sem, vmem_limit_bytes=VMEM_LIMIT)


def _rms(x, w, eps):
    return x * lax.rsqrt(jnp.mean(x * x, axis=-1, keepdims=True) + eps) * w


def _sigmoid(x):
    return 1.0 / (1.0 + jnp.exp(-x))


def _dot(a, b):
    return jnp.dot(a.astype(BF16), b.astype(BF16), preferred_element_type=F32)


def _dot_nt(a, b):
    return lax.dot_general(a.astype(BF16), b.astype(BF16), (((1,), (1,)), ((), ())),
                           preferred_element_type=F32)


def _dot_tn(a, b):
    return lax.dot_general(a.astype(BF16), b.astype(BF16), (((0,), (0,)), ((), ())),
                           preferred_element_type=F32)


def _in_proj_kernel(*refs, shift, n_rope_tiles, n_q_tiles, tiles_per_seq):
    if shift:
        x_ref, xprev_ref, nw_ref, w_ref, mu_ref, o_ref, xn_scr, xpn_scr = refs
    else:
        x_ref, nw_ref, w_ref, cos_ref, s1_ref, s2_ref, o_ref, xn_scr = refs
    i = pl.program_id(0)
    j = pl.program_id(1)

    @pl.when(j == 0)
    def _():
        xn_scr[...] = _rms(x_ref[...], nw_ref[...], NORM_EPS).astype(BF16)
        if shift:
            xpn_scr[...] = _rms(xprev_ref[...], nw_ref[...], NORM_EPS).astype(BF16)

    p = jnp.dot(xn_scr[...], w_ref[...], preferred_element_type=F32)
    tm, tn = p.shape
    if shift:
        pp = jnp.dot(xpn_scr[...], w_ref[...], preferred_element_type=F32)
        first = (i % tiles_per_seq) == 0
        prev_row = jnp.where(first, 0.0, pp[SUBLANE - 1:SUBLANE, :])
        row = lax.broadcasted_iota(jnp.int32, (tm, tn), 0)
        shifted = jnp.where(row == 0, prev_row, pltpu.roll(p, 1, axis=0))
        o_ref[...] = (p + (shifted - p) * mu_ref[...]).astype(o_ref.dtype)
    else:
        @pl.when(j >= n_rope_tiles)
        def _():
            o_ref[...] = p.astype(o_ref.dtype)

        @pl.when(j < n_rope_tiles)
        def _():
            reps = tn // LANE
            cos = jnp.tile(cos_ref[...], (1, reps))
            s1 = jnp.tile(s1_ref[...], (1, reps))
            s2 = jnp.tile(s2_ref[...], (1, reps))
            nxt = pltpu.roll(p, tn - ROPE_HALF, axis=1)
            prv = pltpu.roll(p, ROPE_HALF, axis=1)
            scale = jnp.where(j < n_q_tiles, Q_SCALE, 1.0)
            o_ref[...] = ((p * cos + nxt * s1 + prv * s2) * scale).astype(o_ref.dtype)


def _in_proj(xf, nw, w, *, mu=None, rope=None, out_dtype, tm, tn, tiles_per_seq):
    t_tokens = xf.shape[0]
    n_cols = w.shape[1]
    shift = mu is not None
    grid = (t_tokens // tm, n_cols // tn)
    x_spec = pl.BlockSpec((tm, D_MODEL), lambda i, j: (i, 0))
    nw_spec = pl.BlockSpec((1, D_MODEL), lambda i, j: (0, 0))
    w_spec = pl.BlockSpec((D_MODEL, tn), lambda i, j: (0, j))
    scratch = [pltpu.VMEM((tm, D_MODEL), BF16)]
    if shift:
        rows8 = tm // SUBLANE
        prev_spec = pl.BlockSpec((SUBLANE, D_MODEL), lambda i, j: (jnp.maximum(i * rows8 - 1, 0), 0))
        mu_spec = pl.BlockSpec((1, tn), lambda i, j: (0, j))
        in_specs = [x_spec, prev_spec, nw_spec, w_spec, mu_spec]
        args = (xf, xf, nw, w, mu)
        scratch.append(pltpu.VMEM((SUBLANE, D_MODEL), BF16))
        n_rope_tiles = n_q_tiles = 0
    else:
        tab_spec = pl.BlockSpec((tm, LANE), lambda i, j: (i % tiles_per_seq, 0))
        in_specs = [x_spec, nw_spec, w_spec, tab_spec, tab_spec, tab_spec]
        args = (xf, nw, w) + tuple(rope)
        n_q_tiles = D_MODEL // tn
        n_rope_tiles = 2 * n_q_tiles
    kern = functools.partial(_in_proj_kernel, shift=shift, n_rope_tiles=n_rope_tiles,
                             n_q_tiles=n_q_tiles, tiles_per_seq=tiles_per_seq)
    return pl.pallas_call(
        kern,
        grid=grid,
        in_specs=in_specs,
        out_specs=pl.BlockSpec((tm, tn), lambda i, j: (i, j)),
        out_shape=jax.ShapeDtypeStruct((t_tokens, n_cols), out_dtype),
        scratch_shapes=scratch,
        compiler_params=_params(("arbitrary", "arbitrary")),
        name="in_proj_rwkv" if shift else "in_proj_diff",
    )(*args)


def _rwkv_prep_kernel(p_ref, w0_ref, a0_ref, kk_ref, ka_ref, rk_ref, w2_ref, a2_ref, g2_ref,
                      ones_ref, r_o, k_o, v_o, lw_o, kn_o, b_o, bonus_o, g_o):
    r = p_ref[0, :, 0:D_MODEL]
    k = p_ref[0, :, D_MODEL:2 * D_MODEL]
    v = p_ref[0, :, 2 * D_MODEL:3 * D_MODEL]
    wd = p_ref[0, :, LORA_W_OFF:LORA_W_OFF + LANE]
    ad = p_ref[0, :, LORA_A_OFF:LORA_A_OFF + LANE]
    gd = p_ref[0, :, LORA_G_OFF:LORA_G_OFF + GATE_LORA_PAD]
    ones_bd = ones_ref[...]

    z = w0_ref[...] + _dot(jnp.tanh(wd), w2_ref[...])
    softplus_neg = jnp.maximum(-z, 0.0) + jnp.log(1.0 + jnp.exp(-jnp.abs(z)))
    log_decay = -jnp.exp(-softplus_neg - 0.5)
    a_rate = _sigmoid(a0_ref[...] + _dot(ad, a2_ref[...]))
    gate = _dot(_sigmoid(gd), g2_ref[...])

    kraw = k * kk_ref[...]
    sumsq = _dot(kraw * kraw, ones_bd)
    knorm = kraw / jnp.maximum(jnp.sqrt(sumsq), 1e-12)
    kmod = k * (1.0 + (a_rate - 1.0) * ka_ref[...])
    bonus = _dot(r * kmod * rk_ref[...], ones_bd) * v

    bonus_o[0] = bonus
    g_o[0] = gate
    for pair in range(N_PAIRS):
        sl = slice(pair * PAIR, (pair + 1) * PAIR)
        r_o[0, pair] = r[:, sl]
        k_o[0, pair] = kmod[:, sl]
        v_o[0, pair] = v[:, sl]
        lw_o[0, pair] = log_decay[:, sl]
        kn_o[0, pair] = knorm[:, sl]
        b_o[0, pair] = (knorm * a_rate)[:, sl]


def _rwkv_prep(p3, vecs, w2p, a2p, g2p, ones_bd, *, tm):
    bsz, lp, _ = p3.shape
    grid = (bsz, lp // tm)
    vec_spec = pl.BlockSpec((1, D_MODEL), lambda b, i: (0, 0))
    pair_spec = pl.BlockSpec((1, N_PAIRS, tm, PAIR), lambda b, i: (b, 0, i, 0))
    tok_spec = pl.BlockSpec((1, tm, D_MODEL), lambda b, i: (b, i, 0))
    pair_shape = jax.ShapeDtypeStruct((bsz, N_PAIRS, lp, PAIR), F32)
    tok_shape = jax.ShapeDtypeStruct((bsz, lp, D_MODEL), F32)
    return pl.pallas_call(
        _rwkv_prep_kernel,
        grid=grid,
        in_specs=[pl.BlockSpec((1, tm, RWKV_COLS_PAD), lambda b, i: (b, i, 0))]
        + [vec_spec] * 5
        + [pl.BlockSpec((LANE, D_MODEL), lambda b, i: (0, 0)),
           pl.BlockSpec((LANE, D_MODEL), lambda b, i: (0, 0)),
           pl.BlockSpec((GATE_LORA_PAD, D_MODEL), lambda b, i: (0, 0)),
           pl.BlockSpec((D_MODEL, D_MODEL), lambda b, i: (0, 0))],
        out_specs=[pair_spec] * 6 + [tok_spec] * 2,
        out_shape=[pair_shape] * 6 + [tok_shape] * 2,
        compiler_params=_params(("arbitrary", "arbitrary")),
        name="rwkv_prep",
    )(p3, *vecs, w2p, a2p, g2p, ones_bd)


def _block_diag(x, lane_head0):
    return jnp.concatenate([jnp.where(lane_head0, x, 0.0), jnp.where(lane_head0, 0.0, x)], axis=0)


def _chunk_unit(r, k, v, lw, kn, b, state):
    n = 2 * CHUNK
    row = lax.broadcasted_iota(jnp.int32, (CHUNK, PAIR), 0)
    lane_head0 = lax.broadcasted_iota(jnp.int32, (CHUNK, PAIR), 1) < HEAD_DIM
    c = lw
    for s in (1, 2, 4, 8, 16, 32):
        c = c + jnp.where(row >= s, pltpu.roll(c, s, axis=0), 0.0)
    c_end = c[CHUNK - 1:CHUNK, :]
    e_neg = jnp.exp(-c)
    e_end = jnp.exp(c_end - c)
    a_t = _block_diag(-kn * jnp.exp(c - lw), lane_head0)
    r_t = _block_diag(r * jnp.exp(c), lane_head0)
    b_t = _block_diag(b * e_neg, lane_head0)
    k_t = _block_diag(k * e_neg, lane_head0)
    b_e = _block_diag(b * e_end, lane_head0)
    k_e = _block_diag(k * e_end, lane_head0)
    v_bd = _block_diag(v, lane_head0)

    yield
    g = _dot_nt(jnp.concatenate([a_t, r_t], axis=0), jnp.concatenate([b_t, k_t], axis=0))
    ri = lax.broadcasted_iota(jnp.int32, (n, n), 0)
    ci = lax.broadcasted_iota(jnp.int32, (n, n), 1)
    m_ab = jnp.where(ci < ri, g[:n, :n], 0.0)
    m_ak = jnp.where(ci < ri, g[:n, n:], 0.0)
    a_rb = jnp.where(ci <= ri, g[n:, :n], 0.0)
    a_rk = jnp.where(ci <= ri, g[n:, n:], 0.0)

    t_inv = jnp.where(ci == ri, 1.0, m_ab)
    yield
    power = _dot(m_ab, m_ab)
    w1 = _dot(m_ak, v_bd)
    for _ in range(4):
        yield
        prod = _dot(power, jnp.concatenate([power, t_inv], axis=1))
        power = prod[:, :n]
        t_inv = t_inv + prod[:, n:]
    yield
    t_inv = t_inv + _dot(power, t_inv)

    yield
    tw = _dot(t_inv, jnp.concatenate([w1, a_t], axis=1))
    u0 = tw[:, :n]
    a_hat = tw[:, n:]
    rhs = jnp.concatenate([jnp.concatenate([a_hat, u0], axis=1),
                           jnp.concatenate([jnp.zeros_like(v_bd), v_bd], axis=1)], axis=0)
    yield
    pq = _dot_tn(jnp.concatenate([b_e, k_e], axis=0), rhs)
    ry = _dot(jnp.concatenate([a_rb, a_rk], axis=1), rhs)
    decay_end = jnp.exp(jnp.broadcast_to(c_end, (n, n)))
    p_mat = pq[:, :n] + jnp.where(ci == ri, decay_end, 0.0)
    yield
    y_bd = _dot(r_t + ry[:, :n], state) + ry[:, n:]
    y = y_bd[:CHUNK] + y_bd[CHUNK:]
    new_state = _dot(p_mat, state) + pq[:, n:]
    return y, new_state


def _interleave(generators):
    results = [None] * len(generators)
    pending = list(range(len(generators)))
    while pending:
        still = []
        for idx in pending:
            try:
                next(generators[idx])
                still.append(idx)
            except StopIteration as stop:
                results[idx] = stop.value
        pending = still
    return results


def _rwkv_chunk_kernel(r_ref, k_ref, v_ref, lw_ref, kn_ref, b_ref, y_ref, state_scr, *, n_units):
    @pl.when(pl.program_id(0) == 0)
    def _():
        state_scr[...] = jnp.zeros_like(state_scr)

    def body(g, carry):
        units = [g * UNIT_UNROLL + j for j in range(UNIT_UNROLL)]
        idx = [(u // N_PAIRS, u % N_PAIRS) for u in units]
        args = [(r_ref[bi, pi], k_ref[bi, pi], v_ref[bi, pi], lw_ref[bi, pi], kn_ref[bi, pi],
                 b_ref[bi, pi], state_scr[u]) for u, (bi, pi) in zip(units, idx)]
        outs = _interleave([_chunk_unit(*a) for a in args])
        for u, (bi, pi), (y, new_state) in zip(units, idx, outs):
            y_ref[bi, pi] = y
            state_scr[u] = new_state
        return carry

    lax.fori_loop(0, n_units // UNIT_UNROLL, body, 0)


def _rwkv_chunk(r, k, v, lw, kn, b):
    bsz, _, lp, _ = r.shape
    n_units = bsz * N_PAIRS
    spec = pl.BlockSpec((bsz, N_PAIRS, CHUNK, PAIR), lambda c: (0, 0, c, 0))
    return pl.pallas_call(
        functools.partial(_rwkv_chunk_kernel, n_units=n_units),
        grid=(lp // CHUNK,),
        in_specs=[spec] * 6,
        out_specs=spec,
        out_shape=jax.ShapeDtypeStruct(r.shape, F32),
        scratch_shapes=[pltpu.VMEM((n_units, PAIR, PAIR), F32)],
        compiler_params=_params(("arbitrary",)),
        name="rwkv_chunk",
    )(r, k, v, lw, kn, b)


def _diff_attn_kernel(q_ref, k_ref, v_ref, lq1_ref, lk1_ref, lq2_ref, lk2_ref, subw_ref, o_ref,
                      m_scr, acc_scr, *, tq, kv_tiles):
    i = pl.program_id(2)
    head0 = lax.broadcasted_iota(jnp.int32, (tq, PAIR), 1) < HEAD_DIM
    qq = []
    for t in range(ATTN_HEADS_PER_STEP):
        q = q_ref[0, :, t * PAIR:(t + 1) * PAIR]
        zero = jnp.zeros_like(q)
        qq.append(jnp.concatenate([jnp.where(head0, q, zero), jnp.where(head0, zero, q)], axis=0))
    m_scr[...] = jnp.full_like(m_scr, -1e30)
    acc_scr[...] = jnp.zeros_like(acc_scr)

    def step(j, width, masked):
        start = pl.multiple_of(j * tq, tq)
        ones = jnp.ones((width, PAIR), BF16)
        def scores(t):
            return _dot_nt(qq[t], k_ref[0, pl.ds(start, width), t * PAIR:(t + 1) * PAIR])

        s_next = scores(0)
        for t in range(ATTN_HEADS_PER_STEP):
            s = s_next
            if t + 1 < ATTN_HEADS_PER_STEP:
                s_next = scores(t + 1)
            if masked:
                rr = lax.broadcasted_iota(jnp.int32, (2 * tq, width), 0) % tq
                cc = lax.broadcasted_iota(jnp.int32, (2 * tq, width), 1)
                s = jnp.where(cc <= rr, s, -1e30)
            m_prev = m_scr[t]
            m_next = jnp.maximum(m_prev, jnp.max(s, axis=1, keepdims=True))
            p = jnp.exp2(s - jnp.tile(m_next, (1, width // LANE))).astype(BF16)
            alpha = jnp.exp2(m_prev - m_next)
            v_ext = jnp.concatenate([v_ref[0, pl.ds(start, width), t * PAIR:(t + 1) * PAIR], ones], axis=1)
            acc_scr[t] = jnp.tile(alpha, (1, 2)) * acc_scr[t] + jnp.dot(p, v_ext, preferred_element_type=F32)
            m_scr[t] = m_next

    def big_body(j, carry):
        step(j * kv_tiles, kv_tiles * tq, False)
        return carry

    def small_body(j, carry):
        step(j, tq, False)
        return carry

    n_big = i // kv_tiles
    lax.fori_loop(0, n_big, big_body, 0)
    lax.fori_loop(n_big * kv_tiles, i, small_body, 0)
    step(i, tq, True)

    lam = (jnp.exp(jnp.sum(lq1_ref[...] * lk1_ref[...], axis=1, keepdims=True))
           - jnp.exp(jnp.sum(lq2_ref[...] * lk2_ref[...], axis=1, keepdims=True)) + LAMBDA_INIT)
    for t in range(ATTN_HEADS_PER_STEP):
        acc = acc_scr[t]
        o = acc[:, :PAIR] / acc[:, PAIR:]
        od = o[:tq] - lam * o[tq:]
        o_ref[0, :, t * PAIR:(t + 1) * PAIR] = (
            _rms(od, subw_ref[...], SUBLN_EPS) * (1.0 - LAMBDA_INIT)).astype(o_ref.dtype)


def _diff_attn(a3, lams, subw, *, tq, kv_tiles):
    bsz, lp, _ = a3.shape
    width = ATTN_HEADS_PER_STEP * PAIR
    n_groups = D_MODEL // width
    grid = (bsz, n_groups, lp // tq)
    q_spec = pl.BlockSpec((1, tq, width), lambda b, g, i: (b, i, g))
    k_spec = pl.BlockSpec((1, lp, width), lambda b, g, i: (b, 0, n_groups + g))
    v_spec = pl.BlockSpec((1, lp, width), lambda b, g, i: (b, 0, 2 * n_groups + g))
    lam_spec = pl.BlockSpec((1, HEAD_DIM), lambda b, g, i: (0, 0))
    return pl.pallas_call(
        functools.partial(_diff_attn_kernel, tq=tq, kv_tiles=kv_tiles),
        grid=grid,
        in_specs=[q_spec, k_spec, v_spec, lam_spec, lam_spec, lam_spec, lam_spec,
                  pl.BlockSpec((1, PAIR), lambda b, g, i: (0, 0))],
        out_specs=q_spec,
        out_shape=jax.ShapeDtypeStruct((bsz, lp, D_MODEL), BF16),
        scratch_shapes=[pltpu.VMEM((ATTN_HEADS_PER_STEP, 2 * tq, PAIR), F32),
                        pltpu.VMEM((ATTN_HEADS_PER_STEP, 2 * tq, 2 * PAIR), F32)],
        compiler_params=_params(("arbitrary", "arbitrary", "arbitrary")),
        name="diff_attn",
    )(a3, a3, a3, *lams, subw)


def _merge_kernel(y_ref, bonus_ref, g_ref, od_ref, gate_r_ref, gate_d_ref, res_ref, lnw_ref, lnb_ref,
                  ones_ref, wo_r_ref, wo_d_ref, wout_ref, o_ref):
    ones_pair = ones_ref[...]
    parts = []
    for pair in range(N_PAIRS):
        y = y_ref[0, pair]
        mean = _dot(y, ones_pair) * (1.0 / HEAD_DIM)
        yc = y - mean
        var = _dot(yc * yc, ones_pair) * (1.0 / HEAD_DIM)
        parts.append(yc * lax.rsqrt(var + RWKV_LN_EPS))
    yn = jnp.concatenate(parts, axis=1)
    o_rwkv = (yn * lnw_ref[...] + lnb_ref[...] + bonus_ref[0]) * g_ref[0]
    br_rwkv = _dot(o_rwkv, wo_r_ref[...])
    br_diff = jnp.dot(od_ref[0], wo_d_ref[...], preferred_element_type=F32)
    merged = (_sigmoid(gate_r_ref[0].astype(F32)) * br_rwkv
              + _sigmoid(gate_d_ref[0].astype(F32)) * br_diff)
    o_ref[0] = res_ref[0] + _dot(merged, wout_ref[...])


def _merge(y, bonus, g, od, a3, res, lnw, lnb, ones_pair, wo_r, wo_d, wout, *, tm):
    bsz, lp, _ = res.shape
    grid = (bsz, lp // tm)
    tok = lambda c: pl.BlockSpec((1, tm, c), lambda b, i: (b, i, 0))
    vec = pl.BlockSpec((1, D_MODEL), lambda b, i: (0, 0))
    mat = pl.BlockSpec((D_MODEL, D_MODEL), lambda b, i: (0, 0))
    gate_r_spec = pl.BlockSpec((1, tm, D_MODEL), lambda b, i: (b, i, 3))
    gate_d_spec = pl.BlockSpec((1, tm, D_MODEL), lambda b, i: (b, i, 4))
    return pl.pallas_call(
        _merge_kernel,
        grid=grid,
        in_specs=[pl.BlockSpec((1, N_PAIRS, tm, PAIR), lambda b, i: (b, 0, i, 0)),
                  tok(D_MODEL), tok(D_MODEL), tok(D_MODEL), gate_r_spec, gate_d_spec, tok(D_MODEL), vec, vec,
                  pl.BlockSpec((PAIR, PAIR), lambda b, i: (0, 0)), mat, mat, mat],
        out_specs=tok(D_MODEL),
        out_shape=jax.ShapeDtypeStruct(res.shape, F32),
        compiler_params=_params(("arbitrary", "arbitrary")),
        name="merge",
    )(y, bonus, g, od, a3, a3, res, lnw, lnb, ones_pair, wo_r, wo_d, wout)


def _mlp_kernel(x_ref, nw_ref, w1_ref, w2_ref, fw_ref, o_ref, *, n_ff_chunks):
    x = x_ref[...]
    h = _rms(x, nw_ref[...], NORM_EPS).astype(BF16)
    acc = x
    ff = D_FF // n_ff_chunks
    for c in range(n_ff_chunks):
        z = jnp.dot(h, w1_ref[:, c * ff:(c + 1) * ff], preferred_element_type=F32)
        act = jnp.square(jnp.maximum(z, 0.0))
        acc = acc + _dot(act, w2_ref[c * ff:(c + 1) * ff, :])
    o_ref[...] = _rms(acc, fw_ref[...], NORM_EPS)


def _mlp(xf, nw, w1, w2, fw, *, tm):
    t_tokens = xf.shape[0]
    tok = pl.BlockSpec((tm, D_MODEL), lambda i: (i, 0))
    vec = pl.BlockSpec((1, D_MODEL), lambda i: (0, 0))
    return pl.pallas_call(
        functools.partial(_mlp_kernel, n_ff_chunks=4),
        grid=(t_tokens // tm,),
        in_specs=[tok, vec, pl.BlockSpec((D_MODEL, D_FF), lambda i: (0, 0)),
                  pl.BlockSpec((D_FF, D_MODEL), lambda i: (0, 0)), vec],
        out_specs=tok,
        out_shape=jax.ShapeDtypeStruct(xf.shape, F32),
        compiler_params=_params(("arbitrary",)),
        name="mlp",
    )(xf, nw, w1, w2, fw)


def _pad_cols(a, width):
    return jnp.pad(a, ((0, 0), (0, width - a.shape[1])))


def _pad_rows(a, height):
    return jnp.pad(a, ((0, height - a.shape[0]), (0, 0)))


def _rope_tables(lp):
    inv = ROPE_THETA ** (-jnp.arange(0, ROPE_DIM, 2, dtype=F32) / ROPE_DIM)
    ang = jnp.arange(lp, dtype=F32)[:, None] * inv[None, :]
    cos, sin = jnp.cos(ang), jnp.sin(ang)
    rest = HEAD_DIM - ROPE_DIM
    ones = jnp.ones((lp, rest), F32)
    zeros = jnp.zeros((lp, rest), F32)
    zh = jnp.zeros_like(sin)
    c = jnp.concatenate([cos, cos, ones], axis=1)
    s1 = jnp.concatenate([-sin, zh, zeros], axis=1)
    s2 = jnp.concatenate([zh, sin, zeros], axis=1)
    return tuple(jnp.tile(t, (1, LANE // HEAD_DIM)) for t in (c, s1, s2))


def _pick(n, candidates):
    for c in candidates:
        if n % c == 0:
            return c
    raise ValueError(n)


def kernel(x, meta_tokens, norm_mix_w, w_in, rwkv_mu, rwkv_w0, rwkv_w2, rwkv_a0, rwkv_a2, rwkv_g2, rwkv_k_k, rwkv_k_a, rwkv_r_k, rwkv_ln_w, rwkv_ln_b, rwkv_w_o, diff_lq1, diff_lk1, diff_lq2, diff_lk2, diff_subln_w, diff_w_o, w_out, norm_mlp_w, mlp_w1, mlp_w2, final_norm_w):
    bsz, seq, _ = x.shape
    seqlen = seq + N_META_TOKENS
    lp = -(-seqlen // SEQ_ALIGN) * SEQ_ALIGN
    tm_proj = _pick(lp, (768, 512, 256))
    tm_tok = 256

    meta = jnp.broadcast_to(meta_tokens[None].astype(x.dtype), (bsz, N_META_TOKENS, D_MODEL))
    h_res = jnp.concatenate([meta, x, jnp.zeros((bsz, lp - seqlen, D_MODEL), x.dtype)], axis=1)
    hf = h_res.reshape(bsz * lp, D_MODEL)

    w = w_in[0]
    c3 = 3 * D_MODEL
    w_rwkv = jnp.concatenate(
        [w[:, :c3], _pad_cols(w[:, c3:c3 + DECAY_LORA], LANE),
         _pad_cols(w[:, c3 + DECAY_LORA:c3 + DECAY_LORA + AAA_LORA], LANE),
         _pad_cols(w[:, c3 + DECAY_LORA + AAA_LORA:c3 + DECAY_LORA + AAA_LORA + GATE_LORA], GATE_LORA_PAD)],
        axis=1).astype(BF16)
    rwkv_cols = c3 + DECAY_LORA + AAA_LORA + GATE_LORA
    w_diff = w[:, rwkv_cols:].astype(BF16)
    mu = rwkv_mu[0][None, :]
    mu_p = jnp.concatenate(
        [mu[:, :c3], _pad_cols(mu[:, c3:c3 + DECAY_LORA], LANE),
         _pad_cols(mu[:, c3 + DECAY_LORA:c3 + DECAY_LORA + AAA_LORA], LANE),
         _pad_cols(mu[:, c3 + DECAY_LORA + AAA_LORA:], GATE_LORA_PAD)], axis=1)
    nw = norm_mix_w[0][None, :]

    tiles_per_seq = lp // tm_proj
    p_rwkv = _in_proj(hf, nw, w_rwkv, mu=mu_p, out_dtype=F32, tm=tm_proj, tn=512,
                      tiles_per_seq=tiles_per_seq)
    a_diff = _in_proj(hf, nw, w_diff, rope=_rope_tables(lp), out_dtype=BF16, tm=tm_proj, tn=512,
                      tiles_per_seq=tiles_per_seq)

    head_id = jnp.arange(D_MODEL) // HEAD_DIM
    ones_bd = (head_id[:, None] == head_id[None, :]).astype(BF16)
    vecs = (rwkv_w0[0][None, :], rwkv_a0[0][None, :], rwkv_k_k[0][None, :], rwkv_k_a[0][None, :],
            rwkv_r_k[0].reshape(1, D_MODEL))
    r, kmod, v, lw, kn, b, bonus, g = _rwkv_prep(
        p_rwkv.reshape(bsz, lp, RWKV_COLS_PAD), vecs,
        _pad_rows(rwkv_w2[0], LANE).astype(BF16), _pad_rows(rwkv_a2[0], LANE).astype(BF16),
        _pad_rows(rwkv_g2[0], GATE_LORA_PAD).astype(BF16), ones_bd, tm=tm_tok)
    y = _rwkv_chunk(r, kmod, v, lw, kn, b)

    a3 = a_diff.reshape(bsz, lp, DIFF_GATE_COLS)
    o_diff = _diff_attn(a3, (diff_lq1, diff_lk1, diff_lq2, diff_lk2), diff_subln_w[0][None, :], tq=256,
                        kv_tiles=4)

    h_mid = _merge(y, bonus, g, o_diff, a3, h_res, rwkv_ln_w[0][None, :], rwkv_ln_b[0][None, :],
                   ones_bd[:PAIR, :PAIR], rwkv_w_o[0].astype(BF16), diff_w_o[0].astype(BF16),
                   w_out[0].astype(BF16), tm=tm_tok)
    out = _mlp(h_mid.reshape(bsz * lp, D_MODEL), norm_mlp_w[0][None, :], mlp_w1[0].astype(BF16),
               mlp_w2[0].astype(BF16), final_norm_w[None, :], tm=tm_tok)
    return out.reshape(bsz, lp, D_MODEL)[:, N_META_TOKENS:seqlen]
```

```python
import functools
import math

import jax
import jax.numpy as jnp
from jax import lax
from jax.experimental import pallas as pl
from jax.experimental.pallas import tpu as pltpu

F32 = jnp.float32
BF16 = jnp.bfloat16

D_MODEL = 1024
N_META_TOKENS = 16
HEAD_DIM = 64
PAIR = 2 * HEAD_DIM
N_PAIRS = D_MODEL // PAIR
DECAY_LORA = 64
AAA_LORA = 64
GATE_LORA = 160
RWKV_LN_EPS = 64e-5
ROPE_THETA = 500000.0
ROPE_DIM = HEAD_DIM // 4
ROPE_HALF = ROPE_DIM // 2
D_FF = 4 * D_MODEL
NORM_EPS = 1e-5
SUBLN_EPS = 1e-5
LAMBDA_INIT = 0.8 - 0.6 * math.exp(-0.3 * 0)

LANE = 128
SUBLANE = 8
CHUNK = 64
UNIT_UNROLL = 8
PROJ_COL_CHUNK = 256
ATTN_HEADS_PER_STEP = 4
Q_SCALE = HEAD_DIM ** -0.5 * math.log2(math.e)
SEQ_ALIGN = 256
VMEM_LIMIT = 56 * 1024 * 1024

LORA_W_OFF = 3 * D_MODEL
LORA_A_OFF = LORA_W_OFF + LANE
LORA_G_OFF = LORA_A_OFF + LANE
GATE_LORA_PAD = 2 * LANE
RWKV_COLS_PAD = LORA_G_OFF + GATE_LORA_PAD


def _params(sem):
    return pltpu.CompilerParams(dimension_semantics=sem, vmem_limit_bytes=VMEM_LIMIT)


def _rms(x, w, eps):
    return x * lax.rsqrt(jnp.mean(x * x, axis=-1, keepdims=True) + eps) * w


def _sigmoid(x):
    return 1.0 / (1.0 + jnp.exp(-x))


def _dot(a, b):
    return jnp.dot(a.astype(BF16), b.astype(BF16), preferred_element_type=F32)


def _dot_nt(a, b):
    return lax.dot_general(a.astype(BF16), b.astype(BF16), (((1,), (1,)), ((), ())),
                           preferred_element_type=F32)


def _dot_tn(a, b):
    return lax.dot_general(a.astype(BF16), b.astype(BF16), (((0,), (0,)), ((), ())),
                           preferred_element_type=F32)


def _in_proj_kernel(*refs, mode, tiles_per_seq):
    if mode == "shift":
        x_ref, xprev_ref, nw_ref, w_ref, mu_ref, o_ref = refs
    elif mode == "rope":
        x_ref, nw_ref, w_ref, cos_ref, s1_ref, s2_ref, o_ref = refs
    else:
        x_ref, nw_ref, w_ref, o_ref = refs
    tm, n_cols = o_ref.shape
    cc = PROJ_COL_CHUNK
    xn = _rms(x_ref[...], nw_ref[...], NORM_EPS).astype(BF16)
    if mode == "shift":
        xpn = _rms(xprev_ref[...], nw_ref[...], NORM_EPS).astype(BF16)
        first = (pl.program_id(0) % tiles_per_seq) == 0
        row = lax.broadcasted_iota(jnp.int32, (tm, cc), 0)
    elif mode == "rope":
        reps = cc // LANE
        cos = jnp.tile(cos_ref[...], (1, reps))
        s1 = jnp.tile(s1_ref[...], (1, reps))
        s2 = jnp.tile(s2_ref[...], (1, reps))
    for c in range(n_cols // cc):
        cols = slice(c * cc, (c + 1) * cc)
        w = w_ref[:, cols]
        p = jnp.dot(xn, w, preferred_element_type=F32)
        if mode == "shift":
            pp = jnp.dot(xpn, w, preferred_element_type=F32)
            prev_row = jnp.where(first, 0.0, pp[SUBLANE - 1:SUBLANE, :])
            shifted = jnp.where(row == 0, prev_row, pltpu.roll(p, 1, axis=0))
            p = p + (shifted - p) * mu_ref[:, cols]
        elif mode == "rope":
            nxt = pltpu.roll(p, cc - ROPE_HALF, axis=1)
            prv = pltpu.roll(p, ROPE_HALF, axis=1)
            scale = Q_SCALE if c * cc < D_MODEL else 1.0
            p = (p * cos + nxt * s1 + prv * s2) * scale
        o_ref[:, cols] = p.astype(o_ref.dtype)


def _in_proj(xf, nw, w, *, mode, mu=None, rope=None, tm, tiles_per_seq):
    t_tokens = xf.shape[0]
    n_cols = w.shape[1]
    x_spec = pl.BlockSpec((tm, D_MODEL), lambda i: (i, 0))
    nw_spec = pl.BlockSpec((1, D_MODEL), lambda i: (0, 0))
    w_spec = pl.BlockSpec((D_MODEL, n_cols), lambda i: (0, 0))
    if mode == "shift":
        rows8 = tm // SUBLANE
        prev_spec = pl.BlockSpec((SUBLANE, D_MODEL), lambda i: (jnp.maximum(i * rows8 - 1, 0), 0))
        in_specs = [x_spec, prev_spec, nw_spec, w_spec, pl.BlockSpec((1, n_cols), lambda i: (0, 0))]
        args = (xf, xf, nw, w, mu)
    elif mode == "rope":
        tab_spec = pl.BlockSpec((tm, LANE), lambda i: (i % tiles_per_seq, 0))
        in_specs = [x_spec, nw_spec, w_spec, tab_spec, tab_spec, tab_spec]
        args = (xf, nw, w) + tuple(rope)
    else:
        in_specs = [x_spec, nw_spec, w_spec]
        args = (xf, nw, w)
    return pl.pallas_call(
        functools.partial(_in_proj_kernel, mode=mode, tiles_per_seq=tiles_per_seq),
        grid=(t_tokens // tm,),
        in_specs=in_specs,
        out_specs=pl.BlockSpec((tm, n_cols), lambda i: (i, 0)),
        out_shape=jax.ShapeDtypeStruct((t_tokens, n_cols), BF16),
        compiler_params=_params(("arbitrary",)),
        name="in_proj_" + mode,
    )(*args)


def _rwkv_prep_kernel(p_ref, w0_ref, a0_ref, kk_ref, ka_ref, rk_ref, w2_ref, a2_ref, g2_ref,
                      ones_ref, r_o, k_o, v_o, lw_o, kn_o, b_o, bonus_o, g_o):
    r = p_ref[0, :, 0:D_MODEL].astype(F32)
    k = p_ref[0, :, D_MODEL:2 * D_MODEL].astype(F32)
    v = p_ref[0, :, 2 * D_MODEL:3 * D_MODEL].astype(F32)
    wd = p_ref[0, :, LORA_W_OFF:LORA_W_OFF + LANE].astype(F32)
    ad = p_ref[0, :, LORA_A_OFF:LORA_A_OFF + LANE]
    gd = p_ref[0, :, LORA_G_OFF:LORA_G_OFF + GATE_LORA_PAD].astype(F32)
    ones_bd = ones_ref[...]

    z = w0_ref[...] + _dot(jnp.tanh(wd), w2_ref[...])
    softplus_neg = jnp.maximum(-z, 0.0) + jnp.log(1.0 + jnp.exp(-jnp.abs(z)))
    log_decay = -jnp.exp(-softplus_neg - 0.5)
    a_rate = _sigmoid(a0_ref[...] + _dot(ad, a2_ref[...]))
    gate = _dot(_sigmoid(gd), g2_ref[...])

    kraw = k * kk_ref[...]
    sumsq = _dot(kraw * kraw, ones_bd)
    knorm = kraw / jnp.maximum(jnp.sqrt(sumsq), 1e-12)
    kmod = k * (1.0 + (a_rate - 1.0) * ka_ref[...])
    bonus = _dot(r * kmod * rk_ref[...], ones_bd) * v

    bonus_o[0] = bonus.astype(BF16)
    g_o[0] = gate.astype(BF16)
    r16, k16, v16 = r.astype(BF16), kmod.astype(BF16), v.astype(BF16)
    kn16, b16 = knorm.astype(BF16), (knorm * a_rate).astype(BF16)
    for pair in range(N_PAIRS):
        sl = slice(pair * PAIR, (pair + 1) * PAIR)
        r_o[0, pair] = r16[:, sl]
        k_o[0, pair] = k16[:, sl]
        v_o[0, pair] = v16[:, sl]
        lw_o[0, pair] = log_decay[:, sl]
        kn_o[0, pair] = kn16[:, sl]
        b_o[0, pair] = b16[:, sl]


def _rwkv_prep(p3, vecs, w2p, a2p, g2p, ones_bd, *, tm):
    bsz, lp, _ = p3.shape
    grid = (bsz, lp // tm)
    vec_spec = pl.BlockSpec((1, D_MODEL), lambda b, i: (0, 0))
    pair_spec = pl.BlockSpec((1, N_PAIRS, tm, PAIR), lambda b, i: (b, 0, i, 0))
    tok_spec = pl.BlockSpec((1, tm, D_MODEL), lambda b, i: (b, i, 0))
    pair_shape = jax.ShapeDtypeStruct((bsz, N_PAIRS, lp, PAIR), BF16)
    decay_shape = jax.ShapeDtypeStruct((bsz, N_PAIRS, lp, PAIR), F32)
    tok_shape = jax.ShapeDtypeStruct((bsz, lp, D_MODEL), BF16)
    return pl.pallas_call(
        _rwkv_prep_kernel,
        grid=grid,
        in_specs=[pl.BlockSpec((1, tm, RWKV_COLS_PAD), lambda b, i: (b, i, 0))]
        + [vec_spec] * 5
        + [pl.BlockSpec((LANE, D_MODEL), lambda b, i: (0, 0)),
           pl.BlockSpec((LANE, D_MODEL), lambda b, i: (0, 0)),
           pl.BlockSpec((GATE_LORA_PAD, D_MODEL), lambda b, i: (0, 0)),
           pl.BlockSpec((D_MODEL, D_MODEL), lambda b, i: (0, 0))],
        out_specs=[pair_spec] * 6 + [tok_spec] * 2,
        out_shape=[pair_shape] * 3 + [decay_shape] + [pair_shape] * 2 + [tok_shape] * 2,
        compiler_params=_params(("arbitrary", "arbitrary")),
        name="rwkv_prep",
    )(p3, *vecs, w2p, a2p, g2p, ones_bd)


def _block_diag(x, lane_head0):
    return jnp.concatenate([jnp.where(lane_head0, x, 0.0), jnp.where(lane_head0, 0.0, x)], axis=0)


def _chunk_unit(r, k, v, lw, kn, b, state):
    n = 2 * CHUNK
    row = lax.broadcasted_iota(jnp.int32, (CHUNK, PAIR), 0)
    lane_head0 = lax.broadcasted_iota(jnp.int32, (CHUNK, PAIR), 1) < HEAD_DIM
    c = lw
    for s in (1, 2, 4, 8, 16, 32):
        c = c + jnp.where(row >= s, pltpu.roll(c, s, axis=0), 0.0)
    c_end = c[CHUNK - 1:CHUNK, :]
    e_neg = jnp.exp(-c)
    e_end = jnp.exp(c_end - c)
    a_t = _block_diag(-kn * jnp.exp(c - lw), lane_head0)
    r_t = _block_diag(r * jnp.exp(c), lane_head0)
    b_t = _block_diag(b * e_neg, lane_head0)
    k_t = _block_diag(k * e_neg, lane_head0)
    b_e = _block_diag(b * e_end, lane_head0)
    k_e = _block_diag(k * e_end, lane_head0)
    v_bd = _block_diag(v, lane_head0)

    yield
    g = _dot_nt(jnp.concatenate([a_t, r_t], axis=0), jnp.concatenate([b_t, k_t], axis=0))
    ri = lax.broadcasted_iota(jnp.int32, (n, n), 0)
    ci = lax.broadcasted_iota(jnp.int32, (n, n), 1)
    m_ab = jnp.where(ci < ri, g[:n, :n], 0.0)
    m_ak = jnp.where(ci < ri, g[:n, n:], 0.0)
    a_rb = jnp.where(ci <= ri, g[n:, :n], 0.0)
    a_rk = jnp.where(ci <= ri, g[n:, n:], 0.0)

    t_inv = jnp.where(ci == ri, 1.0, m_ab)
    yield
    power = _dot(m_ab, m_ab)
    w1 = _dot(m_ak, v_bd)
    for _ in range(4):
        yield
        prod = _dot(power, jnp.concatenate([power, t_inv], axis=1))
        power = prod[:, :n]
        t_inv = t_inv + prod[:, n:]
    yield
    t_inv = t_inv + _dot(power, t_inv)

    yield
    tw = _dot(t_inv, jnp.concatenate([w1, a_t], axis=1))
    u0 = tw[:, :n]
    a_hat = tw[:, n:]
    rhs = jnp.concatenate([jnp.concatenate([a_hat, u0], axis=1),
                           jnp.concatenate([jnp.zeros_like(v_bd), v_bd], axis=1)], axis=0)
    yield
    pq = _dot_tn(jnp.concatenate([b_e, k_e], axis=0), rhs)
    ry = _dot(jnp.concatenate([a_rb, a_rk], axis=1), rhs)
    decay_end = jnp.exp(jnp.broadcast_to(c_end, (n, n)))
    p_mat = pq[:, :n] + jnp.where(ci == ri, decay_end, 0.0)
    yield
    y_bd = _dot(r_t + ry[:, :n], state) + ry[:, n:]
    y = y_bd[:CHUNK] + y_bd[CHUNK:]
    new_state = _dot(p_mat, state) + pq[:, n:]
    return y, new_state


def _interleave(generators):
    results = [None] * len(generators)
    pending = list(range(len(generators)))
    while pending:
        still = []
        for idx in pending:
            try:
                next(generators[idx])
                still.append(idx)
            except StopIteration as stop:
                results[idx] = stop.value
        pending = still
    return results


def _rwkv_chunk_kernel(r_ref, k_ref, v_ref, lw_ref, kn_ref, b_ref, y_ref, state_scr, *, n_units):
    @pl.when(pl.program_id(0) == 0)
    def _():
        state_scr[...] = jnp.zeros_like(state_scr)

    def body(g, carry):
        units = [g * UNIT_UNROLL + j for j in range(UNIT_UNROLL)]
        idx = [(u // N_PAIRS, u % N_PAIRS) for u in units]
        args = [(r_ref[bi, pi].astype(F32), k_ref[bi, pi].astype(F32), v_ref[bi, pi].astype(F32),
                 lw_ref[bi, pi], kn_ref[bi, pi].astype(F32), b_ref[bi, pi].astype(F32), state_scr[u])
                for u, (bi, pi) in zip(units, idx)]
        outs = _interleave([_chunk_unit(*a) for a in args])
        for u, (bi, pi), (y, new_state) in zip(units, idx, outs):
            y_ref[bi, pi] = y.astype(BF16)
            state_scr[u] = new_state
        return carry

    lax.fori_loop(0, n_units // UNIT_UNROLL, body, 0)


def _rwkv_chunk(r, k, v, lw, kn, b):
    bsz, _, lp, _ = r.shape
    n_units = bsz * N_PAIRS
    spec = pl.BlockSpec((bsz, N_PAIRS, CHUNK, PAIR), lambda c: (0, 0, c, 0))
    return pl.pallas_call(
        functools.partial(_rwkv_chunk_kernel, n_units=n_units),
        grid=(lp // CHUNK,),
        in_specs=[spec] * 6,
        out_specs=spec,
        out_shape=jax.ShapeDtypeStruct(r.shape, BF16),
        scratch_shapes=[pltpu.VMEM((n_units, PAIR, PAIR), F32)],
        compiler_params=_params(("arbitrary",)),
        name="rwkv_chunk",
    )(r, k, v, lw, kn, b)


def _diff_attn_kernel(q_ref, k_ref, v_ref, lq1_ref, lk1_ref, lq2_ref, lk2_ref, subw_ref, o_ref,
                      m_scr, acc_scr, *, tq, kv_tiles):
    i = pl.program_id(2)
    head0 = lax.broadcasted_iota(jnp.int32, (tq, PAIR), 1) < HEAD_DIM
    qq = []
    for t in range(ATTN_HEADS_PER_STEP):
        q = q_ref[0, :, t * PAIR:(t + 1) * PAIR]
        zero = jnp.zeros_like(q)
        qq.append(jnp.concatenate([jnp.where(head0, q, zero), jnp.where(head0, zero, q)], axis=0))
    m_scr[...] = jnp.full_like(m_scr, -1e30)
    acc_scr[...] = jnp.zeros_like(acc_scr)

    def step(j, width, masked):
        start = pl.multiple_of(j * tq, tq)
        ones = jnp.ones((width, PAIR), BF16)
        def scores(t):
            return _dot_nt(qq[t], k_ref[0, pl.ds(start, width), t * PAIR:(t + 1) * PAIR])

        s_next = scores(0)
        for t in range(ATTN_HEADS_PER_STEP):
            s = s_next
            if t + 1 < ATTN_HEADS_PER_STEP:
                s_next = scores(t + 1)
            if masked:
                rr = lax.broadcasted_iota(jnp.int32, (2 * tq, width), 0) % tq
                cc = lax.broadcasted_iota(jnp.int32, (2 * tq, width), 1)
                s = jnp.where(cc <= rr, s, -1e30)
            m_prev = m_scr[t]
            m_next = jnp.maximum(m_prev, jnp.max(s, axis=1, keepdims=True))
            p = jnp.exp2(s - jnp.tile(m_next, (1, width // LANE))).astype(BF16)
            alpha = jnp.exp2(m_prev - m_next)
            v_ext = jnp.concatenate([v_ref[0, pl.ds(start, width), t * PAIR:(t + 1) * PAIR], ones], axis=1)
            acc_scr[t] = jnp.tile(alpha, (1, 2)) * acc_scr[t] + jnp.dot(p, v_ext, preferred_element_type=F32)
            m_scr[t] = m_next

    def big_body(j, carry):
        step(j * kv_tiles, kv_tiles * tq, False)
        return carry

    def small_body(j, carry):
        step(j, tq, False)
        return carry

    n_big = i // kv_tiles
    lax.fori_loop(0, n_big, big_body, 0)
    lax.fori_loop(n_big * kv_tiles, i, small_body, 0)
    step(i, tq, True)

    lam = (jnp.exp(jnp.sum(lq1_ref[...] * lk1_ref[...], axis=1, keepdims=True))
           - jnp.exp(jnp.sum(lq2_ref[...] * lk2_ref[...], axis=1, keepdims=True)) + LAMBDA_INIT)
    for t in range(ATTN_HEADS_PER_STEP):
        acc = acc_scr[t]
        o = acc[:, :PAIR] / acc[:, PAIR:]
        od = o[:tq] - lam * o[tq:]
        o_ref[0, :, t * PAIR:(t + 1) * PAIR] = (
            _rms(od, subw_ref[...], SUBLN_EPS) * (1.0 - LAMBDA_INIT)).astype(o_ref.dtype)


def _diff_attn(qk3, vg3, lams, subw, *, tq, kv_tiles):
    bsz, lp, _ = qk3.shape
    width = ATTN_HEADS_PER_STEP * PAIR
    n_groups = D_MODEL // width
    grid = (bsz, n_groups, lp // tq)
    q_spec = pl.BlockSpec((1, tq, width), lambda b, g, i: (b, i, g))
    k_spec = pl.BlockSpec((1, lp, width), lambda b, g, i: (b, 0, n_groups + g))
    v_spec = pl.BlockSpec((1, lp, width), lambda b, g, i: (b, 0, g))
    lam_spec = pl.BlockSpec((1, HEAD_DIM), lambda b, g, i: (0, 0))
    return pl.pallas_call(
        functools.partial(_diff_attn_kernel, tq=tq, kv_tiles=kv_tiles),
        grid=grid,
        in_specs=[q_spec, k_spec, v_spec, lam_spec, lam_spec, lam_spec, lam_spec,
                  pl.BlockSpec((1, PAIR), lambda b, g, i: (0, 0))],
        out_specs=q_spec,
        out_shape=jax.ShapeDtypeStruct((bsz, lp, D_MODEL), BF16),
        scratch_shapes=[pltpu.VMEM((ATTN_HEADS_PER_STEP, 2 * tq, PAIR), F32),
                        pltpu.VMEM((ATTN_HEADS_PER_STEP, 2 * tq, 2 * PAIR), F32)],
        compiler_params=_params(("arbitrary", "arbitrary", "arbitrary")),
        name="diff_attn",
    )(qk3, qk3, vg3, *lams, subw)


def _merge_kernel(y_ref, bonus_ref, g_ref, od_ref, gate_r_ref, gate_d_ref, res_ref, lnw_ref, lnb_ref,
                  ones_ref, wo_r_ref, wo_d_ref, wout_ref, o_ref):
    ones_pair = ones_ref[...]
    parts = []
    for pair in range(N_PAIRS):
        y = y_ref[0, pair].astype(F32)
        mean = _dot(y, ones_pair) * (1.0 / HEAD_DIM)
        yc = y - mean
        var = _dot(yc * yc, ones_pair) * (1.0 / HEAD_DIM)
        parts.append(yc * lax.rsqrt(var + RWKV_LN_EPS))
    yn = jnp.concatenate(parts, axis=1)
    o_rwkv = (yn * lnw_ref[...] + lnb_ref[...] + bonus_ref[0].astype(F32)) * g_ref[0].astype(F32)
    br_rwkv = _dot(o_rwkv, wo_r_ref[...])
    br_diff = jnp.dot(od_ref[0], wo_d_ref[...], preferred_element_type=F32)
    merged = (_sigmoid(gate_r_ref[0].astype(F32)) * br_rwkv
              + _sigmoid(gate_d_ref[0].astype(F32)) * br_diff)
    o_ref[0] = res_ref[0] + _dot(merged, wout_ref[...])


def _merge(y, bonus, g, od, a3, res, lnw, lnb, ones_pair, wo_r, wo_d, wout, *, tm):
    bsz, lp, _ = res.shape
    grid = (bsz, lp // tm)
    tok = lambda c: pl.BlockSpec((1, tm, c), lambda b, i: (b, i, 0))
    vec = pl.BlockSpec((1, D_MODEL), lambda b, i: (0, 0))
    mat = pl.BlockSpec((D_MODEL, D_MODEL), lambda b, i: (0, 0))
    gate_r_spec = pl.BlockSpec((1, tm, D_MODEL), lambda b, i: (b, i, 1))
    gate_d_spec = pl.BlockSpec((1, tm, D_MODEL), lambda b, i: (b, i, 2))
    return pl.pallas_call(
        _merge_kernel,
        grid=grid,
        in_specs=[pl.BlockSpec((1, N_PAIRS, tm, PAIR), lambda b, i: (b, 0, i, 0)),
                  tok(D_MODEL), tok(D_MODEL), tok(D_MODEL), gate_r_spec, gate_d_spec, tok(D_MODEL), vec, vec,
                  pl.BlockSpec((PAIR, PAIR), lambda b, i: (0, 0)), mat, mat, mat],
        out_specs=tok(D_MODEL),
        out_shape=jax.ShapeDtypeStruct(res.shape, F32),
        compiler_params=_params(("arbitrary", "arbitrary")),
        name="merge",
    )(y, bonus, g, od, a3, a3, res, lnw, lnb, ones_pair, wo_r, wo_d, wout)


def _mlp_kernel(x_ref, nw_ref, w1_ref, w2_ref, fw_ref, o_ref, *, n_ff_chunks):
    x = x_ref[...]
    h = _rms(x, nw_ref[...], NORM_EPS).astype(BF16)
    acc = x
    ff = D_FF // n_ff_chunks
    for c in range(n_ff_chunks):
        z = jnp.dot(h, w1_ref[:, c * ff:(c + 1) * ff], preferred_element_type=F32)
        act = jnp.square(jnp.maximum(z, 0.0))
        acc = acc + _dot(act, w2_ref[c * ff:(c + 1) * ff, :])
    o_ref[...] = _rms(acc, fw_ref[...], NORM_EPS)


def _mlp(xf, nw, w1, w2, fw, *, tm):
    t_tokens = xf.shape[0]
    tok = pl.BlockSpec((tm, D_MODEL), lambda i: (i, 0))
    vec = pl.BlockSpec((1, D_MODEL), lambda i: (0, 0))
    return pl.pallas_call(
        functools.partial(_mlp_kernel, n_ff_chunks=4),
        grid=(t_tokens // tm,),
        in_specs=[tok, vec, pl.BlockSpec((D_MODEL, D_FF), lambda i: (0, 0)),
                  pl.BlockSpec((D_FF, D_MODEL), lambda i: (0, 0)), vec],
        out_specs=tok,
        out_shape=jax.ShapeDtypeStruct(xf.shape, F32),
        compiler_params=_params(("arbitrary",)),
        name="mlp",
    )(xf, nw, w1, w2, fw)


def _pad_cols(a, width):
    return jnp.pad(a, ((0, 0), (0, width - a.shape[1])))


def _pad_rows(a, height):
    return jnp.pad(a, ((0, height - a.shape[0]), (0, 0)))


def _rope_tables(lp):
    inv = ROPE_THETA ** (-jnp.arange(0, ROPE_DIM, 2, dtype=F32) / ROPE_DIM)
    ang = jnp.arange(lp, dtype=F32)[:, None] * inv[None, :]
    cos, sin = jnp.cos(ang), jnp.sin(ang)
    rest = HEAD_DIM - ROPE_DIM
    ones = jnp.ones((lp, rest), F32)
    zeros = jnp.zeros((lp, rest), F32)
    zh = jnp.zeros_like(sin)
    c = jnp.concatenate([cos, cos, ones], axis=1)
    s1 = jnp.concatenate([-sin, zh, zeros], axis=1)
    s2 = jnp.concatenate([zh, sin, zeros], axis=1)
    return tuple(jnp.tile(t, (1, LANE // HEAD_DIM)) for t in (c, s1, s2))


def _pick(n, candidates):
    for c in candidates:
        if n % c == 0:
            return c
    raise ValueError(n)


def kernel(x, meta_tokens, norm_mix_w, w_in, rwkv_mu, rwkv_w0, rwkv_w2, rwkv_a0, rwkv_a2, rwkv_g2, rwkv_k_k, rwkv_k_a, rwkv_r_k, rwkv_ln_w, rwkv_ln_b, rwkv_w_o, diff_lq1, diff_lk1, diff_lq2, diff_lk2, diff_subln_w, diff_w_o, w_out, norm_mlp_w, mlp_w1, mlp_w2, final_norm_w):
    bsz, seq, _ = x.shape
    seqlen = seq + N_META_TOKENS
    lp = -(-seqlen // SEQ_ALIGN) * SEQ_ALIGN
    tm_proj = _pick(lp, (768, 512, 256))
    tm_tok = 256

    meta = jnp.broadcast_to(meta_tokens[None].astype(x.dtype), (bsz, N_META_TOKENS, D_MODEL))
    h_res = jnp.concatenate([meta, x, jnp.zeros((bsz, lp - seqlen, D_MODEL), x.dtype)], axis=1)
    hf = h_res.reshape(bsz * lp, D_MODEL)

    w = w_in[0]
    c3 = 3 * D_MODEL
    w_rwkv = jnp.concatenate(
        [w[:, :c3], _pad_cols(w[:, c3:c3 + DECAY_LORA], LANE),
         _pad_cols(w[:, c3 + DECAY_LORA:c3 + DECAY_LORA + AAA_LORA], LANE),
         _pad_cols(w[:, c3 + DECAY_LORA + AAA_LORA:c3 + DECAY_LORA + AAA_LORA + GATE_LORA], GATE_LORA_PAD)],
        axis=1).astype(BF16)
    rwkv_cols = c3 + DECAY_LORA + AAA_LORA + GATE_LORA
    w_diff = w[:, rwkv_cols:].astype(BF16)
    mu = rwkv_mu[0][None, :]
    mu_p = jnp.concatenate(
        [mu[:, :c3], _pad_cols(mu[:, c3:c3 + DECAY_LORA], LANE),
         _pad_cols(mu[:, c3 + DECAY_LORA:c3 + DECAY_LORA + AAA_LORA], LANE),
         _pad_cols(mu[:, c3 + DECAY_LORA + AAA_LORA:], GATE_LORA_PAD)], axis=1)
    nw = norm_mix_w[0][None, :]

    tiles_per_seq = lp // tm_proj
    proj = functools.partial(_in_proj, hf, nw, tm=tm_proj, tiles_per_seq=tiles_per_seq)
    p_rwkv = proj(w_rwkv, mode="shift", mu=mu_p)
    qk3 = proj(w_diff[:, :2 * D_MODEL], mode="rope", rope=_rope_tables(lp)).reshape(bsz, lp, 2 * D_MODEL)
    a3 = proj(w_diff[:, 2 * D_MODEL:], mode="plain").reshape(bsz, lp, 3 * D_MODEL)

    head_id = jnp.arange(D_MODEL) // HEAD_DIM
    ones_bd = (head_id[:, None] == head_id[None, :]).astype(BF16)
    vecs = (rwkv_w0[0][None, :], rwkv_a0[0][None, :], rwkv_k_k[0][None, :], rwkv_k_a[0][None, :],
            rwkv_r_k[0].reshape(1, D_MODEL))
    r, kmod, v, lw, kn, b, bonus, g = _rwkv_prep(
        p_rwkv.reshape(bsz, lp, RWKV_COLS_PAD), vecs,
        _pad_rows(rwkv_w2[0], LANE).astype(BF16), _pad_rows(rwkv_a2[0], LANE).astype(BF16),
        _pad_rows(rwkv_g2[0], GATE_LORA_PAD).astype(BF16), ones_bd, tm=tm_tok)
    y = _rwkv_chunk(r, kmod, v, lw, kn, b)

    o_diff = _diff_attn(qk3, a3, (diff_lq1, diff_lk1, diff_lq2, diff_lk2), diff_subln_w[0][None, :], tq=256,
                        kv_tiles=4)

    h_mid = _merge(y, bonus, g, o_diff, a3, h_res, rwkv_ln_w[0][None, :], rwkv_ln_b[0][None, :],
                   ones_bd[:PAIR, :PAIR], rwkv_w_o[0].astype(BF16), diff_w_o[0].astype(BF16),
                   w_out[0].astype(BF16), tm=tm_tok)
    out = _mlp(h_mid.reshape(bsz * lp, D_MODEL), norm_mlp_w[0][None, :], mlp_w1[0].astype(BF16),
               mlp_w2[0].astype(BF16), final_norm_w[None, :], tm=tm_tok)
    return out.reshape(bsz, lp, D_MODEL)[:, N_META_TOKENS:seqlen]
```

```python
import functools
import math

import jax
import jax.numpy as jnp
from jax import lax
from jax.experimental import pallas as pl
from jax.experimental.pallas import tpu as pltpu

F32 = jnp.float32
BF16 = jnp.bfloat16

D_MODEL = 1024
N_META_TOKENS = 16
HEAD_DIM = 64
PAIR = 2 * HEAD_DIM
N_PAIRS = D_MODEL // PAIR
DECAY_LORA = 64
AAA_LORA = 64
GATE_LORA = 160
RWKV_LN_EPS = 64e-5
ROPE_THETA = 500000.0
ROPE_DIM = HEAD_DIM // 4
ROPE_HALF = ROPE_DIM // 2
D_FF = 4 * D_MODEL
NORM_EPS = 1e-5
SUBLN_EPS = 1e-5
LAMBDA_INIT = 0.8 - 0.6 * math.exp(-0.3 * 0)

LANE = 128
SUBLANE = 8
CHUNK = 64
UNIT_UNROLL = 8
PROJ_COL_CHUNK = 256
ATTN_HEADS_PER_STEP = 4
Q_SCALE = HEAD_DIM ** -0.5 * math.log2(math.e)
SEQ_ALIGN = 256
VMEM_LIMIT = 56 * 1024 * 1024

LORA_W_OFF = 3 * D_MODEL
LORA_A_OFF = LORA_W_OFF + LANE
LORA_G_OFF = LORA_A_OFF + LANE
GATE_LORA_PAD = 2 * LANE
RWKV_COLS_PAD = LORA_G_OFF + GATE_LORA_PAD


def _params(sem):
    return pltpu.CompilerParams(dimension_semantics=sem, vmem_limit_bytes=VMEM_LIMIT)


def _rms(x, w, eps):
    return x * lax.rsqrt(jnp.mean(x * x, axis=-1, keepdims=True) + eps) * w


def _sigmoid(x):
    return 1.0 / (1.0 + jnp.exp(-x))


def _dot(a, b):
    return jnp.dot(a.astype(BF16), b.astype(BF16), preferred_element_type=F32)


def _dot_nt(a, b):
    return lax.dot_general(a.astype(BF16), b.astype(BF16), (((1,), (1,)), ((), ())),
                           preferred_element_type=F32)


def _dot_tn(a, b):
    return lax.dot_general(a.astype(BF16), b.astype(BF16), (((0,), (0,)), ((), ())),
                           preferred_element_type=F32)


def _in_proj_kernel(*refs, mode, tiles_per_seq):
    if mode == "shift":
        x_ref, xprev_ref, nw_ref, w_ref, mu_ref, o_ref = refs
    elif mode == "rope":
        x_ref, nw_ref, w_ref, cos_ref, s1_ref, s2_ref, o_ref = refs
    else:
        x_ref, nw_ref, w_ref, o_ref = refs
    tm, n_cols = o_ref.shape
    cc = PROJ_COL_CHUNK
    xn = _rms(x_ref[...], nw_ref[...], NORM_EPS).astype(BF16)
    if mode == "shift":
        xpn = _rms(xprev_ref[...], nw_ref[...], NORM_EPS).astype(BF16)
        first = (pl.program_id(0) % tiles_per_seq) == 0
        row = lax.broadcasted_iota(jnp.int32, (tm, cc), 0)
    elif mode == "rope":
        reps = cc // LANE
        cos = jnp.tile(cos_ref[...], (1, reps))
        s1 = jnp.tile(s1_ref[...], (1, reps))
        s2 = jnp.tile(s2_ref[...], (1, reps))
    for c in range(n_cols // cc):
        cols = slice(c * cc, (c + 1) * cc)
        w = w_ref[:, cols]
        p = jnp.dot(xn, w, preferred_element_type=F32)
        if mode == "shift":
            pp = jnp.dot(xpn, w, preferred_element_type=F32)
            prev_row = jnp.where(first, 0.0, pp[SUBLANE - 1:SUBLANE, :])
            shifted = jnp.where(row == 0, prev_row, pltpu.roll(p, 1, axis=0))
            p = p + (shifted - p) * mu_ref[:, cols]
        elif mode == "rope":
            nxt = pltpu.roll(p, cc - ROPE_HALF, axis=1)
            prv = pltpu.roll(p, ROPE_HALF, axis=1)
            scale = Q_SCALE if c * cc < D_MODEL else 1.0
            p = (p * cos + nxt * s1 + prv * s2) * scale
        o_ref[:, cols] = p.astype(o_ref.dtype)


def _in_proj(xf, nw, w, *, mode, mu=None, rope=None, tm, tiles_per_seq):
    t_tokens = xf.shape[0]
    n_cols = w.shape[1]
    x_spec = pl.BlockSpec((tm, D_MODEL), lambda i: (i, 0))
    nw_spec = pl.BlockSpec((1, D_MODEL), lambda i: (0, 0))
    w_spec = pl.BlockSpec((D_MODEL, n_cols), lambda i: (0, 0))
    if mode == "shift":
        rows8 = tm // SUBLANE
        prev_spec = pl.BlockSpec((SUBLANE, D_MODEL), lambda i: (jnp.maximum(i * rows8 - 1, 0), 0))
        in_specs = [x_spec, prev_spec, nw_spec, w_spec, pl.BlockSpec((1, n_cols), lambda i: (0, 0))]
        args = (xf, xf, nw, w, mu)
    elif mode == "rope":
        tab_spec = pl.BlockSpec((tm, LANE), lambda i: (i % tiles_per_seq, 0))
        in_specs = [x_spec, nw_spec, w_spec, tab_spec, tab_spec, tab_spec]
        args = (xf, nw, w) + tuple(rope)
    else:
        in_specs = [x_spec, nw_spec, w_spec]
        args = (xf, nw, w)
    return pl.pallas_call(
        functools.partial(_in_proj_kernel, mode=mode, tiles_per_seq=tiles_per_seq),
        grid=(t_tokens // tm,),
        in_specs=in_specs,
        out_specs=pl.BlockSpec((tm, n_cols), lambda i: (i, 0)),
        out_shape=jax.ShapeDtypeStruct((t_tokens, n_cols), BF16),
        compiler_params=_params(("arbitrary",)),
        name="in_proj_" + mode,
    )(*args)


def _rwkv_prep_kernel(p_ref, w0_ref, a0_ref, kk_ref, ka_ref, rk_ref, w2_ref, a2_ref, g2_ref,
                      ones_ref, r_o, k_o, v_o, lw_o, kn_o, b_o, bonus_o, g_o):
    r = p_ref[0, :, 0:D_MODEL].astype(F32)
    k = p_ref[0, :, D_MODEL:2 * D_MODEL].astype(F32)
    v = p_ref[0, :, 2 * D_MODEL:3 * D_MODEL].astype(F32)
    wd = p_ref[0, :, LORA_W_OFF:LORA_W_OFF + LANE].astype(F32)
    ad = p_ref[0, :, LORA_A_OFF:LORA_A_OFF + LANE]
    gd = p_ref[0, :, LORA_G_OFF:LORA_G_OFF + GATE_LORA_PAD].astype(F32)
    ones_bd = ones_ref[...]

    z = w0_ref[...] + _dot(jnp.tanh(wd), w2_ref[...])
    log_decay = -math.exp(-0.5) * _sigmoid(z)
    a_rate = _sigmoid(a0_ref[...] + _dot(ad, a2_ref[...]))
    gate = _dot(_sigmoid(gd), g2_ref[...])

    kraw = k * kk_ref[...]
    sumsq = _dot(kraw * kraw, ones_bd)
    knorm = kraw * lax.rsqrt(jnp.maximum(sumsq, 1e-24))
    kmod = k * (1.0 + (a_rate - 1.0) * ka_ref[...])
    bonus = _dot(r * kmod * rk_ref[...], ones_bd) * v

    bonus_o[0] = bonus.astype(BF16)
    g_o[0] = gate.astype(BF16)
    r16, k16, v16 = r.astype(BF16), kmod.astype(BF16), v.astype(BF16)
    kn16, b16 = knorm.astype(BF16), (knorm * a_rate).astype(BF16)
    for pair in range(N_PAIRS):
        sl = slice(pair * PAIR, (pair + 1) * PAIR)
        r_o[0, pair] = r16[:, sl]
        k_o[0, pair] = k16[:, sl]
        v_o[0, pair] = v16[:, sl]
        lw_o[0, pair] = log_decay[:, sl]
        kn_o[0, pair] = kn16[:, sl]
        b_o[0, pair] = b16[:, sl]


def _rwkv_prep(p3, vecs, w2p, a2p, g2p, ones_bd, *, tm):
    bsz, lp, _ = p3.shape
    grid = (bsz, lp // tm)
    vec_spec = pl.BlockSpec((1, D_MODEL), lambda b, i: (0, 0))
    pair_spec = pl.BlockSpec((1, N_PAIRS, tm, PAIR), lambda b, i: (b, 0, i, 0))
    tok_spec = pl.BlockSpec((1, tm, D_MODEL), lambda b, i: (b, i, 0))
    pair_shape = jax.ShapeDtypeStruct((bsz, N_PAIRS, lp, PAIR), BF16)
    decay_shape = jax.ShapeDtypeStruct((bsz, N_PAIRS, lp, PAIR), F32)
    tok_shape = jax.ShapeDtypeStruct((bsz, lp, D_MODEL), BF16)
    return pl.pallas_call(
        _rwkv_prep_kernel,
        grid=grid,
        in_specs=[pl.BlockSpec((1, tm, RWKV_COLS_PAD), lambda b, i: (b, i, 0))]
        + [vec_spec] * 5
        + [pl.BlockSpec((LANE, D_MODEL), lambda b, i: (0, 0)),
           pl.BlockSpec((LANE, D_MODEL), lambda b, i: (0, 0)),
           pl.BlockSpec((GATE_LORA_PAD, D_MODEL), lambda b, i: (0, 0)),
           pl.BlockSpec((D_MODEL, D_MODEL), lambda b, i: (0, 0))],
        out_specs=[pair_spec] * 6 + [tok_spec] * 2,
        out_shape=[pair_shape] * 3 + [decay_shape] + [pair_shape] * 2 + [tok_shape] * 2,
        compiler_params=_params(("arbitrary", "arbitrary")),
        name="rwkv_prep",
    )(p3, *vecs, w2p, a2p, g2p, ones_bd)


def _block_diag(x, lane_head0):
    return jnp.concatenate([jnp.where(lane_head0, x, 0.0), jnp.where(lane_head0, 0.0, x)], axis=0)


def _chunk_unit(r, k, v, lw, kn, b, state):
    n = 2 * CHUNK
    row = lax.broadcasted_iota(jnp.int32, (CHUNK, PAIR), 0)
    lane_head0 = lax.broadcasted_iota(jnp.int32, (CHUNK, PAIR), 1) < HEAD_DIM
    c = lw
    for s in (1, 2, 4, 8, 16, 32):
        c = c + jnp.where(row >= s, pltpu.roll(c, s, axis=0), 0.0)
    c_end = c[CHUNK - 1:CHUNK, :]
    e_neg = jnp.exp(-c)
    e_end = jnp.exp(c_end - c)
    a_t = _block_diag(-kn * jnp.exp(c - lw), lane_head0)
    r_t = _block_diag(r * jnp.exp(c), lane_head0)
    b_t = _block_diag(b * e_neg, lane_head0)
    k_t = _block_diag(k * e_neg, lane_head0)
    b_e = _block_diag(b * e_end, lane_head0)
    k_e = _block_diag(k * e_end, lane_head0)
    v_bd = _block_diag(v, lane_head0)

    yield
    g = _dot_nt(jnp.concatenate([a_t, r_t], axis=0), jnp.concatenate([b_t, k_t], axis=0))
    ri = lax.broadcasted_iota(jnp.int32, (n, n), 0)
    ci = lax.broadcasted_iota(jnp.int32, (n, n), 1)
    m_ab = jnp.where(ci < ri, g[:n, :n], 0.0)
    m_ak = jnp.where(ci < ri, g[:n, n:], 0.0)
    a_rb = jnp.where(ci <= ri, g[n:, :n], 0.0)
    a_rk = jnp.where(ci <= ri, g[n:, n:], 0.0)

    t_inv = jnp.where(ci == ri, 1.0, m_ab)
    yield
    power = _dot(m_ab, m_ab)
    w1 = _dot(m_ak, v_bd)
    for _ in range(4):
        yield
        prod = _dot(power, jnp.concatenate([power, t_inv], axis=1))
        power = prod[:, :n]
        t_inv = t_inv + prod[:, n:]
    yield
    t_inv = t_inv + _dot(power, t_inv)

    yield
    tw = _dot(t_inv, jnp.concatenate([w1, a_t], axis=1))
    u0 = tw[:, :n]
    a_hat = tw[:, n:]
    rhs = jnp.concatenate([jnp.concatenate([a_hat, u0], axis=1),
                           jnp.concatenate([jnp.zeros_like(v_bd), v_bd], axis=1)], axis=0)
    yield
    pq = _dot_tn(jnp.concatenate([b_e, k_e], axis=0), rhs)
    ry = _dot(jnp.concatenate([a_rb, a_rk], axis=1), rhs)
    decay_end = jnp.exp(jnp.broadcast_to(c_end, (n, n)))
    p_mat = pq[:, :n] + jnp.where(ci == ri, decay_end, 0.0)
    yield
    ys = _dot(jnp.concatenate([r_t + ry[:, :n], p_mat], axis=0), state)
    y_bd = ys[:n] + ry[:, n:]
    y = y_bd[:CHUNK] + y_bd[CHUNK:]
    new_state = ys[n:] + pq[:, n:]
    return y, new_state


def _interleave(generators):
    results = [None] * len(generators)
    pending = list(range(len(generators)))
    while pending:
        still = []
        for idx in pending:
            try:
                next(generators[idx])
                still.append(idx)
            except StopIteration as stop:
                results[idx] = stop.value
        pending = still
    return results


def _rwkv_chunk_kernel(r_ref, k_ref, v_ref, lw_ref, kn_ref, b_ref, y_ref, state_scr, *, n_units):
    @pl.when(pl.program_id(0) == 0)
    def _():
        state_scr[...] = jnp.zeros_like(state_scr)

    def body(g, carry):
        units = [g * UNIT_UNROLL + j for j in range(UNIT_UNROLL)]
        idx = [(u // N_PAIRS, u % N_PAIRS) for u in units]
        args = [(r_ref[bi, pi].astype(F32), k_ref[bi, pi].astype(F32), v_ref[bi, pi].astype(F32),
                 lw_ref[bi, pi], kn_ref[bi, pi].astype(F32), b_ref[bi, pi].astype(F32), state_scr[u])
                for u, (bi, pi) in zip(units, idx)]
        outs = _interleave([_chunk_unit(*a) for a in args])
        for u, (bi, pi), (y, new_state) in zip(units, idx, outs):
            y_ref[bi, pi] = y.astype(BF16)
            state_scr[u] = new_state
        return carry

    lax.fori_loop(0, n_units // UNIT_UNROLL, body, 0)


def _rwkv_chunk(r, k, v, lw, kn, b):
    bsz, _, lp, _ = r.shape
    n_units = bsz * N_PAIRS
    spec = pl.BlockSpec((bsz, N_PAIRS, CHUNK, PAIR), lambda c: (0, 0, c, 0))
    return pl.pallas_call(
        functools.partial(_rwkv_chunk_kernel, n_units=n_units),
        grid=(lp // CHUNK,),
        in_specs=[spec] * 6,
        out_specs=spec,
        out_shape=jax.ShapeDtypeStruct(r.shape, BF16),
        scratch_shapes=[pltpu.VMEM((n_units, PAIR, PAIR), F32)],
        compiler_params=_params(("arbitrary",)),
        name="rwkv_chunk",
    )(r, k, v, lw, kn, b)


def _diff_attn_kernel(q_ref, k_ref, v_ref, lq1_ref, lk1_ref, lq2_ref, lk2_ref, subw_ref, o_ref,
                      m_scr, acc_scr, s0_scr, *, tq, kv_tiles):
    i = pl.program_id(2)
    head0 = lax.broadcasted_iota(jnp.int32, (tq, PAIR), 1) < HEAD_DIM
    qq = []
    for t in range(ATTN_HEADS_PER_STEP):
        q = q_ref[0, :, t * PAIR:(t + 1) * PAIR]
        zero = jnp.zeros_like(q)
        qq.append(jnp.concatenate([jnp.where(head0, q, zero), jnp.where(head0, zero, q)], axis=0))
    m_scr[...] = jnp.full_like(m_scr, -1e30)
    acc_scr[...] = jnp.zeros_like(acc_scr)

    def scores(t, j, width):
        start = pl.multiple_of(j * tq, tq)
        return _dot_nt(qq[t], k_ref[0, pl.ds(start, width), t * PAIR:(t + 1) * PAIR])

    def step(j, width, masked, s_first=None, emit_next_first=None):
        start = pl.multiple_of(j * tq, tq)
        ones = jnp.ones((width, PAIR), BF16)
        s_next = scores(0, j, width) if s_first is None else s_first
        for t in range(ATTN_HEADS_PER_STEP):
            s = s_next
            if t + 1 < ATTN_HEADS_PER_STEP:
                s_next = scores(t + 1, j, width)
            elif emit_next_first is not None:
                emit_next_first()
            if masked:
                rr = lax.broadcasted_iota(jnp.int32, (2 * tq, width), 0) % tq
                cc = lax.broadcasted_iota(jnp.int32, (2 * tq, width), 1)
                s = jnp.where(cc <= rr, s, -1e30)
            m_prev = m_scr[t]
            m_next = jnp.maximum(m_prev, jnp.max(s, axis=1, keepdims=True))
            p = jnp.exp2(s - jnp.tile(m_next, (1, width // LANE))).astype(BF16)
            alpha = jnp.exp2(m_prev - m_next)
            v_ext = jnp.concatenate([v_ref[0, pl.ds(start, width), t * PAIR:(t + 1) * PAIR], ones], axis=1)
            acc_scr[t] = jnp.tile(alpha, (1, 2)) * acc_scr[t] + jnp.dot(p, v_ext, preferred_element_type=F32)
            m_scr[t] = m_next

    n_big = i // kv_tiles
    big = kv_tiles * tq

    def big_body(j, carry):
        def emit_next_first():
            j_next = jnp.minimum(j + 1, n_big - 1)
            s0_scr[...] = scores(0, j_next * kv_tiles, big)

        step(j * kv_tiles, big, False, s_first=s0_scr[...], emit_next_first=emit_next_first)
        return carry

    def small_body(j, carry):
        step(j, tq, False)
        return carry

    def emit_first_big():
        s0_scr[...] = scores(0, 0, big)

    step(i, tq, True, emit_next_first=emit_first_big)
    lax.fori_loop(0, n_big, big_body, 0)
    lax.fori_loop(n_big * kv_tiles, i, small_body, 0)

    lam = (jnp.exp(jnp.sum(lq1_ref[...] * lk1_ref[...], axis=1, keepdims=True))
           - jnp.exp(jnp.sum(lq2_ref[...] * lk2_ref[...], axis=1, keepdims=True)) + LAMBDA_INIT)
    for t in range(ATTN_HEADS_PER_STEP):
        acc = acc_scr[t]
        o = acc[:, :PAIR] / acc[:, PAIR:]
        od = o[:tq] - lam * o[tq:]
        o_ref[0, :, t * PAIR:(t + 1) * PAIR] = (
            _rms(od, subw_ref[...], SUBLN_EPS) * (1.0 - LAMBDA_INIT)).astype(o_ref.dtype)


def _diff_attn(qk3, vg3, lams, subw, *, tq, kv_tiles):
    bsz, lp, _ = qk3.shape
    width = ATTN_HEADS_PER_STEP * PAIR
    n_groups = D_MODEL // width
    grid = (bsz, n_groups, lp // tq)
    q_spec = pl.BlockSpec((1, tq, width), lambda b, g, i: (b, i, g))
    k_spec = pl.BlockSpec((1, lp, width), lambda b, g, i: (b, 0, n_groups + g))
    v_spec = pl.BlockSpec((1, lp, width), lambda b, g, i: (b, 0, g))
    lam_spec = pl.BlockSpec((1, HEAD_DIM), lambda b, g, i: (0, 0))
    return pl.pallas_call(
        functools.partial(_diff_attn_kernel, tq=tq, kv_tiles=kv_tiles),
        grid=grid,
        in_specs=[q_spec, k_spec, v_spec, lam_spec, lam_spec, lam_spec, lam_spec,
                  pl.BlockSpec((1, PAIR), lambda b, g, i: (0, 0))],
        out_specs=q_spec,
        out_shape=jax.ShapeDtypeStruct((bsz, lp, D_MODEL), BF16),
        scratch_shapes=[pltpu.VMEM((ATTN_HEADS_PER_STEP, 2 * tq, PAIR), F32),
                        pltpu.VMEM((ATTN_HEADS_PER_STEP, 2 * tq, 2 * PAIR), F32),
                        pltpu.VMEM((2 * tq, kv_tiles * tq), F32)],
        compiler_params=_params(("arbitrary", "arbitrary", "arbitrary")),
        name="diff_attn",
    )(qk3, qk3, vg3, *lams, subw)


def _merge_kernel(y_ref, bonus_ref, g_ref, od_ref, gate_r_ref, gate_d_ref, res_ref, lnw_ref, lnb_ref,
                  ones_ref, wo_r_ref, wo_d_ref, wout_ref, o_ref):
    ones_bd = ones_ref[...]
    parts = []
    for pair in range(0, N_PAIRS, 2):
        y = jnp.concatenate([y_ref[0, pair], y_ref[0, pair + 1]], axis=1).astype(F32)
        mean = _dot(y, ones_bd) * (1.0 / HEAD_DIM)
        yc = y - mean
        var = _dot(yc * yc, ones_bd) * (1.0 / HEAD_DIM)
        parts.append(yc * lax.rsqrt(var + RWKV_LN_EPS))
    yn = jnp.concatenate(parts, axis=1)
    o_rwkv = (yn * lnw_ref[...] + lnb_ref[...] + bonus_ref[0].astype(F32)) * g_ref[0].astype(F32)
    br_rwkv = _dot(o_rwkv, wo_r_ref[...])
    br_diff = jnp.dot(od_ref[0], wo_d_ref[...], preferred_element_type=F32)
    merged = (_sigmoid(gate_r_ref[0].astype(F32)) * br_rwkv
              + _sigmoid(gate_d_ref[0].astype(F32)) * br_diff)
    o_ref[0] = res_ref[0] + _dot(merged, wout_ref[...])


def _merge(y, bonus, g, od, a3, res, lnw, lnb, ones_pair, wo_r, wo_d, wout, *, tm):
    bsz, lp, _ = res.shape
    grid = (bsz, lp // tm)
    tok = lambda c: pl.BlockSpec((1, tm, c), lambda b, i: (b, i, 0))
    vec = pl.BlockSpec((1, D_MODEL), lambda b, i: (0, 0))
    mat = pl.BlockSpec((D_MODEL, D_MODEL), lambda b, i: (0, 0))
    gate_r_spec = pl.BlockSpec((1, tm, D_MODEL), lambda b, i: (b, i, 1))
    gate_d_spec = pl.BlockSpec((1, tm, D_MODEL), lambda b, i: (b, i, 2))
    return pl.pallas_call(
        _merge_kernel,
        grid=grid,
        in_specs=[pl.BlockSpec((1, N_PAIRS, tm, PAIR), lambda b, i: (b, 0, i, 0)),
                  tok(D_MODEL), tok(D_MODEL), tok(D_MODEL), gate_r_spec, gate_d_spec, tok(D_MODEL), vec, vec,
                  pl.BlockSpec((2 * PAIR, 2 * PAIR), lambda b, i: (0, 0)), mat, mat, mat],
        out_specs=tok(D_MODEL),
        out_shape=jax.ShapeDtypeStruct(res.shape, F32),
        compiler_params=_params(("arbitrary", "arbitrary")),
        name="merge",
    )(y, bonus, g, od, a3, a3, res, lnw, lnb, ones_pair, wo_r, wo_d, wout)


def _mlp_kernel(x_ref, nw_ref, w1_ref, w2_ref, fw_ref, o_ref, *, n_ff_chunks):
    x = x_ref[...]
    h = _rms(x, nw_ref[...], NORM_EPS).astype(BF16)
    acc = x
    ff = D_FF // n_ff_chunks
    for c in range(n_ff_chunks):
        z = jnp.dot(h, w1_ref[:, c * ff:(c + 1) * ff], preferred_element_type=F32)
        act = jnp.square(jnp.maximum(z, 0.0))
        acc = acc + _dot(act, w2_ref[c * ff:(c + 1) * ff, :])
    o_ref[...] = _rms(acc, fw_ref[...], NORM_EPS)


def _mlp(xf, nw, w1, w2, fw, *, tm):
    t_tokens = xf.shape[0]
    tok = pl.BlockSpec((tm, D_MODEL), lambda i: (i, 0))
    vec = pl.BlockSpec((1, D_MODEL), lambda i: (0, 0))
    return pl.pallas_call(
        functools.partial(_mlp_kernel, n_ff_chunks=4),
        grid=(t_tokens // tm,),
        in_specs=[tok, vec, pl.BlockSpec((D_MODEL, D_FF), lambda i: (0, 0)),
                  pl.BlockSpec((D_FF, D_MODEL), lambda i: (0, 0)), vec],
        out_specs=tok,
        out_shape=jax.ShapeDtypeStruct(xf.shape, F32),
        compiler_params=_params(("arbitrary",)),
        name="mlp",
    )(xf, nw, w1, w2, fw)


def _pad_cols(a, width):
    return jnp.pad(a, ((0, 0), (0, width - a.shape[1])))


def _pad_rows(a, height):
    return jnp.pad(a, ((0, height - a.shape[0]), (0, 0)))


def _rope_tables(lp):
    inv = ROPE_THETA ** (-jnp.arange(0, ROPE_DIM, 2, dtype=F32) / ROPE_DIM)
    ang = jnp.arange(lp, dtype=F32)[:, None] * inv[None, :]
    cos, sin = jnp.cos(ang), jnp.sin(ang)
    rest = HEAD_DIM - ROPE_DIM
    ones = jnp.ones((lp, rest), F32)
    zeros = jnp.zeros((lp, rest), F32)
    zh = jnp.zeros_like(sin)
    c = jnp.concatenate([cos, cos, ones], axis=1)
    s1 = jnp.concatenate([-sin, zh, zeros], axis=1)
    s2 = jnp.concatenate([zh, sin, zeros], axis=1)
    return tuple(jnp.tile(t, (1, LANE // HEAD_DIM)) for t in (c, s1, s2))


def _pick(n, candidates):
    for c in candidates:
        if n % c == 0:
            return c
    raise ValueError(n)


def kernel(x, meta_tokens, norm_mix_w, w_in, rwkv_mu, rwkv_w0, rwkv_w2, rwkv_a0, rwkv_a2, rwkv_g2, rwkv_k_k, rwkv_k_a, rwkv_r_k, rwkv_ln_w, rwkv_ln_b, rwkv_w_o, diff_lq1, diff_lk1, diff_lq2, diff_lk2, diff_subln_w, diff_w_o, w_out, norm_mlp_w, mlp_w1, mlp_w2, final_norm_w):
    bsz, seq, _ = x.shape
    seqlen = seq + N_META_TOKENS
    lp = -(-seqlen // SEQ_ALIGN) * SEQ_ALIGN
    tm_proj = _pick(lp, (768, 512, 256))
    tm_tok = 256

    meta = jnp.broadcast_to(meta_tokens[None].astype(x.dtype), (bsz, N_META_TOKENS, D_MODEL))
    h_res = jnp.concatenate([meta, x, jnp.zeros((bsz, lp - seqlen, D_MODEL), x.dtype)], axis=1)
    hf = h_res.reshape(bsz * lp, D_MODEL)

    w = w_in[0]
    c3 = 3 * D_MODEL
    w_rwkv = jnp.concatenate(
        [w[:, :c3], _pad_cols(w[:, c3:c3 + DECAY_LORA], LANE),
         _pad_cols(w[:, c3 + DECAY_LORA:c3 + DECAY_LORA + AAA_LORA], LANE),
         _pad_cols(w[:, c3 + DECAY_LORA + AAA_LORA:c3 + DECAY_LORA + AAA_LORA + GATE_LORA], GATE_LORA_PAD)],
        axis=1).astype(BF16)
    rwkv_cols = c3 + DECAY_LORA + AAA_LORA + GATE_LORA
    w_diff = w[:, rwkv_cols:].astype(BF16)
    mu = rwkv_mu[0][None, :]
    mu_p = jnp.concatenate(
        [mu[:, :c3], _pad_cols(mu[:, c3:c3 + DECAY_LORA], LANE),
         _pad_cols(mu[:, c3 + DECAY_LORA:c3 + DECAY_LORA + AAA_LORA], LANE),
         _pad_cols(mu[:, c3 + DECAY_LORA + AAA_LORA:], GATE_LORA_PAD)], axis=1)
    nw = norm_mix_w[0][None, :]

    tiles_per_seq = lp // tm_proj
    proj = functools.partial(_in_proj, hf, nw, tm=tm_proj, tiles_per_seq=tiles_per_seq)
    p_rwkv = proj(w_rwkv, mode="shift", mu=mu_p)
    qk3 = proj(w_diff[:, :2 * D_MODEL], mode="rope", rope=_rope_tables(lp)).reshape(bsz, lp, 2 * D_MODEL)
    a3 = proj(w_diff[:, 2 * D_MODEL:], mode="plain").reshape(bsz, lp, 3 * D_MODEL)

    head_id = jnp.arange(D_MODEL) // HEAD_DIM
    ones_bd = (head_id[:, None] == head_id[None, :]).astype(BF16)
    vecs = (rwkv_w0[0][None, :], rwkv_a0[0][None, :], rwkv_k_k[0][None, :], rwkv_k_a[0][None, :],
            rwkv_r_k[0].reshape(1, D_MODEL))
    r, kmod, v, lw, kn, b, bonus, g = _rwkv_prep(
        p_rwkv.reshape(bsz, lp, RWKV_COLS_PAD), vecs,
        _pad_rows(rwkv_w2[0], LANE).astype(BF16), _pad_rows(rwkv_a2[0], LANE).astype(BF16),
        _pad_rows(rwkv_g2[0], GATE_LORA_PAD).astype(BF16), ones_bd, tm=tm_tok)
    y = _rwkv_chunk(r, kmod, v, lw, kn, b)

    o_diff = _diff_attn(qk3, a3, (diff_lq1, diff_lk1, diff_lq2, diff_lk2), diff_subln_w[0][None, :], tq=256,
                        kv_tiles=min(4, lp // 256))

    h_mid = _merge(y, bonus, g, o_diff, a3, h_res, rwkv_ln_w[0][None, :], rwkv_ln_b[0][None, :],
                   ones_bd[:2 * PAIR, :2 * PAIR], rwkv_w_o[0].astype(BF16), diff_w_o[0].astype(BF16),
                   w_out[0].astype(BF16), tm=tm_tok)
    out = _mlp(h_mid.reshape(bsz * lp, D_MODEL), norm_mlp_w[0][None, :], mlp_w1[0].astype(BF16),
               mlp_w2[0].astype(BF16), final_norm_w[None, :], tm=tm_tok)
    return out.reshape(bsz, lp, D_MODEL)[:, N_META_TOKENS:seqlen]
```

```python
import functools
import math

import jax
import jax.numpy as jnp
from jax import lax
from jax.experimental import pallas as pl
from jax.experimental.pallas import tpu as pltpu

F32 = jnp.float32
BF16 = jnp.bfloat16

D_MODEL = 1024
N_META_TOKENS = 16
HEAD_DIM = 64
PAIR = 2 * HEAD_DIM
N_PAIRS = D_MODEL // PAIR
DECAY_LORA = 64
AAA_LORA = 64
GATE_LORA = 160
RWKV_LN_EPS = 64e-5
ROPE_THETA = 500000.0
ROPE_DIM = HEAD_DIM // 4
ROPE_HALF = ROPE_DIM // 2
D_FF = 4 * D_MODEL
NORM_EPS = 1e-5
SUBLN_EPS = 1e-5
LAMBDA_INIT = 0.8 - 0.6 * math.exp(-0.3 * 0)

LANE = 128
SUBLANE = 8
CHUNK = 64
UNIT_UNROLL = 8
PROJ_COL_CHUNK = 256
ATTN_KEY_BLOCK = 256
ATTN_TAIL_ROWS = 16
ATTN_HEADS_PER_STEP = 4
Q_SCALE = HEAD_DIM ** -0.5 * math.log2(math.e)
SEQ_ALIGN = 256
VMEM_LIMIT = 56 * 1024 * 1024

LORA_W_OFF = 3 * D_MODEL
LORA_A_OFF = LORA_W_OFF + LANE
LORA_G_OFF = LORA_A_OFF + LANE
GATE_LORA_PAD = 2 * LANE
RWKV_COLS_PAD = LORA_G_OFF + GATE_LORA_PAD


def _params(sem):
    return pltpu.CompilerParams(dimension_semantics=sem, vmem_limit_bytes=VMEM_LIMIT)


def _rms(x, w, eps):
    return x * lax.rsqrt(jnp.mean(x * x, axis=-1, keepdims=True) + eps) * w


def _sigmoid(x):
    return 1.0 / (1.0 + jnp.exp(-x))


def _dot(a, b):
    return jnp.dot(a.astype(BF16), b.astype(BF16), preferred_element_type=F32)


def _dot_nt(a, b):
    return lax.dot_general(a.astype(BF16), b.astype(BF16), (((1,), (1,)), ((), ())),
                           preferred_element_type=F32)


def _dot_tn(a, b):
    return lax.dot_general(a.astype(BF16), b.astype(BF16), (((0,), (0,)), ((), ())),
                           preferred_element_type=F32)


def _in_proj_kernel(*refs, mode, tiles_per_seq):
    if mode == "shift":
        x_ref, xprev_ref, nw_ref, w_ref, mu_ref, o_ref = refs
    elif mode == "rope":
        x_ref, nw_ref, w_ref, cos_ref, s1_ref, s2_ref, o_ref = refs
    else:
        x_ref, nw_ref, w_ref, o_ref = refs
    tm, n_cols = o_ref.shape
    cc = PROJ_COL_CHUNK
    xn = _rms(x_ref[...], nw_ref[...], NORM_EPS).astype(BF16)
    if mode == "shift":
        xpn = _rms(xprev_ref[...], nw_ref[...], NORM_EPS).astype(BF16)
        first = (pl.program_id(0) % tiles_per_seq) == 0
        row = lax.broadcasted_iota(jnp.int32, (tm, cc), 0)
    elif mode == "rope":
        reps = cc // LANE
        cos = jnp.tile(cos_ref[...], (1, reps))
        s1 = jnp.tile(s1_ref[...], (1, reps))
        s2 = jnp.tile(s2_ref[...], (1, reps))
    for c in range(n_cols // cc):
        cols = slice(c * cc, (c + 1) * cc)
        w = w_ref[:, cols]
        p = jnp.dot(xn, w, preferred_element_type=F32)
        if mode == "shift":
            pp = jnp.dot(xpn, w, preferred_element_type=F32)
            prev_row = jnp.where(first, 0.0, pp[SUBLANE - 1:SUBLANE, :])
            shifted = jnp.where(row == 0, prev_row, pltpu.roll(p, 1, axis=0))
            p = p + (shifted - p) * mu_ref[:, cols]
        elif mode == "rope":
            nxt = pltpu.roll(p, cc - ROPE_HALF, axis=1)
            prv = pltpu.roll(p, ROPE_HALF, axis=1)
            scale = Q_SCALE if c * cc < D_MODEL else 1.0
            p = (p * cos + nxt * s1 + prv * s2) * scale
        o_ref[:, cols] = p.astype(o_ref.dtype)


def _in_proj(xf, nw, w, *, mode, mu=None, rope=None, tm, tiles_per_seq):
    t_tokens = xf.shape[0]
    n_cols = w.shape[1]
    x_spec = pl.BlockSpec((tm, D_MODEL), lambda i: (i, 0))
    nw_spec = pl.BlockSpec((1, D_MODEL), lambda i: (0, 0))
    w_spec = pl.BlockSpec((D_MODEL, n_cols), lambda i: (0, 0))
    if mode == "shift":
        rows8 = tm // SUBLANE
        prev_spec = pl.BlockSpec((SUBLANE, D_MODEL), lambda i: (jnp.maximum(i * rows8 - 1, 0), 0))
        in_specs = [x_spec, prev_spec, nw_spec, w_spec, pl.BlockSpec((1, n_cols), lambda i: (0, 0))]
        args = (xf, xf, nw, w, mu)
    elif mode == "rope":
        tab_spec = pl.BlockSpec((tm, LANE), lambda i: (i % tiles_per_seq, 0))
        in_specs = [x_spec, nw_spec, w_spec, tab_spec, tab_spec, tab_spec]
        args = (xf, nw, w) + tuple(rope)
    else:
        in_specs = [x_spec, nw_spec, w_spec]
        args = (xf, nw, w)
    return pl.pallas_call(
        functools.partial(_in_proj_kernel, mode=mode, tiles_per_seq=tiles_per_seq),
        grid=(t_tokens // tm,),
        in_specs=in_specs,
        out_specs=pl.BlockSpec((tm, n_cols), lambda i: (i, 0)),
        out_shape=jax.ShapeDtypeStruct((t_tokens, n_cols), BF16),
        compiler_params=_params(("arbitrary",)),
        name="in_proj_" + mode,
    )(*args)


def _rwkv_prep_kernel(p_ref, w0_ref, a0_ref, kk_ref, ka_ref, rk_ref, w2_ref, a2_ref, g2_ref,
                      ones_ref, r_o, k_o, v_o, lw_o, kn_o, b_o, bonus_o, g_o):
    r = p_ref[0, :, 0:D_MODEL].astype(F32)
    k = p_ref[0, :, D_MODEL:2 * D_MODEL].astype(F32)
    v = p_ref[0, :, 2 * D_MODEL:3 * D_MODEL].astype(F32)
    wd = p_ref[0, :, LORA_W_OFF:LORA_W_OFF + LANE].astype(F32)
    ad = p_ref[0, :, LORA_A_OFF:LORA_A_OFF + LANE]
    gd = p_ref[0, :, LORA_G_OFF:LORA_G_OFF + GATE_LORA_PAD].astype(F32)
    ones_bd = ones_ref[...]

    z = w0_ref[...] + _dot(jnp.tanh(wd), w2_ref[...])
    log_decay = -math.exp(-0.5) * _sigmoid(z)
    a_rate = _sigmoid(a0_ref[...] + _dot(ad, a2_ref[...]))
    gate = _dot(_sigmoid(gd), g2_ref[...])

    kraw = k * kk_ref[...]
    sumsq = _dot(kraw * kraw, ones_bd)
    knorm = kraw * lax.rsqrt(jnp.maximum(sumsq, 1e-24))
    kmod = k * (1.0 + (a_rate - 1.0) * ka_ref[...])
    bonus = _dot(r * kmod * rk_ref[...], ones_bd) * v

    bonus_o[0] = bonus.astype(BF16)
    g_o[0] = gate.astype(BF16)
    r16, k16, v16 = r.astype(BF16), kmod.astype(BF16), v.astype(BF16)
    kn16, b16 = knorm.astype(BF16), (knorm * a_rate).astype(BF16)
    for pair in range(N_PAIRS):
        sl = slice(pair * PAIR, (pair + 1) * PAIR)
        r_o[0, pair] = r16[:, sl]
        k_o[0, pair] = k16[:, sl]
        v_o[0, pair] = v16[:, sl]
        lw_o[0, pair] = log_decay[:, sl]
        kn_o[0, pair] = kn16[:, sl]
        b_o[0, pair] = b16[:, sl]


def _rwkv_prep(p3, vecs, w2p, a2p, g2p, ones_bd, *, tm):
    bsz, lp, _ = p3.shape
    grid = (bsz, lp // tm)
    vec_spec = pl.BlockSpec((1, D_MODEL), lambda b, i: (0, 0))
    pair_spec = pl.BlockSpec((1, N_PAIRS, tm, PAIR), lambda b, i: (b, 0, i, 0))
    tok_spec = pl.BlockSpec((1, tm, D_MODEL), lambda b, i: (b, i, 0))
    pair_shape = jax.ShapeDtypeStruct((bsz, N_PAIRS, lp, PAIR), BF16)
    decay_shape = jax.ShapeDtypeStruct((bsz, N_PAIRS, lp, PAIR), F32)
    tok_shape = jax.ShapeDtypeStruct((bsz, lp, D_MODEL), BF16)
    return pl.pallas_call(
        _rwkv_prep_kernel,
        grid=grid,
        in_specs=[pl.BlockSpec((1, tm, RWKV_COLS_PAD), lambda b, i: (b, i, 0))]
        + [vec_spec] * 5
        + [pl.BlockSpec((LANE, D_MODEL), lambda b, i: (0, 0)),
           pl.BlockSpec((LANE, D_MODEL), lambda b, i: (0, 0)),
           pl.BlockSpec((GATE_LORA_PAD, D_MODEL), lambda b, i: (0, 0)),
           pl.BlockSpec((D_MODEL, D_MODEL), lambda b, i: (0, 0))],
        out_specs=[pair_spec] * 6 + [tok_spec] * 2,
        out_shape=[pair_shape] * 3 + [decay_shape] + [pair_shape] * 2 + [tok_shape] * 2,
        compiler_params=_params(("arbitrary", "arbitrary")),
        name="rwkv_prep",
    )(p3, *vecs, w2p, a2p, g2p, ones_bd)


def _block_diag(x, lane_head0):
    return jnp.concatenate([jnp.where(lane_head0, x, 0.0), jnp.where(lane_head0, 0.0, x)], axis=0)


def _chunk_unit(r, k, v, lw, kn, b, state):
    n = 2 * CHUNK
    row = lax.broadcasted_iota(jnp.int32, (CHUNK, PAIR), 0)
    lane_head0 = lax.broadcasted_iota(jnp.int32, (CHUNK, PAIR), 1) < HEAD_DIM
    c = lw
    for s in (1, 2, 4, 8, 16, 32):
        c = c + jnp.where(row >= s, pltpu.roll(c, s, axis=0), 0.0)
    c_end = c[CHUNK - 1:CHUNK, :]
    e_neg = jnp.exp(-c)
    e_end = jnp.exp(c_end - c)
    a_t = _block_diag(-kn * jnp.exp(c - lw), lane_head0)
    r_t = _block_diag(r * jnp.exp(c), lane_head0)
    b_t = _block_diag(b * e_neg, lane_head0)
    k_t = _block_diag(k * e_neg, lane_head0)
    b_e = _block_diag(b * e_end, lane_head0)
    k_e = _block_diag(k * e_end, lane_head0)
    v_bd = _block_diag(v, lane_head0)

    yield
    g = _dot_nt(jnp.concatenate([a_t, r_t], axis=0), jnp.concatenate([b_t, k_t], axis=0))
    ri = lax.broadcasted_iota(jnp.int32, (n, n), 0)
    ci = lax.broadcasted_iota(jnp.int32, (n, n), 1)
    m_ab = jnp.where(ci < ri, g[:n, :n], 0.0)
    m_ak = jnp.where(ci < ri, g[:n, n:], 0.0)
    a_rb = jnp.where(ci <= ri, g[n:, :n], 0.0)
    a_rk = jnp.where(ci <= ri, g[n:, n:], 0.0)

    t_inv = jnp.where(ci == ri, 1.0, m_ab)
    yield
    power = _dot(m_ab, m_ab)
    w1 = _dot(m_ak, v_bd)
    for _ in range(4):
        yield
        prod = _dot(power, jnp.concatenate([power, t_inv], axis=1))
        power = prod[:, :n]
        t_inv = t_inv + prod[:, n:]
    yield
    t_inv = t_inv + _dot(power, t_inv)

    yield
    tw = _dot(t_inv, jnp.concatenate([w1, a_t], axis=1))
    u0 = tw[:, :n]
    a_hat = tw[:, n:]
    rhs = jnp.concatenate([jnp.concatenate([a_hat, u0], axis=1),
                           jnp.concatenate([jnp.zeros_like(v_bd), v_bd], axis=1)], axis=0)
    yield
    pq = _dot_tn(jnp.concatenate([b_e, k_e], axis=0), rhs)
    ry = _dot(jnp.concatenate([a_rb, a_rk], axis=1), rhs)
    decay_end = jnp.exp(jnp.broadcast_to(c_end, (n, n)))
    p_mat = pq[:, :n] + jnp.where(ci == ri, decay_end, 0.0)
    yield
    ys = _dot(jnp.concatenate([r_t + ry[:, :n], p_mat], axis=0), state)
    y_bd = ys[:n] + ry[:, n:]
    y = y_bd[:CHUNK] + y_bd[CHUNK:]
    new_state = ys[n:] + pq[:, n:]
    return y, new_state


def _interleave(generators):
    results = [None] * len(generators)
    pending = list(range(len(generators)))
    while pending:
        still = []
        for idx in pending:
            try:
                next(generators[idx])
                still.append(idx)
            except StopIteration as stop:
                results[idx] = stop.value
        pending = still
    return results


def _rwkv_chunk_kernel(r_ref, k_ref, v_ref, lw_ref, kn_ref, b_ref, y_ref, state_scr, *, n_units):
    @pl.when(pl.program_id(0) == 0)
    def _():
        state_scr[...] = jnp.zeros_like(state_scr)

    def body(g, carry):
        units = [g * UNIT_UNROLL + j for j in range(UNIT_UNROLL)]
        idx = [(u // N_PAIRS, u % N_PAIRS) for u in units]
        args = [(r_ref[bi, pi].astype(F32), k_ref[bi, pi].astype(F32), v_ref[bi, pi].astype(F32),
                 lw_ref[bi, pi], kn_ref[bi, pi].astype(F32), b_ref[bi, pi].astype(F32), state_scr[u])
                for u, (bi, pi) in zip(units, idx)]
        outs = _interleave([_chunk_unit(*a) for a in args])
        for u, (bi, pi), (y, new_state) in zip(units, idx, outs):
            y_ref[bi, pi] = y.astype(BF16)
            state_scr[u] = new_state
        return carry

    lax.fori_loop(0, n_units // UNIT_UNROLL, body, 0)


def _rwkv_chunk(r, k, v, lw, kn, b):
    bsz, _, lp, _ = r.shape
    n_units = bsz * N_PAIRS
    spec = pl.BlockSpec((bsz, N_PAIRS, CHUNK, PAIR), lambda c: (0, 0, c, 0))
    return pl.pallas_call(
        functools.partial(_rwkv_chunk_kernel, n_units=n_units),
        grid=(lp // CHUNK,),
        in_specs=[spec] * 6,
        out_specs=spec,
        out_shape=jax.ShapeDtypeStruct(r.shape, BF16),
        scratch_shapes=[pltpu.VMEM((n_units, PAIR, PAIR), F32)],
        compiler_params=_params(("arbitrary",)),
        name="rwkv_chunk",
    )(r, k, v, lw, kn, b)


def _diff_attn_kernel(q_ref, k_ref, v_ref, lq1_ref, lk1_ref, lq2_ref, lk2_ref, subw_ref,
                      *rest, tq, kv_tiles, block0, aliased):
    o_ref, m_scr, acc_scr, s0_scr = rest[1:] if aliased else rest
    kb = ATTN_KEY_BLOCK
    diag = block0 + (pl.program_id(2) * tq) // kb
    head0 = lax.broadcasted_iota(jnp.int32, (tq, PAIR), 1) < HEAD_DIM
    qq = []
    for t in range(ATTN_HEADS_PER_STEP):
        q = q_ref[0, :, t * PAIR:(t + 1) * PAIR]
        zero = jnp.zeros_like(q)
        qq.append(jnp.concatenate([jnp.where(head0, q, zero), jnp.where(head0, zero, q)], axis=0))
    m_scr[...] = jnp.full_like(m_scr, -1e30)
    acc_scr[...] = jnp.zeros_like(acc_scr)

    def scores(t, j, width):
        start = pl.multiple_of(j * kb, kb)
        return _dot_nt(qq[t], k_ref[0, pl.ds(start, width), t * PAIR:(t + 1) * PAIR])

    def step(j, width, masked, s_first=None, emit_next_first=None):
        start = pl.multiple_of(j * kb, kb)
        ones = jnp.ones((width, PAIR), BF16)
        s_next = scores(0, j, width) if s_first is None else s_first
        for t in range(ATTN_HEADS_PER_STEP):
            s = s_next
            if t + 1 < ATTN_HEADS_PER_STEP:
                s_next = scores(t + 1, j, width)
            elif emit_next_first is not None:
                emit_next_first()
            if masked:
                rr = lax.broadcasted_iota(jnp.int32, (2 * tq, width), 0) % tq
                cc = lax.broadcasted_iota(jnp.int32, (2 * tq, width), 1)
                s = jnp.where(cc <= rr, s, -1e30)
            m_prev = m_scr[t]
            m_next = jnp.maximum(m_prev, jnp.max(s, axis=1, keepdims=True))
            p = jnp.exp2(s - jnp.tile(m_next, (1, width // LANE))).astype(BF16)
            alpha = jnp.exp2(m_prev - m_next)
            v_ext = jnp.concatenate([v_ref[0, pl.ds(start, width), t * PAIR:(t + 1) * PAIR], ones], axis=1)
            acc_scr[t] = jnp.tile(alpha, (1, 2)) * acc_scr[t] + jnp.dot(p, v_ext, preferred_element_type=F32)
            m_scr[t] = m_next

    n_big = diag // kv_tiles
    big = kv_tiles * kb

    def big_body(j, carry):
        def emit_next_first():
            j_next = jnp.minimum(j + 1, n_big - 1)
            s0_scr[...] = scores(0, j_next * kv_tiles, big)

        step(j * kv_tiles, big, False, s_first=s0_scr[...], emit_next_first=emit_next_first)
        return carry

    def small_body(j, carry):
        step(j, kb, False)
        return carry

    def emit_first_big():
        s0_scr[...] = scores(0, 0, big)

    step(diag, kb, True, emit_next_first=emit_first_big)
    lax.fori_loop(0, n_big, big_body, 0)
    lax.fori_loop(n_big * kv_tiles, diag, small_body, 0)

    lam = (jnp.exp(jnp.sum(lq1_ref[...] * lk1_ref[...], axis=1, keepdims=True))
           - jnp.exp(jnp.sum(lq2_ref[...] * lk2_ref[...], axis=1, keepdims=True)) + LAMBDA_INIT)
    for t in range(ATTN_HEADS_PER_STEP):
        acc = acc_scr[t]
        o = acc[:, :PAIR] / acc[:, PAIR:]
        od = o[:tq] - lam * o[tq:]
        o_ref[0, :tq, t * PAIR:(t + 1) * PAIR] = (
            _rms(od, subw_ref[...], SUBLN_EPS) * (1.0 - LAMBDA_INIT)).astype(o_ref.dtype)
    if o_ref.shape[1] > tq:
        o_ref[0, tq:, :] = jnp.zeros((o_ref.shape[1] - tq, o_ref.shape[2]), o_ref.dtype)


def _diff_attn(qk3, vg3, lams, subw, *, tq, n_tiles, row0, kv_tiles, out_rows=None, prev_out=None):
    bsz, lp, _ = qk3.shape
    width = ATTN_HEADS_PER_STEP * PAIR
    n_groups = D_MODEL // width
    out_rows = tq if out_rows is None else out_rows
    assert row0 % tq == 0 and row0 % out_rows == 0 and row0 % ATTN_KEY_BLOCK == 0
    assert tq % ATTN_KEY_BLOCK == 0 or n_tiles == 1
    q_spec = pl.BlockSpec((1, tq, width), lambda b, g, i: (b, row0 // tq + i, g))
    k_spec = pl.BlockSpec((1, lp, width), lambda b, g, i: (b, 0, n_groups + g))
    v_spec = pl.BlockSpec((1, lp, width), lambda b, g, i: (b, 0, g))
    lam_spec = pl.BlockSpec((1, HEAD_DIM), lambda b, g, i: (0, 0))
    in_specs = [q_spec, k_spec, v_spec, lam_spec, lam_spec, lam_spec, lam_spec,
                pl.BlockSpec((1, PAIR), lambda b, g, i: (0, 0))]
    args = (qk3, qk3, vg3, *lams, subw)
    aliases = {}
    if prev_out is not None:
        aliases = {len(args): 0}
        in_specs.append(pl.BlockSpec(memory_space=pl.ANY))
        args = args + (prev_out,)
    return pl.pallas_call(
        functools.partial(_diff_attn_kernel, tq=tq, kv_tiles=kv_tiles, block0=row0 // ATTN_KEY_BLOCK,
                          aliased=prev_out is not None),
        grid=(bsz, n_groups, n_tiles),
        in_specs=in_specs,
        out_specs=pl.BlockSpec((1, out_rows, width), lambda b, g, i: (b, row0 // out_rows + i, g)),
        out_shape=jax.ShapeDtypeStruct((bsz, lp, D_MODEL), BF16),
        input_output_aliases=aliases,
        scratch_shapes=[pltpu.VMEM((ATTN_HEADS_PER_STEP, 2 * tq, PAIR), F32),
                        pltpu.VMEM((ATTN_HEADS_PER_STEP, 2 * tq, 2 * PAIR), F32),
                        pltpu.VMEM((2 * tq, kv_tiles * ATTN_KEY_BLOCK), F32)],
        compiler_params=_params(("arbitrary", "arbitrary", "arbitrary")),
        name="diff_attn" if prev_out is None else "diff_attn_tail",
    )(*args)


def _merge_kernel(y_ref, bonus_ref, g_ref, od_ref, gate_r_ref, gate_d_ref, res_ref, lnw_ref, lnb_ref,
                  ones_ref, wo_r_ref, wo_d_ref, wout_ref, o_ref):
    ones_bd = ones_ref[...]
    parts = []
    for pair in range(0, N_PAIRS, 2):
        y = jnp.concatenate([y_ref[0, pair], y_ref[0, pair + 1]], axis=1).astype(F32)
        mean = _dot(y, ones_bd) * (1.0 / HEAD_DIM)
        yc = y - mean
        var = _dot(yc * yc, ones_bd) * (1.0 / HEAD_DIM)
        parts.append(yc * lax.rsqrt(var + RWKV_LN_EPS))
    yn = jnp.concatenate(parts, axis=1)
    o_rwkv = (yn * lnw_ref[...] + lnb_ref[...] + bonus_ref[0].astype(F32)) * g_ref[0].astype(F32)
    br_rwkv = _dot(o_rwkv, wo_r_ref[...])
    br_diff = jnp.dot(od_ref[0], wo_d_ref[...], preferred_element_type=F32)
    merged = (_sigmoid(gate_r_ref[0].astype(F32)) * br_rwkv
              + _sigmoid(gate_d_ref[0].astype(F32)) * br_diff)
    o_ref[0] = res_ref[0] + _dot(merged, wout_ref[...])


def _merge(y, bonus, g, od, a3, res, lnw, lnb, ones_pair, wo_r, wo_d, wout, *, tm):
    bsz, lp, _ = res.shape
    grid = (bsz, lp // tm)
    tok = lambda c: pl.BlockSpec((1, tm, c), lambda b, i: (b, i, 0))
    vec = pl.BlockSpec((1, D_MODEL), lambda b, i: (0, 0))
    mat = pl.BlockSpec((D_MODEL, D_MODEL), lambda b, i: (0, 0))
    gate_r_spec = pl.BlockSpec((1, tm, D_MODEL), lambda b, i: (b, i, 1))
    gate_d_spec = pl.BlockSpec((1, tm, D_MODEL), lambda b, i: (b, i, 2))
    return pl.pallas_call(
        _merge_kernel,
        grid=grid,
        in_specs=[pl.BlockSpec((1, N_PAIRS, tm, PAIR), lambda b, i: (b, 0, i, 0)),
                  tok(D_MODEL), tok(D_MODEL), tok(D_MODEL), gate_r_spec, gate_d_spec, tok(D_MODEL), vec, vec,
                  pl.BlockSpec((2 * PAIR, 2 * PAIR), lambda b, i: (0, 0)), mat, mat, mat],
        out_specs=tok(D_MODEL),
        out_shape=jax.ShapeDtypeStruct(res.shape, F32),
        compiler_params=_params(("arbitrary", "arbitrary")),
        name="merge",
    )(y, bonus, g, od, a3, a3, res, lnw, lnb, ones_pair, wo_r, wo_d, wout)


def _mlp_kernel(x_ref, nw_ref, w1_ref, w2_ref, fw_ref, o_ref, *, n_ff_chunks):
    x = x_ref[...]
    h = _rms(x, nw_ref[...], NORM_EPS).astype(BF16)
    acc = x
    ff = D_FF // n_ff_chunks
    for c in range(n_ff_chunks):
        z = jnp.dot(h, w1_ref[:, c * ff:(c + 1) * ff], preferred_element_type=F32)
        act = jnp.square(jnp.maximum(z, 0.0))
        acc = acc + _dot(act, w2_ref[c * ff:(c + 1) * ff, :])
    o_ref[...] = _rms(acc, fw_ref[...], NORM_EPS)


def _mlp(x3, nw, w1, w2, fw, *, seq, row0, tm):
    bsz = x3.shape[0]
    x_spec = pl.BlockSpec((None, pl.Element(tm), pl.Element(D_MODEL)), lambda b, i: (b, pl.multiple_of(row0 + i * tm, SUBLANE), 0))
    vec = pl.BlockSpec((1, D_MODEL), lambda b, i: (0, 0))
    return pl.pallas_call(
        functools.partial(_mlp_kernel, n_ff_chunks=4),
        grid=(bsz, seq // tm),
        in_specs=[x_spec, vec, pl.BlockSpec((D_MODEL, D_FF), lambda b, i: (0, 0)),
                  pl.BlockSpec((D_FF, D_MODEL), lambda b, i: (0, 0)), vec],
        out_specs=pl.BlockSpec((None, tm, D_MODEL), lambda b, i: (b, i, 0)),
        out_shape=jax.ShapeDtypeStruct((bsz, seq, D_MODEL), F32),
        compiler_params=_params(("arbitrary", "arbitrary")),
        name="mlp",
    )(x3, nw, w1, w2, fw)


def _pad_cols(a, width):
    return jnp.pad(a, ((0, 0), (0, width - a.shape[1])))


def _pad_rows(a, height):
    return jnp.pad(a, ((0, height - a.shape[0]), (0, 0)))


def _rope_tables(lp):
    inv = ROPE_THETA ** (-jnp.arange(0, ROPE_DIM, 2, dtype=F32) / ROPE_DIM)
    ang = jnp.arange(lp, dtype=F32)[:, None] * inv[None, :]
    cos, sin = jnp.cos(ang), jnp.sin(ang)
    rest = HEAD_DIM - ROPE_DIM
    ones = jnp.ones((lp, rest), F32)
    zeros = jnp.zeros((lp, rest), F32)
    zh = jnp.zeros_like(sin)
    c = jnp.concatenate([cos, cos, ones], axis=1)
    s1 = jnp.concatenate([-sin, zh, zeros], axis=1)
    s2 = jnp.concatenate([zh, sin, zeros], axis=1)
    return tuple(jnp.tile(t, (1, LANE // HEAD_DIM)) for t in (c, s1, s2))


def _pick(n, candidates):
    for c in candidates:
        if n % c == 0:
            return c
    raise ValueError(n)


def kernel(x, meta_tokens, norm_mix_w, w_in, rwkv_mu, rwkv_w0, rwkv_w2, rwkv_a0, rwkv_a2, rwkv_g2, rwkv_k_k, rwkv_k_a, rwkv_r_k, rwkv_ln_w, rwkv_ln_b, rwkv_w_o, diff_lq1, diff_lk1, diff_lq2, diff_lk2, diff_subln_w, diff_w_o, w_out, norm_mlp_w, mlp_w1, mlp_w2, final_norm_w):
    bsz, seq, _ = x.shape
    seqlen = seq + N_META_TOKENS
    lp = -(-seqlen // SEQ_ALIGN) * SEQ_ALIGN
    tm_proj = _pick(lp, (768, 512, 256))
    tm_tok = 256

    meta = jnp.broadcast_to(meta_tokens[None].astype(x.dtype), (bsz, N_META_TOKENS, D_MODEL))
    h_res = jnp.concatenate([meta, x, jnp.zeros((bsz, lp - seqlen, D_MODEL), x.dtype)], axis=1)
    hf = h_res.reshape(bsz * lp, D_MODEL)

    w = w_in.reshape(D_MODEL, -1).astype(BF16)
    c3 = 3 * D_MODEL
    w_rwkv = jnp.concatenate(
        [w[:, :c3], _pad_cols(w[:, c3:c3 + DECAY_LORA], LANE),
         _pad_cols(w[:, c3 + DECAY_LORA:c3 + DECAY_LORA + AAA_LORA], LANE),
         _pad_cols(w[:, c3 + DECAY_LORA + AAA_LORA:c3 + DECAY_LORA + AAA_LORA + GATE_LORA], GATE_LORA_PAD)],
        axis=1)
    rwkv_cols = c3 + DECAY_LORA + AAA_LORA + GATE_LORA
    w_diff = w[:, rwkv_cols:]
    mu = rwkv_mu[0][None, :]
    mu_p = jnp.concatenate(
        [mu[:, :c3], _pad_cols(mu[:, c3:c3 + DECAY_LORA], LANE),
         _pad_cols(mu[:, c3 + DECAY_LORA:c3 + DECAY_LORA + AAA_LORA], LANE),
         _pad_cols(mu[:, c3 + DECAY_LORA + AAA_LORA:], GATE_LORA_PAD)], axis=1)
    nw = norm_mix_w[0][None, :]

    tiles_per_seq = lp // tm_proj
    proj = functools.partial(_in_proj, hf, nw, tm=tm_proj, tiles_per_seq=tiles_per_seq)
    p_rwkv = proj(w_rwkv, mode="shift", mu=mu_p)
    qk3 = proj(w_diff[:, :2 * D_MODEL], mode="rope", rope=_rope_tables(lp)).reshape(bsz, lp, 2 * D_MODEL)
    a3 = proj(w_diff[:, 2 * D_MODEL:], mode="plain").reshape(bsz, lp, 3 * D_MODEL)

    head_id = jnp.arange(D_MODEL) // HEAD_DIM
    ones_bd = (head_id[:, None] == head_id[None, :]).astype(BF16)
    vecs = (rwkv_w0[0][None, :], rwkv_a0[0][None, :], rwkv_k_k[0][None, :], rwkv_k_a[0][None, :],
            rwkv_r_k[0].reshape(1, D_MODEL))
    r, kmod, v, lw, kn, b, bonus, g = _rwkv_prep(
        p_rwkv.reshape(bsz, lp, RWKV_COLS_PAD), vecs,
        _pad_rows(rwkv_w2[0], LANE).astype(BF16), _pad_rows(rwkv_a2[0], LANE).astype(BF16),
        _pad_rows(rwkv_g2[0], GATE_LORA_PAD).astype(BF16), ones_bd, tm=tm_tok)
    y = _rwkv_chunk(r, kmod, v, lw, kn, b)

    lams = (diff_lq1, diff_lk1, diff_lq2, diff_lk2)
    subw = diff_subln_w[0][None, :]
    tq = ATTN_KEY_BLOCK
    kv_tiles = min(4, lp // tq)
    n_full = seqlen // tq
    tail = -(-(seqlen - n_full * tq) // ATTN_TAIL_ROWS) * ATTN_TAIL_ROWS
    o_diff = _diff_attn(qk3, a3, lams, subw, tq=tq, n_tiles=n_full if tail else lp // tq, row0=0,
                        kv_tiles=kv_tiles)
    if tail:
        o_diff = _diff_attn(qk3, a3, lams, subw, tq=tail, n_tiles=1, row0=n_full * tq, kv_tiles=kv_tiles,
                            out_rows=lp - n_full * tq, prev_out=o_diff)

    h_mid = _merge(y, bonus, g, o_diff, a3, h_res, rwkv_ln_w[0][None, :], rwkv_ln_b[0][None, :],
                   ones_bd[:2 * PAIR, :2 * PAIR], rwkv_w_o[0].astype(BF16), diff_w_o[0].astype(BF16),
                   w_out[0].astype(BF16), tm=tm_tok)
    return _mlp(h_mid, norm_mlp_w[0][None, :], mlp_w1[0].astype(BF16), mlp_w2[0].astype(BF16),
                final_norm_w[None, :], seq=seq, row0=N_META_TOKENS, tm=_pick(seq, (256, 128)))
```

```python
import functools
import math

import jax
import jax.numpy as jnp
from jax import lax
from jax.experimental import pallas as pl
from jax.experimental.pallas import tpu as pltpu

F32 = jnp.float32
BF16 = jnp.bfloat16

D_MODEL = 1024
N_META_TOKENS = 16
HEAD_DIM = 64
PAIR = 2 * HEAD_DIM
N_PAIRS = D_MODEL // PAIR
DECAY_LORA = 64
AAA_LORA = 64
GATE_LORA = 160
RWKV_LN_EPS = 64e-5
ROPE_THETA = 500000.0
ROPE_DIM = HEAD_DIM // 4
ROPE_HALF = ROPE_DIM // 2
D_FF = 4 * D_MODEL
NORM_EPS = 1e-5
SUBLN_EPS = 1e-5
LAMBDA_INIT = 0.8 - 0.6 * math.exp(-0.3 * 0)

LANE = 128
SUBLANE = 8
PREV_ROWS = 16
CHUNK = 64
UNIT_UNROLL = 8
PROJ_COL_CHUNK = 256
ATTN_KEY_BLOCK = 256
ATTN_TAIL_ROWS = 16
ATTN_HEADS_PER_STEP = 4
Q_SCALE = HEAD_DIM ** -0.5 * math.log2(math.e)
SEQ_ALIGN = 256
VMEM_LIMIT = 56 * 1024 * 1024

LORA_W_OFF = 3 * D_MODEL
LORA_A_OFF = LORA_W_OFF + LANE
LORA_G_OFF = LORA_A_OFF + LANE
GATE_LORA_PAD = 2 * LANE
RWKV_COLS_PAD = LORA_G_OFF + GATE_LORA_PAD


def _params(sem):
    return pltpu.CompilerParams(dimension_semantics=sem, vmem_limit_bytes=VMEM_LIMIT)


def _rms(x, w, eps):
    return x * lax.rsqrt(jnp.mean(x * x, axis=-1, keepdims=True) + eps) * w


def _sigmoid(x):
    return 1.0 / (1.0 + jnp.exp(-x))


def _dot(a, b):
    return jnp.dot(a.astype(BF16), b.astype(BF16), preferred_element_type=F32)


def _dot_nt(a, b):
    return lax.dot_general(a.astype(BF16), b.astype(BF16), (((1,), (1,)), ((), ())),
                           preferred_element_type=F32)


def _dot_tn(a, b):
    return lax.dot_general(a.astype(BF16), b.astype(BF16), (((0,), (0,)), ((), ())),
                           preferred_element_type=F32)


def _in_proj_kernel(*refs, mode, tiles_per_seq):
    if mode == "shift":
        x_ref, xprev_ref, nw_ref, w_ref, mu_ref, o_ref = refs
    elif mode == "rope":
        x_ref, nw_ref, w_ref, cos_ref, s1_ref, s2_ref, o_ref = refs
    else:
        x_ref, nw_ref, w_ref, o_ref = refs
    tm, n_cols = o_ref.shape
    cc = PROJ_COL_CHUNK
    xn = _rms(x_ref[...], nw_ref[...], NORM_EPS).astype(BF16)
    if mode == "shift":
        xn = jnp.concatenate([xn, _rms(xprev_ref[...], nw_ref[...], NORM_EPS).astype(BF16)], axis=0)
        first = (pl.program_id(0) % tiles_per_seq) == 0
        row = lax.broadcasted_iota(jnp.int32, (tm, cc), 0)
    elif mode == "rope":
        reps = cc // LANE
        cos = jnp.tile(cos_ref[...], (1, reps))
        s1 = jnp.tile(s1_ref[...], (1, reps))
        s2 = jnp.tile(s2_ref[...], (1, reps))
    for c in range(n_cols // cc):
        cols = slice(c * cc, (c + 1) * cc)
        w = w_ref[:, cols]
        p = jnp.dot(xn, w, preferred_element_type=F32)
        if mode == "shift":
            prev_row = jnp.where(first, 0.0, p[tm + PREV_ROWS - 1:tm + PREV_ROWS, :])
            p = p[:tm]
            shifted = jnp.where(row == 0, prev_row, pltpu.roll(p, 1, axis=0))
            p = p + (shifted - p) * mu_ref[:, cols]
        elif mode == "rope":
            nxt = pltpu.roll(p, cc - ROPE_HALF, axis=1)
            prv = pltpu.roll(p, ROPE_HALF, axis=1)
            scale = Q_SCALE if c * cc < D_MODEL else 1.0
            p = (p * cos + nxt * s1 + prv * s2) * scale
        o_ref[:, cols] = p.astype(o_ref.dtype)


def _in_proj(xf, nw, w, *, mode, mu=None, rope=None, tm, tiles_per_seq):
    t_tokens = xf.shape[0]
    n_cols = w.shape[1]
    x_spec = pl.BlockSpec((tm, D_MODEL), lambda i: (i, 0))
    nw_spec = pl.BlockSpec((1, D_MODEL), lambda i: (0, 0))
    w_spec = pl.BlockSpec((D_MODEL, n_cols), lambda i: (0, 0))
    if mode == "shift":
        blocks = tm // PREV_ROWS
        prev_spec = pl.BlockSpec((PREV_ROWS, D_MODEL), lambda i: (jnp.maximum(i * blocks - 1, 0), 0))
        in_specs = [x_spec, prev_spec, nw_spec, w_spec, pl.BlockSpec((1, n_cols), lambda i: (0, 0))]
        args = (xf, xf, nw, w, mu)
    elif mode == "rope":
        tab_spec = pl.BlockSpec((tm, LANE), lambda i: (i % tiles_per_seq, 0))
        in_specs = [x_spec, nw_spec, w_spec, tab_spec, tab_spec, tab_spec]
        args = (xf, nw, w) + tuple(rope)
    else:
        in_specs = [x_spec, nw_spec, w_spec]
        args = (xf, nw, w)
    return pl.pallas_call(
        functools.partial(_in_proj_kernel, mode=mode, tiles_per_seq=tiles_per_seq),
        grid=(t_tokens // tm,),
        in_specs=in_specs,
        out_specs=pl.BlockSpec((tm, n_cols), lambda i: (i, 0)),
        out_shape=jax.ShapeDtypeStruct((t_tokens, n_cols), BF16),
        compiler_params=_params(("arbitrary",)),
        name="in_proj_" + mode,
    )(*args)


def _rwkv_prep_kernel(p_ref, w0_ref, a0_ref, kk_ref, ka_ref, rk_ref, w2_ref, a2_ref, g2_ref,
                      ones_ref, r_o, k_o, v_o, lw_o, kn_o, b_o, bonus_o, g_o):
    tanh_wd = jnp.tanh(p_ref[0, :, LORA_W_OFF:LORA_W_OFF + LANE].astype(F32)).astype(BF16)
    ad = p_ref[0, :, LORA_A_OFF:LORA_A_OFF + LANE]
    sig_gd = _sigmoid(p_ref[0, :, LORA_G_OFF:LORA_G_OFF + GATE_LORA_PAD].astype(F32)).astype(BF16)
    ones_bd = ones_ref[...]
    blk = 2 * PAIR
    for c in range(D_MODEL // blk):
        cols = slice(c * blk, (c + 1) * blk)
        r16 = p_ref[0, :, c * blk:(c + 1) * blk]
        v16 = p_ref[0, :, 2 * D_MODEL + c * blk:2 * D_MODEL + (c + 1) * blk]
        k = p_ref[0, :, D_MODEL + c * blk:D_MODEL + (c + 1) * blk].astype(F32)
        z = w0_ref[:, cols] + jnp.dot(tanh_wd, w2_ref[:, cols], preferred_element_type=F32)
        log_decay = -math.exp(-0.5) * _sigmoid(z)
        a_rate = _sigmoid(a0_ref[:, cols] + jnp.dot(ad, a2_ref[:, cols], preferred_element_type=F32))
        gate = jnp.dot(sig_gd, g2_ref[:, cols], preferred_element_type=F32)

        kraw = k * kk_ref[:, cols]
        sumsq = _dot(kraw * kraw, ones_bd)
        knorm = kraw * lax.rsqrt(jnp.maximum(sumsq, 1e-24))
        kmod = k * (1.0 + (a_rate - 1.0) * ka_ref[:, cols])
        bonus = _dot(r16.astype(F32) * kmod * rk_ref[:, cols], ones_bd) * v16.astype(F32)

        bonus_o[0, :, cols] = bonus.astype(BF16)
        g_o[0, :, cols] = gate.astype(BF16)
        k16, kn16, b16 = kmod.astype(BF16), knorm.astype(BF16), (knorm * a_rate).astype(BF16)
        for h in range(2):
            pair = 2 * c + h
            sl = slice(h * PAIR, (h + 1) * PAIR)
            r_o[0, pair] = r16[:, sl]
            k_o[0, pair] = k16[:, sl]
            v_o[0, pair] = v16[:, sl]
            lw_o[0, pair] = log_decay[:, sl]
            kn_o[0, pair] = kn16[:, sl]
            b_o[0, pair] = b16[:, sl]


def _rwkv_prep(p3, vecs, w2p, a2p, g2p, ones_bd, *, tm):
    bsz, lp, _ = p3.shape
    grid = (bsz, lp // tm)
    vec_spec = pl.BlockSpec((1, D_MODEL), lambda b, i: (0, 0))
    pair_spec = pl.BlockSpec((1, N_PAIRS, tm, PAIR), lambda b, i: (b, 0, i, 0))
    tok_spec = pl.BlockSpec((1, tm, D_MODEL), lambda b, i: (b, i, 0))
    pair_shape = jax.ShapeDtypeStruct((bsz, N_PAIRS, lp, PAIR), BF16)
    decay_shape = jax.ShapeDtypeStruct((bsz, N_PAIRS, lp, PAIR), F32)
    tok_shape = jax.ShapeDtypeStruct((bsz, lp, D_MODEL), BF16)
    return pl.pallas_call(
        _rwkv_prep_kernel,
        grid=grid,
        in_specs=[pl.BlockSpec((1, tm, RWKV_COLS_PAD), lambda b, i: (b, i, 0))]
        + [vec_spec] * 5
        + [pl.BlockSpec((LANE, D_MODEL), lambda b, i: (0, 0)),
           pl.BlockSpec((LANE, D_MODEL), lambda b, i: (0, 0)),
           pl.BlockSpec((GATE_LORA_PAD, D_MODEL), lambda b, i: (0, 0)),
           pl.BlockSpec((2 * PAIR, 2 * PAIR), lambda b, i: (0, 0))],
        out_specs=[pair_spec] * 6 + [tok_spec] * 2,
        out_shape=[pair_shape] * 3 + [decay_shape] + [pair_shape] * 2 + [tok_shape] * 2,
        compiler_params=_params(("arbitrary", "arbitrary")),
        name="rwkv_prep",
    )(p3, *vecs, w2p, a2p, g2p, ones_bd)


def _block_diag(x, lane_head0):
    return jnp.concatenate([jnp.where(lane_head0, x, 0.0), jnp.where(lane_head0, 0.0, x)], axis=0)


def _chunk_unit(r, k, v, lw, kn, b, state):
    n = 2 * CHUNK
    row = lax.broadcasted_iota(jnp.int32, (CHUNK, PAIR), 0)
    lane_head0 = lax.broadcasted_iota(jnp.int32, (CHUNK, PAIR), 1) < HEAD_DIM
    c = lw
    for s in (1, 2, 4, 8, 16, 32):
        c = c + jnp.where(row >= s, pltpu.roll(c, s, axis=0), 0.0)
    c_end = c[CHUNK - 1:CHUNK, :]
    e_neg = jnp.exp(-c)
    e_end = jnp.exp(c_end - c)
    a_t = _block_diag(-kn * jnp.exp(c - lw), lane_head0)
    r_t = _block_diag(r * jnp.exp(c), lane_head0)
    b_t = _block_diag(b * e_neg, lane_head0)
    k_t = _block_diag(k * e_neg, lane_head0)
    b_e = _block_diag(b * e_end, lane_head0)
    k_e = _block_diag(k * e_end, lane_head0)
    v_bd = _block_diag(v, lane_head0)

    yield
    g = _dot_nt(jnp.concatenate([a_t, r_t], axis=0), jnp.concatenate([b_t, k_t], axis=0))
    ri = lax.broadcasted_iota(jnp.int32, (n, n), 0)
    ci = lax.broadcasted_iota(jnp.int32, (n, n), 1)
    m_ab = jnp.where(ci < ri, g[:n, :n], 0.0)
    m_ak = jnp.where(ci < ri, g[:n, n:], 0.0)
    a_rb = jnp.where(ci <= ri, g[n:, :n], 0.0)
    a_rk = jnp.where(ci <= ri, g[n:, n:], 0.0)

    t_inv = jnp.where(ci == ri, 1.0, m_ab)
    yield
    power = _dot(m_ab, m_ab)
    w1 = _dot(m_ak, v_bd)
    for _ in range(4):
        yield
        prod = _dot(power, jnp.concatenate([power, t_inv], axis=1))
        power = prod[:, :n]
        t_inv = t_inv + prod[:, n:]
    yield
    t_inv = t_inv + _dot(power, t_inv)

    yield
    tw = _dot(t_inv, jnp.concatenate([w1, a_t], axis=1))
    u0 = tw[:, :n]
    a_hat = tw[:, n:]
    rhs = jnp.concatenate([jnp.concatenate([a_hat, u0], axis=1),
                           jnp.concatenate([jnp.zeros_like(v_bd), v_bd], axis=1)], axis=0)
    yield
    pq = _dot_tn(jnp.concatenate([b_e, k_e], axis=0), rhs)
    ry = _dot(jnp.concatenate([a_rb, a_rk], axis=1), rhs)
    decay_end = jnp.exp(jnp.broadcast_to(c_end, (n, n)))
    p_mat = pq[:, :n] + jnp.where(ci == ri, decay_end, 0.0)
    yield
    ys = _dot(jnp.concatenate([r_t + ry[:, :n], p_mat], axis=0), state)
    y_bd = ys[:n] + ry[:, n:]
    y = y_bd[:CHUNK] + y_bd[CHUNK:]
    new_state = ys[n:] + pq[:, n:]
    return y, new_state


def _interleave(generators):
    results = [None] * len(generators)
    pending = list(range(len(generators)))
    while pending:
        still = []
        for idx in pending:
            try:
                next(generators[idx])
                still.append(idx)
            except StopIteration as stop:
                results[idx] = stop.value
        pending = still
    return results


def _rwkv_chunk_kernel(r_ref, k_ref, v_ref, lw_ref, kn_ref, b_ref, y_ref, state_scr, *, n_units):
    @pl.when(pl.program_id(0) == 0)
    def _():
        state_scr[...] = jnp.zeros_like(state_scr)

    def body(g, carry):
        units = [g * UNIT_UNROLL + j for j in range(UNIT_UNROLL)]
        idx = [(u // N_PAIRS, u % N_PAIRS) for u in units]
        args = [(r_ref[bi, pi].astype(F32), k_ref[bi, pi].astype(F32), v_ref[bi, pi].astype(F32),
                 lw_ref[bi, pi], kn_ref[bi, pi].astype(F32), b_ref[bi, pi].astype(F32), state_scr[u])
                for u, (bi, pi) in zip(units, idx)]
        outs = _interleave([_chunk_unit(*a) for a in args])
        for u, (bi, pi), (y, new_state) in zip(units, idx, outs):
            y_ref[bi, pi] = y.astype(BF16)
            state_scr[u] = new_state
        return carry

    lax.fori_loop(0, n_units // UNIT_UNROLL, body, 0)


def _rwkv_chunk(r, k, v, lw, kn, b):
    bsz, _, lp, _ = r.shape
    n_units = bsz * N_PAIRS
    spec = pl.BlockSpec((bsz, N_PAIRS, CHUNK, PAIR), lambda c: (0, 0, c, 0))
    return pl.pallas_call(
        functools.partial(_rwkv_chunk_kernel, n_units=n_units),
        grid=(lp // CHUNK,),
        in_specs=[spec] * 6,
        out_specs=spec,
        out_shape=jax.ShapeDtypeStruct(r.shape, BF16),
        scratch_shapes=[pltpu.VMEM((n_units, PAIR, PAIR), F32)],
        compiler_params=_params(("arbitrary",)),
        name="rwkv_chunk",
    )(r, k, v, lw, kn, b)


def _diff_attn_kernel(q_ref, k_ref, v_ref, lq1_ref, lk1_ref, lq2_ref, lk2_ref, subw_ref,
                      *rest, tq, kv_tiles, block0, aliased):
    o_ref, m_scr, acc_scr, s0_scr = rest[1:] if aliased else rest
    kb = ATTN_KEY_BLOCK
    diag = block0 + (pl.program_id(2) * tq) // kb
    head0 = lax.broadcasted_iota(jnp.int32, (tq, PAIR), 1) < HEAD_DIM
    qq = []
    for t in range(ATTN_HEADS_PER_STEP):
        q = q_ref[0, :, t * PAIR:(t + 1) * PAIR]
        zero = jnp.zeros_like(q)
        qq.append(jnp.concatenate([jnp.where(head0, q, zero), jnp.where(head0, zero, q)], axis=0))
    m_scr[...] = jnp.full_like(m_scr, -1e30)
    acc_scr[...] = jnp.zeros_like(acc_scr)

    def scores(t, j, width):
        start = pl.multiple_of(j * kb, kb)
        return _dot_nt(qq[t], k_ref[0, pl.ds(start, width), t * PAIR:(t + 1) * PAIR])

    def step(j, width, masked, s_first=None, emit_next_first=None):
        start = pl.multiple_of(j * kb, kb)
        ones = jnp.ones((width, PAIR), BF16)
        s_next = scores(0, j, width) if s_first is None else s_first
        for t in range(ATTN_HEADS_PER_STEP):
            s = s_next
            if t + 1 < ATTN_HEADS_PER_STEP:
                s_next = scores(t + 1, j, width)
            elif emit_next_first is not None:
                emit_next_first()
            if masked:
                rr = lax.broadcasted_iota(jnp.int32, (2 * tq, width), 0) % tq
                cc = lax.broadcasted_iota(jnp.int32, (2 * tq, width), 1)
                s = jnp.where(cc <= rr, s, -1e30)
            m_prev = m_scr[t]
            m_next = jnp.maximum(m_prev, jnp.max(s, axis=1, keepdims=True))
            p = jnp.exp2(s - jnp.tile(m_next, (1, width // LANE))).astype(BF16)
            alpha = jnp.exp2(m_prev - m_next)
            v_ext = jnp.concatenate([v_ref[0, pl.ds(start, width), t * PAIR:(t + 1) * PAIR], ones], axis=1)
            acc_scr[t] = jnp.tile(alpha, (1, 2)) * acc_scr[t] + jnp.dot(p, v_ext, preferred_element_type=F32)
            m_scr[t] = m_next

    n_big = diag // kv_tiles
    big = kv_tiles * kb

    def big_body(j, carry):
        def emit_next_first():
            j_next = jnp.minimum(j + 1, n_big - 1)
            s0_scr[...] = scores(0, j_next * kv_tiles, big)

        step(j * kv_tiles, big, False, s_first=s0_scr[...], emit_next_first=emit_next_first)
        return carry

    def small_body(j, carry):
        step(j, kb, False)
        return carry

    def emit_first_big():
        s0_scr[...] = scores(0, 0, big)

    step(diag, kb, True, emit_next_first=emit_first_big)
    lax.fori_loop(0, n_big, big_body, 0)
    lax.fori_loop(n_big * kv_tiles, diag, small_body, 0)

    lam = (jnp.exp(jnp.sum(lq1_ref[...] * lk1_ref[...], axis=1, keepdims=True))
           - jnp.exp(jnp.sum(lq2_ref[...] * lk2_ref[...], axis=1, keepdims=True)) + LAMBDA_INIT)
    for t in range(ATTN_HEADS_PER_STEP):
        acc = acc_scr[t]
        o = acc[:, :PAIR] / acc[:, PAIR:]
        od = o[:tq] - lam * o[tq:]
        o_ref[0, :tq, t * PAIR:(t + 1) * PAIR] = (
            _rms(od, subw_ref[...], SUBLN_EPS) * (1.0 - LAMBDA_INIT)).astype(o_ref.dtype)
    if o_ref.shape[1] > tq:
        o_ref[0, tq:, :] = jnp.zeros((o_ref.shape[1] - tq, o_ref.shape[2]), o_ref.dtype)


def _diff_attn(qk3, vg3, lams, subw, *, tq, n_tiles, row0, kv_tiles, out_rows=None, prev_out=None):
    bsz, lp, _ = qk3.shape
    width = ATTN_HEADS_PER_STEP * PAIR
    n_groups = D_MODEL // width
    out_rows = tq if out_rows is None else out_rows
    assert row0 % tq == 0 and row0 % out_rows == 0 and row0 % ATTN_KEY_BLOCK == 0
    assert tq % ATTN_KEY_BLOCK == 0 or n_tiles == 1
    q_spec = pl.BlockSpec((1, tq, width), lambda b, g, i: (b, row0 // tq + i, g))
    k_spec = pl.BlockSpec((1, lp, width), lambda b, g, i: (b, 0, n_groups + g))
    v_spec = pl.BlockSpec((1, lp, width), lambda b, g, i: (b, 0, g))
    lam_spec = pl.BlockSpec((1, HEAD_DIM), lambda b, g, i: (0, 0))
    in_specs = [q_spec, k_spec, v_spec, lam_spec, lam_spec, lam_spec, lam_spec,
                pl.BlockSpec((1, PAIR), lambda b, g, i: (0, 0))]
    args = (qk3, qk3, vg3, *lams, subw)
    aliases = {}
    if prev_out is not None:
        aliases = {len(args): 0}
        in_specs.append(pl.BlockSpec(memory_space=pl.ANY))
        args = args + (prev_out,)
    return pl.pallas_call(
        functools.partial(_diff_attn_kernel, tq=tq, kv_tiles=kv_tiles, block0=row0 // ATTN_KEY_BLOCK,
                          aliased=prev_out is not None),
        grid=(bsz, n_groups, n_tiles),
        in_specs=in_specs,
        out_specs=pl.BlockSpec((1, out_rows, width), lambda b, g, i: (b, row0 // out_rows + i, g)),
        out_shape=jax.ShapeDtypeStruct((bsz, lp, D_MODEL), BF16),
        input_output_aliases=aliases,
        scratch_shapes=[pltpu.VMEM((ATTN_HEADS_PER_STEP, 2 * tq, PAIR), F32),
                        pltpu.VMEM((ATTN_HEADS_PER_STEP, 2 * tq, 2 * PAIR), F32),
                        pltpu.VMEM((2 * tq, kv_tiles * ATTN_KEY_BLOCK), F32)],
        compiler_params=_params(("arbitrary", "arbitrary", "arbitrary")),
        name="diff_attn" if prev_out is None else "diff_attn_tail",
    )(*args)


def _merge_kernel(y_ref, bonus_ref, g_ref, od_ref, gate_r_ref, gate_d_ref, res_ref, lnw_ref, lnb_ref,
                  ones_ref, wo_r_ref, wo_d_ref, wout_ref, o_ref):
    ones_bd = ones_ref[...]
    parts = []
    for pair in range(0, N_PAIRS, 2):
        y = jnp.concatenate([y_ref[0, pair], y_ref[0, pair + 1]], axis=1).astype(F32)
        mean = _dot(y, ones_bd) * (1.0 / HEAD_DIM)
        yc = y - mean
        var = _dot(yc * yc, ones_bd) * (1.0 / HEAD_DIM)
        parts.append(yc * lax.rsqrt(var + RWKV_LN_EPS))
    yn = jnp.concatenate(parts, axis=1)
    o_rwkv = (yn * lnw_ref[...] + lnb_ref[...] + bonus_ref[0].astype(F32)) * g_ref[0].astype(F32)
    br_rwkv = _dot(o_rwkv, wo_r_ref[...])
    br_diff = jnp.dot(od_ref[0], wo_d_ref[...], preferred_element_type=F32)
    merged = (_sigmoid(gate_r_ref[0].astype(F32)) * br_rwkv
              + _sigmoid(gate_d_ref[0].astype(F32)) * br_diff)
    o_ref[0] = res_ref[0] + _dot(merged, wout_ref[...])


def _merge(y, bonus, g, od, a3, res, lnw, lnb, ones_pair, wo_r, wo_d, wout, *, tm):
    bsz, lp, _ = res.shape
    grid = (bsz, lp // tm)
    tok = lambda c: pl.BlockSpec((1, tm, c), lambda b, i: (b, i, 0))
    vec = pl.BlockSpec((1, D_MODEL), lambda b, i: (0, 0))
    mat = pl.BlockSpec((D_MODEL, D_MODEL), lambda b, i: (0, 0))
    gate_r_spec = pl.BlockSpec((1, tm, D_MODEL), lambda b, i: (b, i, 1))
    gate_d_spec = pl.BlockSpec((1, tm, D_MODEL), lambda b, i: (b, i, 2))
    return pl.pallas_call(
        _merge_kernel,
        grid=grid,
        in_specs=[pl.BlockSpec((1, N_PAIRS, tm, PAIR), lambda b, i: (b, 0, i, 0)),
                  tok(D_MODEL), tok(D_MODEL), tok(D_MODEL), gate_r_spec, gate_d_spec, tok(D_MODEL), vec, vec,
                  pl.BlockSpec((2 * PAIR, 2 * PAIR), lambda b, i: (0, 0)), mat, mat, mat],
        out_specs=tok(D_MODEL),
        out_shape=jax.ShapeDtypeStruct(res.shape, F32),
        compiler_params=_params(("arbitrary", "arbitrary")),
        name="merge",
    )(y, bonus, g, od, a3, a3, res, lnw, lnb, ones_pair, wo_r, wo_d, wout)


def _mlp_kernel(x_ref, nw_ref, w1_ref, w2_ref, fw_ref, o_ref, *, n_ff_chunks):
    x = x_ref[...]
    h = _rms(x, nw_ref[...], NORM_EPS).astype(BF16)
    acc = x
    ff = D_FF // n_ff_chunks
    for c in range(n_ff_chunks):
        z = jnp.dot(h, w1_ref[:, c * ff:(c + 1) * ff], preferred_element_type=F32)
        act = jnp.square(jnp.maximum(z, 0.0))
        acc = acc + _dot(act, w2_ref[c * ff:(c + 1) * ff, :])
    o_ref[...] = _rms(acc, fw_ref[...], NORM_EPS)


def _mlp(x3, nw, w1, w2, fw, *, seq, row0, tm):
    bsz = x3.shape[0]
    x_spec = pl.BlockSpec((None, pl.Element(tm), pl.Element(D_MODEL)), lambda b, i: (b, pl.multiple_of(row0 + i * tm, SUBLANE), 0))
    vec = pl.BlockSpec((1, D_MODEL), lambda b, i: (0, 0))
    return pl.pallas_call(
        functools.partial(_mlp_kernel, n_ff_chunks=4),
        grid=(bsz, seq // tm),
        in_specs=[x_spec, vec, pl.BlockSpec((D_MODEL, D_FF), lambda b, i: (0, 0)),
                  pl.BlockSpec((D_FF, D_MODEL), lambda b, i: (0, 0)), vec],
        out_specs=pl.BlockSpec((None, tm, D_MODEL), lambda b, i: (b, i, 0)),
        out_shape=jax.ShapeDtypeStruct((bsz, seq, D_MODEL), F32),
        compiler_params=_params(("arbitrary", "arbitrary")),
        name="mlp",
    )(x3, nw, w1, w2, fw)


def _pad_cols(a, width):
    return jnp.pad(a, ((0, 0), (0, width - a.shape[1])))


def _pad_rows(a, height):
    return jnp.pad(a, ((0, height - a.shape[0]), (0, 0)))


def _rope_tables(lp):
    inv = ROPE_THETA ** (-jnp.arange(0, ROPE_DIM, 2, dtype=F32) / ROPE_DIM)
    ang = jnp.arange(lp, dtype=F32)[:, None] * inv[None, :]
    cos, sin = jnp.cos(ang), jnp.sin(ang)
    rest = HEAD_DIM - ROPE_DIM
    ones = jnp.ones((lp, rest), F32)
    zeros = jnp.zeros((lp, rest), F32)
    zh = jnp.zeros_like(sin)
    c = jnp.concatenate([cos, cos, ones], axis=1)
    s1 = jnp.concatenate([-sin, zh, zeros], axis=1)
    s2 = jnp.concatenate([zh, sin, zeros], axis=1)
    return tuple(jnp.tile(t, (1, LANE // HEAD_DIM)) for t in (c, s1, s2))


def _pick(n, candidates):
    for c in candidates:
        if n % c == 0:
            return c
    raise ValueError(n)


def kernel(x, meta_tokens, norm_mix_w, w_in, rwkv_mu, rwkv_w0, rwkv_w2, rwkv_a0, rwkv_a2, rwkv_g2, rwkv_k_k, rwkv_k_a, rwkv_r_k, rwkv_ln_w, rwkv_ln_b, rwkv_w_o, diff_lq1, diff_lk1, diff_lq2, diff_lk2, diff_subln_w, diff_w_o, w_out, norm_mlp_w, mlp_w1, mlp_w2, final_norm_w):
    bsz, seq, _ = x.shape
    seqlen = seq + N_META_TOKENS
    lp = -(-seqlen // SEQ_ALIGN) * SEQ_ALIGN
    tm_proj = _pick(lp, (768, 512, 256))
    tm_tok = _pick(lp, (384, 256))

    meta = jnp.broadcast_to(meta_tokens[None].astype(x.dtype), (bsz, N_META_TOKENS, D_MODEL))
    h_res = jnp.concatenate([meta, x, jnp.zeros((bsz, lp - seqlen, D_MODEL), x.dtype)], axis=1)
    hf = h_res.reshape(bsz * lp, D_MODEL)

    w = w_in.reshape(D_MODEL, -1).astype(BF16)
    c3 = 3 * D_MODEL
    w_rwkv = jnp.concatenate(
        [w[:, :c3], _pad_cols(w[:, c3:c3 + DECAY_LORA], LANE),
         _pad_cols(w[:, c3 + DECAY_LORA:c3 + DECAY_LORA + AAA_LORA], LANE),
         _pad_cols(w[:, c3 + DECAY_LORA + AAA_LORA:c3 + DECAY_LORA + AAA_LORA + GATE_LORA], GATE_LORA_PAD)],
        axis=1)
    rwkv_cols = c3 + DECAY_LORA + AAA_LORA + GATE_LORA
    w_diff = w[:, rwkv_cols:]
    mu = rwkv_mu[0][None, :]
    mu_p = jnp.concatenate(
        [mu[:, :c3], _pad_cols(mu[:, c3:c3 + DECAY_LORA], LANE),
         _pad_cols(mu[:, c3 + DECAY_LORA:c3 + DECAY_LORA + AAA_LORA], LANE),
         _pad_cols(mu[:, c3 + DECAY_LORA + AAA_LORA:], GATE_LORA_PAD)], axis=1)
    nw = norm_mix_w[0][None, :]

    tiles_per_seq = lp // tm_proj
    proj = functools.partial(_in_proj, hf, nw, tm=tm_proj, tiles_per_seq=tiles_per_seq)
    p_rwkv = proj(w_rwkv, mode="shift", mu=mu_p)
    qk3 = proj(w_diff[:, :2 * D_MODEL], mode="rope", rope=_rope_tables(lp)).reshape(bsz, lp, 2 * D_MODEL)
    a3 = proj(w_diff[:, 2 * D_MODEL:], mode="plain").reshape(bsz, lp, 3 * D_MODEL)

    head_id = jnp.arange(2 * PAIR) // HEAD_DIM
    ones_bd = (head_id[:, None] == head_id[None, :]).astype(BF16)
    vecs = (rwkv_w0[0][None, :], rwkv_a0[0][None, :], rwkv_k_k[0][None, :], rwkv_k_a[0][None, :],
            rwkv_r_k[0].reshape(1, D_MODEL))
    r, kmod, v, lw, kn, b, bonus, g = _rwkv_prep(
        p_rwkv.reshape(bsz, lp, RWKV_COLS_PAD), vecs,
        _pad_rows(rwkv_w2[0], LANE).astype(BF16), _pad_rows(rwkv_a2[0], LANE).astype(BF16),
        _pad_rows(rwkv_g2[0], GATE_LORA_PAD).astype(BF16), ones_bd, tm=tm_tok)
    y = _rwkv_chunk(r, kmod, v, lw, kn, b)

    lams = (diff_lq1, diff_lk1, diff_lq2, diff_lk2)
    subw = diff_subln_w[0][None, :]
    tq = ATTN_KEY_BLOCK
    kv_tiles = min(4, lp // tq)
    n_full = seqlen // tq
    tail = -(-(seqlen - n_full * tq) // ATTN_TAIL_ROWS) * ATTN_TAIL_ROWS
    o_diff = _diff_attn(qk3, a3, lams, subw, tq=tq, n_tiles=n_full if tail else lp // tq, row0=0,
                        kv_tiles=kv_tiles)
    if tail:
        o_diff = _diff_attn(qk3, a3, lams, subw, tq=tail, n_tiles=1, row0=n_full * tq, kv_tiles=kv_tiles,
                            out_rows=lp - n_full * tq, prev_out=o_diff)

    h_mid = _merge(y, bonus, g, o_diff, a3, h_res, rwkv_ln_w[0][None, :], rwkv_ln_b[0][None, :],
                   ones_bd, rwkv_w_o[0].astype(BF16), diff_w_o[0].astype(BF16),
                   w_out[0].astype(BF16), tm=tm_tok)
    return _mlp(h_mid, norm_mlp_w[0][None, :], mlp_w1[0].astype(BF16), mlp_w2[0].astype(BF16),
                final_norm_w[None, :], seq=seq, row0=N_META_TOKENS, tm=_pick(seq, (256, 128)))
```

```python
import functools
import math

import jax
import jax.numpy as jnp
from jax import lax
from jax.experimental import pallas as pl
from jax.experimental.pallas import tpu as pltpu

F32 = jnp.float32
BF16 = jnp.bfloat16

D_MODEL = 1024
N_META_TOKENS = 16
HEAD_DIM = 64
PAIR = 2 * HEAD_DIM
N_PAIRS = D_MODEL // PAIR
DECAY_LORA = 64
AAA_LORA = 64
GATE_LORA = 160
RWKV_LN_EPS = 64e-5
ROPE_THETA = 500000.0
ROPE_DIM = HEAD_DIM // 4
ROPE_HALF = ROPE_DIM // 2
D_FF = 4 * D_MODEL
NORM_EPS = 1e-5
SUBLN_EPS = 1e-5
LAMBDA_INIT = 0.8 - 0.6 * math.exp(-0.3 * 0)

LANE = 128
SUBLANE = 8
PREV_ROWS = 16
CHUNK = 64
UNIT_UNROLL = 8
PROJ_COL_CHUNK = 256
ATTN_KEY_BLOCK = 256
ATTN_Q_TILE = 512
ATTN_TAIL_ROWS = 16
ATTN_HEADS_PER_STEP = 4
Q_SCALE = HEAD_DIM ** -0.5 * math.log2(math.e)
SEQ_ALIGN = 256
VMEM_LIMIT = 56 * 1024 * 1024

LORA_W_OFF = 3 * D_MODEL
LORA_A_OFF = LORA_W_OFF + LANE
LORA_G_OFF = LORA_A_OFF + LANE
GATE_LORA_PAD = 2 * LANE
RWKV_COLS_PAD = LORA_G_OFF + GATE_LORA_PAD


def _params(sem):
    return pltpu.CompilerParams(dimension_semantics=sem, vmem_limit_bytes=VMEM_LIMIT)


def _rms(x, w, eps):
    return x * lax.rsqrt(jnp.mean(x * x, axis=-1, keepdims=True) + eps) * w


def _sigmoid(x):
    return 1.0 / (1.0 + jnp.exp(-x))


def _dot(a, b):
    return jnp.dot(a.astype(BF16), b.astype(BF16), preferred_element_type=F32)


def _dot_nt(a, b):
    return lax.dot_general(a.astype(BF16), b.astype(BF16), (((1,), (1,)), ((), ())),
                           preferred_element_type=F32)


def _dot_tn(a, b):
    return lax.dot_general(a.astype(BF16), b.astype(BF16), (((0,), (0,)), ((), ())),
                           preferred_element_type=F32)


def _in_proj_kernel(*refs, mode, tiles_per_seq):
    if mode == "shift":
        x_ref, xprev_ref, nw_ref, w_ref, mu_ref, o_ref = refs
    elif mode == "rope":
        x_ref, nw_ref, w_ref, cos_ref, s1_ref, s2_ref, o_ref = refs
    else:
        x_ref, nw_ref, w_ref, o_ref = refs
    tm, n_cols = o_ref.shape
    cc = PROJ_COL_CHUNK
    xn = _rms(x_ref[...], nw_ref[...], NORM_EPS).astype(BF16)
    if mode == "shift":
        xn = jnp.concatenate([xn, _rms(xprev_ref[...], nw_ref[...], NORM_EPS).astype(BF16)], axis=0)
        first = (pl.program_id(0) % tiles_per_seq) == 0
        row = lax.broadcasted_iota(jnp.int32, (tm, cc), 0)
    elif mode == "rope":
        reps = cc // LANE
        cos = jnp.tile(cos_ref[...], (1, reps))
        s1 = jnp.tile(s1_ref[...], (1, reps))
        s2 = jnp.tile(s2_ref[...], (1, reps))
    for c in range(n_cols // cc):
        cols = slice(c * cc, (c + 1) * cc)
        w = w_ref[:, cols]
        p = jnp.dot(xn, w, preferred_element_type=F32)
        if mode == "shift":
            prev_row = jnp.where(first, 0.0, p[tm + PREV_ROWS - 1:tm + PREV_ROWS, :])
            p = p[:tm]
            shifted = jnp.where(row == 0, prev_row, pltpu.roll(p, 1, axis=0))
            p = p + (shifted - p) * mu_ref[:, cols]
        elif mode == "rope":
            nxt = pltpu.roll(p, cc - ROPE_HALF, axis=1)
            prv = pltpu.roll(p, ROPE_HALF, axis=1)
            scale = Q_SCALE if c * cc < D_MODEL else 1.0
            p = (p * cos + nxt * s1 + prv * s2) * scale
        o_ref[:, cols] = p.astype(o_ref.dtype)


def _in_proj(xf, nw, w, *, mode, mu=None, rope=None, tm, tiles_per_seq):
    t_tokens = xf.shape[0]
    n_cols = w.shape[1]
    x_spec = pl.BlockSpec((tm, D_MODEL), lambda i: (i, 0))
    nw_spec = pl.BlockSpec((1, D_MODEL), lambda i: (0, 0))
    w_spec = pl.BlockSpec((D_MODEL, n_cols), lambda i: (0, 0))
    if mode == "shift":
        blocks = tm // PREV_ROWS
        prev_spec = pl.BlockSpec((PREV_ROWS, D_MODEL), lambda i: (jnp.maximum(i * blocks - 1, 0), 0))
        in_specs = [x_spec, prev_spec, nw_spec, w_spec, pl.BlockSpec((1, n_cols), lambda i: (0, 0))]
        args = (xf, xf, nw, w, mu)
    elif mode == "rope":
        tab_spec = pl.BlockSpec((tm, LANE), lambda i: (i % tiles_per_seq, 0))
        in_specs = [x_spec, nw_spec, w_spec, tab_spec, tab_spec, tab_spec]
        args = (xf, nw, w) + tuple(rope)
    else:
        in_specs = [x_spec, nw_spec, w_spec]
        args = (xf, nw, w)
    return pl.pallas_call(
        functools.partial(_in_proj_kernel, mode=mode, tiles_per_seq=tiles_per_seq),
        grid=(t_tokens // tm,),
        in_specs=in_specs,
        out_specs=pl.BlockSpec((tm, n_cols), lambda i: (i, 0)),
        out_shape=jax.ShapeDtypeStruct((t_tokens, n_cols), BF16),
        compiler_params=_params(("arbitrary",)),
        name="in_proj_" + mode,
    )(*args)


def _rwkv_prep_kernel(p_ref, w0_ref, a0_ref, kk_ref, ka_ref, rk_ref, w2_ref, a2_ref, g2_ref,
                      ones_ref, r_o, k_o, v_o, lw_o, kn_o, b_o, bonus_o, g_o):
    tanh_wd = jnp.tanh(p_ref[0, :, LORA_W_OFF:LORA_W_OFF + LANE].astype(F32)).astype(BF16)
    ad = p_ref[0, :, LORA_A_OFF:LORA_A_OFF + LANE]
    sig_gd = _sigmoid(p_ref[0, :, LORA_G_OFF:LORA_G_OFF + GATE_LORA_PAD].astype(F32)).astype(BF16)
    ones_bd = ones_ref[...]
    blk = 2 * PAIR
    for c in range(D_MODEL // blk):
        cols = slice(c * blk, (c + 1) * blk)
        r16 = p_ref[0, :, c * blk:(c + 1) * blk]
        v16 = p_ref[0, :, 2 * D_MODEL + c * blk:2 * D_MODEL + (c + 1) * blk]
        k = p_ref[0, :, D_MODEL + c * blk:D_MODEL + (c + 1) * blk].astype(F32)
        z = w0_ref[:, cols] + jnp.dot(tanh_wd, w2_ref[:, cols], preferred_element_type=F32)
        log_decay = -math.exp(-0.5) * _sigmoid(z)
        a_rate = _sigmoid(a0_ref[:, cols] + jnp.dot(ad, a2_ref[:, cols], preferred_element_type=F32))
        gate = jnp.dot(sig_gd, g2_ref[:, cols], preferred_element_type=F32)

        kraw = k * kk_ref[:, cols]
        sumsq = _dot(kraw * kraw, ones_bd)
        knorm = kraw * lax.rsqrt(jnp.maximum(sumsq, 1e-24))
        kmod = k * (1.0 + (a_rate - 1.0) * ka_ref[:, cols])
        bonus = _dot(r16.astype(F32) * kmod * rk_ref[:, cols], ones_bd) * v16.astype(F32)

        bonus_o[0, :, cols] = bonus.astype(BF16)
        g_o[0, :, cols] = gate.astype(BF16)
        k16, kn16, b16 = kmod.astype(BF16), knorm.astype(BF16), (knorm * a_rate).astype(BF16)
        for h in range(2):
            pair = 2 * c + h
            sl = slice(h * PAIR, (h + 1) * PAIR)
            r_o[0, pair] = r16[:, sl]
            k_o[0, pair] = k16[:, sl]
            v_o[0, pair] = v16[:, sl]
            lw_o[0, pair] = log_decay[:, sl]
            kn_o[0, pair] = kn16[:, sl]
            b_o[0, pair] = b16[:, sl]


def _rwkv_prep(p3, vecs, w2p, a2p, g2p, ones_bd, *, tm):
    bsz, lp, _ = p3.shape
    grid = (bsz, lp // tm)
    vec_spec = pl.BlockSpec((1, D_MODEL), lambda b, i: (0, 0))
    pair_spec = pl.BlockSpec((1, N_PAIRS, tm, PAIR), lambda b, i: (b, 0, i, 0))
    tok_spec = pl.BlockSpec((1, tm, D_MODEL), lambda b, i: (b, i, 0))
    pair_shape = jax.ShapeDtypeStruct((bsz, N_PAIRS, lp, PAIR), BF16)
    decay_shape = jax.ShapeDtypeStruct((bsz, N_PAIRS, lp, PAIR), F32)
    tok_shape = jax.ShapeDtypeStruct((bsz, lp, D_MODEL), BF16)
    return pl.pallas_call(
        _rwkv_prep_kernel,
        grid=grid,
        in_specs=[pl.BlockSpec((1, tm, RWKV_COLS_PAD), lambda b, i: (b, i, 0))]
        + [vec_spec] * 5
        + [pl.BlockSpec((LANE, D_MODEL), lambda b, i: (0, 0)),
           pl.BlockSpec((LANE, D_MODEL), lambda b, i: (0, 0)),
           pl.BlockSpec((GATE_LORA_PAD, D_MODEL), lambda b, i: (0, 0)),
           pl.BlockSpec((2 * PAIR, 2 * PAIR), lambda b, i: (0, 0))],
        out_specs=[pair_spec] * 6 + [tok_spec] * 2,
        out_shape=[pair_shape] * 3 + [decay_shape] + [pair_shape] * 2 + [tok_shape] * 2,
        compiler_params=_params(("arbitrary", "arbitrary")),
        name="rwkv_prep",
    )(p3, *vecs, w2p, a2p, g2p, ones_bd)


def _block_diag(x, lane_head0):
    return jnp.concatenate([jnp.where(lane_head0, x, 0.0), jnp.where(lane_head0, 0.0, x)], axis=0)


def _chunk_unit(r, k, v, lw, kn, b, state):
    n = 2 * CHUNK
    row = lax.broadcasted_iota(jnp.int32, (CHUNK, PAIR), 0)
    lane_head0 = lax.broadcasted_iota(jnp.int32, (CHUNK, PAIR), 1) < HEAD_DIM
    c = lw
    for s in (1, 2, 4, 8, 16, 32):
        c = c + jnp.where(row >= s, pltpu.roll(c, s, axis=0), 0.0)
    c_end = c[CHUNK - 1:CHUNK, :]
    e_neg = jnp.exp(-c)
    e_end = jnp.exp(c_end - c)
    a_t = _block_diag(-kn * jnp.exp(c - lw), lane_head0)
    r_t = _block_diag(r * jnp.exp(c), lane_head0)
    b_t = _block_diag(b * e_neg, lane_head0)
    k_t = _block_diag(k * e_neg, lane_head0)
    b_e = _block_diag(b * e_end, lane_head0)
    k_e = _block_diag(k * e_end, lane_head0)
    v_bd = _block_diag(v, lane_head0)

    yield
    g = _dot_nt(jnp.concatenate([a_t, r_t], axis=0), jnp.concatenate([b_t, k_t], axis=0))
    ri = lax.broadcasted_iota(jnp.int32, (n, n), 0)
    ci = lax.broadcasted_iota(jnp.int32, (n, n), 1)
    m_ab = jnp.where(ci < ri, g[:n, :n], 0.0)
    m_ak = jnp.where(ci < ri, g[:n, n:], 0.0)
    a_rb = jnp.where(ci <= ri, g[n:, :n], 0.0)
    a_rk = jnp.where(ci <= ri, g[n:, n:], 0.0)

    t_inv = jnp.where(ci == ri, 1.0, m_ab)
    yield
    power = _dot(m_ab, m_ab)
    w1 = _dot(m_ak, v_bd)
    for _ in range(4):
        yield
        prod = _dot(power, jnp.concatenate([power, t_inv], axis=1))
        power = prod[:, :n]
        t_inv = t_inv + prod[:, n:]
    yield
    t_inv = t_inv + _dot(power, t_inv)

    yield
    tw = _dot(t_inv, jnp.concatenate([w1, a_t], axis=1))
    u0 = tw[:, :n]
    a_hat = tw[:, n:]
    rhs = jnp.concatenate([jnp.concatenate([a_hat, u0], axis=1),
                           jnp.concatenate([jnp.zeros_like(v_bd), v_bd], axis=1)], axis=0)
    yield
    pq = _dot_tn(jnp.concatenate([b_e, k_e], axis=0), rhs)
    ry = _dot(jnp.concatenate([a_rb, a_rk], axis=1), rhs)
    decay_end = jnp.exp(jnp.broadcast_to(c_end, (n, n)))
    p_mat = pq[:, :n] + jnp.where(ci == ri, decay_end, 0.0)
    yield
    ys = _dot(jnp.concatenate([r_t + ry[:, :n], p_mat], axis=0), state)
    y_bd = ys[:n] + ry[:, n:]
    y = y_bd[:CHUNK] + y_bd[CHUNK:]
    new_state = ys[n:] + pq[:, n:]
    return y, new_state


def _interleave(generators):
    results = [None] * len(generators)
    pending = list(range(len(generators)))
    while pending:
        still = []
        for idx in pending:
            try:
                next(generators[idx])
                still.append(idx)
            except StopIteration as stop:
                results[idx] = stop.value
        pending = still
    return results


def _rwkv_chunk_kernel(r_ref, k_ref, v_ref, lw_ref, kn_ref, b_ref, y_ref, state_scr, *, n_units):
    @pl.when(pl.program_id(0) == 0)
    def _():
        state_scr[...] = jnp.zeros_like(state_scr)

    def body(g, carry):
        units = [g * UNIT_UNROLL + j for j in range(UNIT_UNROLL)]
        idx = [(u // N_PAIRS, u % N_PAIRS) for u in units]
        args = [(r_ref[bi, pi].astype(F32), k_ref[bi, pi].astype(F32), v_ref[bi, pi].astype(F32),
                 lw_ref[bi, pi], kn_ref[bi, pi].astype(F32), b_ref[bi, pi].astype(F32), state_scr[u])
                for u, (bi, pi) in zip(units, idx)]
        outs = _interleave([_chunk_unit(*a) for a in args])
        for u, (bi, pi), (y, new_state) in zip(units, idx, outs):
            y_ref[bi, pi] = y.astype(BF16)
            state_scr[u] = new_state
        return carry

    lax.fori_loop(0, n_units // UNIT_UNROLL, body, 0)


def _rwkv_chunk(r, k, v, lw, kn, b):
    bsz, _, lp, _ = r.shape
    n_units = bsz * N_PAIRS
    spec = pl.BlockSpec((bsz, N_PAIRS, CHUNK, PAIR), lambda c: (0, 0, c, 0))
    return pl.pallas_call(
        functools.partial(_rwkv_chunk_kernel, n_units=n_units),
        grid=(lp // CHUNK,),
        in_specs=[spec] * 6,
        out_specs=spec,
        out_shape=jax.ShapeDtypeStruct(r.shape, BF16),
        scratch_shapes=[pltpu.VMEM((n_units, PAIR, PAIR), F32)],
        compiler_params=_params(("arbitrary",)),
        name="rwkv_chunk",
    )(r, k, v, lw, kn, b)


def _diff_attn_kernel(q_ref, k_ref, v_ref, lq1_ref, lk1_ref, lq2_ref, lk2_ref, subw_ref,
                      *rest, tq, kv_tiles, block0, aliased):
    o_ref, m_scr, acc_scr, s0_scr = rest[1:] if aliased else rest
    kb = ATTN_KEY_BLOCK
    diag = block0 + (pl.program_id(2) * tq) // kb
    head0 = lax.broadcasted_iota(jnp.int32, (tq, PAIR), 1) < HEAD_DIM
    qq = []
    for t in range(ATTN_HEADS_PER_STEP):
        q = q_ref[0, :, t * PAIR:(t + 1) * PAIR]
        zero = jnp.zeros_like(q)
        qq.append(jnp.concatenate([jnp.where(head0, q, zero), jnp.where(head0, zero, q)], axis=0))

    def scores(t, j, width):
        start = pl.multiple_of(j * kb, kb)
        return _dot_nt(qq[t], k_ref[0, pl.ds(start, width), t * PAIR:(t + 1) * PAIR])

    def step(j, width, masked, s_first=None, emit_next_first=None):
        start = pl.multiple_of(j * kb, kb)
        ones = jnp.ones((width, PAIR), BF16)
        s_next = scores(0, j, width) if s_first is None else s_first
        for t in range(ATTN_HEADS_PER_STEP):
            s = s_next
            if t + 1 < ATTN_HEADS_PER_STEP:
                s_next = scores(t + 1, j, width)
            elif emit_next_first is not None:
                emit_next_first()
            if masked:
                rr = lax.broadcasted_iota(jnp.int32, (2 * tq, width), 0) % tq
                cc = lax.broadcasted_iota(jnp.int32, (2 * tq, width), 1)
                s = jnp.where(cc <= rr, s, -1e30)
            row_max = jnp.max(s, axis=1, keepdims=True)
            if masked:
                m_next = jnp.broadcast_to(row_max, (2 * tq, PAIR))
            else:
                m_prev = m_scr[t]
                m_next = jnp.maximum(m_prev, row_max)
            p = jnp.exp2(s - jnp.tile(m_next, (1, width // LANE))).astype(BF16)
            v_ext = jnp.concatenate([v_ref[0, pl.ds(start, width), t * PAIR:(t + 1) * PAIR], ones], axis=1)
            pv = jnp.dot(p, v_ext, preferred_element_type=F32)
            if masked:
                acc_scr[t] = pv
            else:
                acc_scr[t] = jnp.tile(jnp.exp2(m_prev - m_next), (1, 2)) * acc_scr[t] + pv
            m_scr[t] = m_next

    n_big = diag // kv_tiles
    big = kv_tiles * kb

    def big_body(j, carry):
        def emit_next_first():
            j_next = jnp.minimum(j + 1, n_big - 1)
            s0_scr[...] = scores(0, j_next * kv_tiles, big)

        step(j * kv_tiles, big, False, s_first=s0_scr[...], emit_next_first=emit_next_first)
        return carry

    def small_body(j, carry):
        step(j, kb, False)
        return carry

    def emit_first_big():
        s0_scr[...] = scores(0, 0, big)

    step(diag, max(tq, kb), True, emit_next_first=emit_first_big)
    lax.fori_loop(0, n_big, big_body, 0)
    lax.fori_loop(n_big * kv_tiles, diag, small_body, 0)

    lam = (jnp.exp(jnp.sum(lq1_ref[...] * lk1_ref[...], axis=1, keepdims=True))
           - jnp.exp(jnp.sum(lq2_ref[...] * lk2_ref[...], axis=1, keepdims=True)) + LAMBDA_INIT)
    for t in range(ATTN_HEADS_PER_STEP):
        acc = acc_scr[t]
        o = acc[:, :PAIR] / acc[:, PAIR:]
        od = o[:tq] - lam * o[tq:]
        o_ref[0, :tq, t * PAIR:(t + 1) * PAIR] = (
            _rms(od, subw_ref[...], SUBLN_EPS) * (1.0 - LAMBDA_INIT)).astype(o_ref.dtype)
    if o_ref.shape[1] > tq:
        o_ref[0, tq:, :] = jnp.zeros((o_ref.shape[1] - tq, o_ref.shape[2]), o_ref.dtype)


def _diff_attn(qk3, vg3, lams, subw, *, tq, n_tiles, row0, kv_tiles, out_rows=None, prev_out=None):
    bsz, lp, _ = qk3.shape
    width = ATTN_HEADS_PER_STEP * PAIR
    n_groups = D_MODEL // width
    out_rows = tq if out_rows is None else out_rows
    assert row0 % tq == 0 and row0 % out_rows == 0 and row0 % ATTN_KEY_BLOCK == 0
    assert tq % ATTN_KEY_BLOCK == 0 or n_tiles == 1
    q_spec = pl.BlockSpec((1, tq, width), lambda b, g, i: (b, row0 // tq + i, g))
    k_spec = pl.BlockSpec((1, lp, width), lambda b, g, i: (b, 0, n_groups + g))
    v_spec = pl.BlockSpec((1, lp, width), lambda b, g, i: (b, 0, g))
    lam_spec = pl.BlockSpec((1, HEAD_DIM), lambda b, g, i: (0, 0))
    in_specs = [q_spec, k_spec, v_spec, lam_spec, lam_spec, lam_spec, lam_spec,
                pl.BlockSpec((1, PAIR), lambda b, g, i: (0, 0))]
    args = (qk3, qk3, vg3, *lams, subw)
    aliases = {}
    if prev_out is not None:
        aliases = {len(args): 0}
        in_specs.append(pl.BlockSpec(memory_space=pl.ANY))
        args = args + (prev_out,)
    return pl.pallas_call(
        functools.partial(_diff_attn_kernel, tq=tq, kv_tiles=kv_tiles, block0=row0 // ATTN_KEY_BLOCK,
                          aliased=prev_out is not None),
        grid=(bsz, n_groups, n_tiles),
        in_specs=in_specs,
        out_specs=pl.BlockSpec((1, out_rows, width), lambda b, g, i: (b, row0 // out_rows + i, g)),
        out_shape=jax.ShapeDtypeStruct((bsz, lp, D_MODEL), BF16),
        input_output_aliases=aliases,
        scratch_shapes=[pltpu.VMEM((ATTN_HEADS_PER_STEP, 2 * tq, PAIR), F32),
                        pltpu.VMEM((ATTN_HEADS_PER_STEP, 2 * tq, 2 * PAIR), F32),
                        pltpu.VMEM((2 * tq, kv_tiles * ATTN_KEY_BLOCK), F32)],
        compiler_params=_params(("arbitrary", "arbitrary", "arbitrary")),
        name=f"diff_attn_q{tq}",
    )(*args)


def _merge_kernel(y_ref, bonus_ref, g_ref, od_ref, gate_r_ref, gate_d_ref, res_ref, lnw_ref, lnb_ref,
                  ones_ref, wo_r_ref, wo_d_ref, wout_ref, o_ref):
    ones_bd = ones_ref[...]
    parts = []
    for pair in range(0, N_PAIRS, 2):
        y = jnp.concatenate([y_ref[0, pair], y_ref[0, pair + 1]], axis=1).astype(F32)
        mean = _dot(y, ones_bd) * (1.0 / HEAD_DIM)
        yc = y - mean
        var = _dot(yc * yc, ones_bd) * (1.0 / HEAD_DIM)
        parts.append(yc * lax.rsqrt(var + RWKV_LN_EPS))
    yn = jnp.concatenate(parts, axis=1)
    o_rwkv = (yn * lnw_ref[...] + lnb_ref[...] + bonus_ref[0].astype(F32)) * g_ref[0].astype(F32)
    br_rwkv = _dot(o_rwkv, wo_r_ref[...])
    br_diff = jnp.dot(od_ref[0], wo_d_ref[...], preferred_element_type=F32)
    merged = (_sigmoid(gate_r_ref[0].astype(F32)) * br_rwkv
              + _sigmoid(gate_d_ref[0].astype(F32)) * br_diff)
    o_ref[0] = res_ref[0] + _dot(merged, wout_ref[...])


def _merge(y, bonus, g, od, a3, res, lnw, lnb, ones_pair, wo_r, wo_d, wout, *, tm):
    bsz, lp, _ = res.shape
    grid = (bsz, lp // tm)
    tok = lambda c: pl.BlockSpec((1, tm, c), lambda b, i: (b, i, 0))
    vec = pl.BlockSpec((1, D_MODEL), lambda b, i: (0, 0))
    mat = pl.BlockSpec((D_MODEL, D_MODEL), lambda b, i: (0, 0))
    gate_r_spec = pl.BlockSpec((1, tm, D_MODEL), lambda b, i: (b, i, 1))
    gate_d_spec = pl.BlockSpec((1, tm, D_MODEL), lambda b, i: (b, i, 2))
    return pl.pallas_call(
        _merge_kernel,
        grid=grid,
        in_specs=[pl.BlockSpec((1, N_PAIRS, tm, PAIR), lambda b, i: (b, 0, i, 0)),
                  tok(D_MODEL), tok(D_MODEL), tok(D_MODEL), gate_r_spec, gate_d_spec, tok(D_MODEL), vec, vec,
                  pl.BlockSpec((2 * PAIR, 2 * PAIR), lambda b, i: (0, 0)), mat, mat, mat],
        out_specs=tok(D_MODEL),
        out_shape=jax.ShapeDtypeStruct(res.shape, F32),
        compiler_params=_params(("arbitrary", "arbitrary")),
        name="merge",
    )(y, bonus, g, od, a3, a3, res, lnw, lnb, ones_pair, wo_r, wo_d, wout)


def _mlp_kernel(x_ref, nw_ref, w1_ref, w2_ref, fw_ref, o_ref, *, n_ff_chunks):
    x = x_ref[...]
    h = _rms(x, nw_ref[...], NORM_EPS).astype(BF16)
    acc = x
    ff = D_FF // n_ff_chunks
    for c in range(n_ff_chunks):
        z = jnp.dot(h, w1_ref[:, c * ff:(c + 1) * ff], preferred_element_type=F32)
        act = jnp.square(jnp.maximum(z, 0.0))
        acc = acc + _dot(act, w2_ref[c * ff:(c + 1) * ff, :])
    o_ref[...] = _rms(acc, fw_ref[...], NORM_EPS)


def _mlp(x3, nw, w1, w2, fw, *, seq, row0, tm):
    bsz = x3.shape[0]
    x_spec = pl.BlockSpec((None, pl.Element(tm), pl.Element(D_MODEL)), lambda b, i: (b, pl.multiple_of(row0 + i * tm, SUBLANE), 0))
    vec = pl.BlockSpec((1, D_MODEL), lambda b, i: (0, 0))
    return pl.pallas_call(
        functools.partial(_mlp_kernel, n_ff_chunks=4),
        grid=(bsz, seq // tm),
        in_specs=[x_spec, vec, pl.BlockSpec((D_MODEL, D_FF), lambda b, i: (0, 0)),
                  pl.BlockSpec((D_FF, D_MODEL), lambda b, i: (0, 0)), vec],
        out_specs=pl.BlockSpec((None, tm, D_MODEL), lambda b, i: (b, i, 0)),
        out_shape=jax.ShapeDtypeStruct((bsz, seq, D_MODEL), F32),
        compiler_params=_params(("arbitrary", "arbitrary")),
        name="mlp",
    )(x3, nw, w1, w2, fw)


def _pad_cols(a, width):
    return jnp.pad(a, ((0, 0), (0, width - a.shape[1])))


def _pad_rows(a, height):
    return jnp.pad(a, ((0, height - a.shape[0]), (0, 0)))


def _rope_tables(lp):
    inv = ROPE_THETA ** (-jnp.arange(0, ROPE_DIM, 2, dtype=F32) / ROPE_DIM)
    ang = jnp.arange(lp, dtype=F32)[:, None] * inv[None, :]
    cos, sin = jnp.cos(ang), jnp.sin(ang)
    rest = HEAD_DIM - ROPE_DIM
    ones = jnp.ones((lp, rest), F32)
    zeros = jnp.zeros((lp, rest), F32)
    zh = jnp.zeros_like(sin)
    c = jnp.concatenate([cos, cos, ones], axis=1)
    s1 = jnp.concatenate([-sin, zh, zeros], axis=1)
    s2 = jnp.concatenate([zh, sin, zeros], axis=1)
    return tuple(jnp.tile(t, (1, LANE // HEAD_DIM)) for t in (c, s1, s2))


def _pick(n, candidates):
    for c in candidates:
        if n % c == 0:
            return c
    raise ValueError(n)


def kernel(x, meta_tokens, norm_mix_w, w_in, rwkv_mu, rwkv_w0, rwkv_w2, rwkv_a0, rwkv_a2, rwkv_g2, rwkv_k_k, rwkv_k_a, rwkv_r_k, rwkv_ln_w, rwkv_ln_b, rwkv_w_o, diff_lq1, diff_lk1, diff_lq2, diff_lk2, diff_subln_w, diff_w_o, w_out, norm_mlp_w, mlp_w1, mlp_w2, final_norm_w):
    bsz, seq, _ = x.shape
    seqlen = seq + N_META_TOKENS
    lp = -(-seqlen // SEQ_ALIGN) * SEQ_ALIGN
    tm_proj = _pick(lp, (768, 512, 256))
    tm_tok = _pick(lp, (384, 256))

    meta = jnp.broadcast_to(meta_tokens[None].astype(x.dtype), (bsz, N_META_TOKENS, D_MODEL))
    h_res = jnp.concatenate([meta, x, jnp.zeros((bsz, lp - seqlen, D_MODEL), x.dtype)], axis=1)
    hf = h_res.reshape(bsz * lp, D_MODEL)

    w = w_in.reshape(D_MODEL, -1).astype(BF16)
    c3 = 3 * D_MODEL
    w_rwkv = jnp.concatenate(
        [w[:, :c3], _pad_cols(w[:, c3:c3 + DECAY_LORA], LANE),
         _pad_cols(w[:, c3 + DECAY_LORA:c3 + DECAY_LORA + AAA_LORA], LANE),
         _pad_cols(w[:, c3 + DECAY_LORA + AAA_LORA:c3 + DECAY_LORA + AAA_LORA + GATE_LORA], GATE_LORA_PAD)],
        axis=1)
    rwkv_cols = c3 + DECAY_LORA + AAA_LORA + GATE_LORA
    w_diff = w[:, rwkv_cols:]
    mu = rwkv_mu[0][None, :]
    mu_p = jnp.concatenate(
        [mu[:, :c3], _pad_cols(mu[:, c3:c3 + DECAY_LORA], LANE),
         _pad_cols(mu[:, c3 + DECAY_LORA:c3 + DECAY_LORA + AAA_LORA], LANE),
         _pad_cols(mu[:, c3 + DECAY_LORA + AAA_LORA:], GATE_LORA_PAD)], axis=1)
    nw = norm_mix_w[0][None, :]

    tiles_per_seq = lp // tm_proj
    proj = functools.partial(_in_proj, hf, nw, tm=tm_proj, tiles_per_seq=tiles_per_seq)
    p_rwkv = proj(w_rwkv, mode="shift", mu=mu_p)
    qk3 = proj(w_diff[:, :2 * D_MODEL], mode="rope", rope=_rope_tables(lp)).reshape(bsz, lp, 2 * D_MODEL)
    a3 = proj(w_diff[:, 2 * D_MODEL:], mode="plain").reshape(bsz, lp, 3 * D_MODEL)

    head_id = jnp.arange(2 * PAIR) // HEAD_DIM
    ones_bd = (head_id[:, None] == head_id[None, :]).astype(BF16)
    vecs = (rwkv_w0[0][None, :], rwkv_a0[0][None, :], rwkv_k_k[0][None, :], rwkv_k_a[0][None, :],
            rwkv_r_k[0].reshape(1, D_MODEL))
    r, kmod, v, lw, kn, b, bonus, g = _rwkv_prep(
        p_rwkv.reshape(bsz, lp, RWKV_COLS_PAD), vecs,
        _pad_rows(rwkv_w2[0], LANE).astype(BF16), _pad_rows(rwkv_a2[0], LANE).astype(BF16),
        _pad_rows(rwkv_g2[0], GATE_LORA_PAD).astype(BF16), ones_bd, tm=tm_tok)
    y = _rwkv_chunk(r, kmod, v, lw, kn, b)

    lams = (diff_lq1, diff_lk1, diff_lq2, diff_lk2)
    subw = diff_subln_w[0][None, :]
    kv_tiles = min(4, lp // ATTN_KEY_BLOCK)
    o_diff, row0 = None, 0
    for tq in (ATTN_Q_TILE, ATTN_KEY_BLOCK):
        n_tiles = (seqlen - row0) // tq
        if n_tiles:
            o_diff = _diff_attn(qk3, a3, lams, subw, tq=tq, n_tiles=n_tiles, row0=row0, kv_tiles=kv_tiles,
                                prev_out=o_diff)
            row0 += n_tiles * tq
    if row0 < lp:
        tail = max(-(-(seqlen - row0) // ATTN_TAIL_ROWS) * ATTN_TAIL_ROWS, ATTN_TAIL_ROWS)
        o_diff = _diff_attn(qk3, a3, lams, subw, tq=tail, n_tiles=1, row0=row0, kv_tiles=kv_tiles,
                            out_rows=lp - row0, prev_out=o_diff)

    h_mid = _merge(y, bonus, g, o_diff, a3, h_res, rwkv_ln_w[0][None, :], rwkv_ln_b[0][None, :],
                   ones_bd, rwkv_w_o[0].astype(BF16), diff_w_o[0].astype(BF16),
                   w_out[0].astype(BF16), tm=tm_tok)
    return _mlp(h_mid, norm_mlp_w[0][None, :], mlp_w1[0].astype(BF16), mlp_w2[0].astype(BF16),
                final_norm_w[None, :], seq=seq, row0=N_META_TOKENS, tm=_pick(seq, (256, 128)))
```

```python
import functools
import math

import jax
import jax.numpy as jnp
from jax import lax
from jax.experimental import pallas as pl
from jax.experimental.pallas import tpu as pltpu

F32 = jnp.float32
BF16 = jnp.bfloat16

D_MODEL = 1024
N_META_TOKENS = 16
HEAD_DIM = 64
PAIR = 2 * HEAD_DIM
N_PAIRS = D_MODEL // PAIR
DECAY_LORA = 64
AAA_LORA = 64
GATE_LORA = 160
RWKV_LN_EPS = 64e-5
ROPE_THETA = 500000.0
ROPE_DIM = HEAD_DIM // 4
ROPE_HALF = ROPE_DIM // 2
D_FF = 4 * D_MODEL
NORM_EPS = 1e-5
SUBLN_EPS = 1e-5
LAMBDA_INIT = 0.8 - 0.6 * math.exp(-0.3 * 0)

LANE = 128
SUBLANE = 8
PREV_ROWS = 16
CHUNK = 64
UNIT_UNROLL = 16
PROJ_COL_CHUNK = 256
ATTN_KEY_BLOCK = 256
ATTN_Q_TILE = 512
ATTN_TAIL_ROWS = 16
ATTN_HEADS_PER_STEP = 4
Q_SCALE = HEAD_DIM ** -0.5 * math.log2(math.e)
SEQ_ALIGN = 256
VMEM_LIMIT = 56 * 1024 * 1024

LORA_W_OFF = 3 * D_MODEL
LORA_A_OFF = LORA_W_OFF + LANE
LORA_G_OFF = LORA_A_OFF + LANE
GATE_LORA_PAD = 2 * LANE
RWKV_COLS_PAD = LORA_G_OFF + GATE_LORA_PAD


def _params(sem):
    return pltpu.CompilerParams(dimension_semantics=sem, vmem_limit_bytes=VMEM_LIMIT)


def _rms(x, w, eps):
    return x * lax.rsqrt(jnp.mean(x * x, axis=-1, keepdims=True) + eps) * w


def _sigmoid(x):
    return 1.0 / (1.0 + jnp.exp(-x))


def _dot(a, b):
    return jnp.dot(a.astype(BF16), b.astype(BF16), preferred_element_type=F32)


def _dot_nt(a, b):
    return lax.dot_general(a.astype(BF16), b.astype(BF16), (((1,), (1,)), ((), ())),
                           preferred_element_type=F32)


def _dot_tn(a, b):
    return lax.dot_general(a.astype(BF16), b.astype(BF16), (((0,), (0,)), ((), ())),
                           preferred_element_type=F32)


def _in_proj_kernel(*refs, mode, tiles_per_seq):
    if mode == "shift":
        x_ref, xprev_ref, nw_ref, w_ref, mu_ref, o_ref = refs
    elif mode == "rope":
        x_ref, nw_ref, w_ref, cos_ref, s1_ref, s2_ref, o_ref = refs
    else:
        x_ref, nw_ref, w_ref, o_ref = refs
    tm, n_cols = o_ref.shape
    cc = PROJ_COL_CHUNK
    xn = _rms(x_ref[...], nw_ref[...], NORM_EPS).astype(BF16)
    if mode == "shift":
        xn = jnp.concatenate([xn, _rms(xprev_ref[...], nw_ref[...], NORM_EPS).astype(BF16)], axis=0)
        first = (pl.program_id(0) % tiles_per_seq) == 0
        row = lax.broadcasted_iota(jnp.int32, (tm, cc), 0)
    elif mode == "rope":
        reps = cc // LANE
        cos = jnp.tile(cos_ref[...], (1, reps))
        s1 = jnp.tile(s1_ref[...], (1, reps))
        s2 = jnp.tile(s2_ref[...], (1, reps))
    for c in range(n_cols // cc):
        cols = slice(c * cc, (c + 1) * cc)
        w = w_ref[:, cols]
        p = jnp.dot(xn, w, preferred_element_type=F32)
        if mode == "shift":
            prev_row = jnp.where(first, 0.0, p[tm + PREV_ROWS - 1:tm + PREV_ROWS, :])
            p = p[:tm]
            shifted = jnp.where(row == 0, prev_row, pltpu.roll(p, 1, axis=0))
            p = p + (shifted - p) * mu_ref[:, cols]
        elif mode == "rope":
            nxt = pltpu.roll(p, cc - ROPE_HALF, axis=1)
            prv = pltpu.roll(p, ROPE_HALF, axis=1)
            scale = Q_SCALE if c * cc < D_MODEL else 1.0
            p = (p * cos + nxt * s1 + prv * s2) * scale
        o_ref[:, cols] = p.astype(o_ref.dtype)


def _in_proj(xf, nw, w, *, mode, mu=None, rope=None, tm, tiles_per_seq):
    t_tokens = xf.shape[0]
    n_cols = w.shape[1]
    x_spec = pl.BlockSpec((tm, D_MODEL), lambda i: (i, 0))
    nw_spec = pl.BlockSpec((1, D_MODEL), lambda i: (0, 0))
    w_spec = pl.BlockSpec((D_MODEL, n_cols), lambda i: (0, 0))
    if mode == "shift":
        blocks = tm // PREV_ROWS
        prev_spec = pl.BlockSpec((PREV_ROWS, D_MODEL), lambda i: (jnp.maximum(i * blocks - 1, 0), 0))
        in_specs = [x_spec, prev_spec, nw_spec, w_spec, pl.BlockSpec((1, n_cols), lambda i: (0, 0))]
        args = (xf, xf, nw, w, mu)
    elif mode == "rope":
        tab_spec = pl.BlockSpec((tm, LANE), lambda i: (i % tiles_per_seq, 0))
        in_specs = [x_spec, nw_spec, w_spec, tab_spec, tab_spec, tab_spec]
        args = (xf, nw, w) + tuple(rope)
    else:
        in_specs = [x_spec, nw_spec, w_spec]
        args = (xf, nw, w)
    return pl.pallas_call(
        functools.partial(_in_proj_kernel, mode=mode, tiles_per_seq=tiles_per_seq),
        grid=(t_tokens // tm,),
        in_specs=in_specs,
        out_specs=pl.BlockSpec((tm, n_cols), lambda i: (i, 0)),
        out_shape=jax.ShapeDtypeStruct((t_tokens, n_cols), BF16),
        compiler_params=_params(("arbitrary",)),
        name="in_proj_" + mode,
    )(*args)


def _rwkv_prep_kernel(p_ref, w0_ref, a0_ref, kk_ref, ka_ref, rk_ref, w2_ref, a2_ref, g2_ref,
                      ones_ref, r_o, k_o, v_o, lw_o, kn_o, b_o, bonus_o, g_o):
    tanh_wd = jnp.tanh(p_ref[0, :, LORA_W_OFF:LORA_W_OFF + LANE].astype(F32)).astype(BF16)
    ad = p_ref[0, :, LORA_A_OFF:LORA_A_OFF + LANE]
    sig_gd = _sigmoid(p_ref[0, :, LORA_G_OFF:LORA_G_OFF + GATE_LORA_PAD].astype(F32)).astype(BF16)
    ones_bd = ones_ref[...]
    blk = 2 * PAIR
    for c in range(D_MODEL // blk):
        cols = slice(c * blk, (c + 1) * blk)
        r16 = p_ref[0, :, c * blk:(c + 1) * blk]
        v16 = p_ref[0, :, 2 * D_MODEL + c * blk:2 * D_MODEL + (c + 1) * blk]
        k = p_ref[0, :, D_MODEL + c * blk:D_MODEL + (c + 1) * blk].astype(F32)
        z = w0_ref[:, cols] + jnp.dot(tanh_wd, w2_ref[:, cols], preferred_element_type=F32)
        log_decay = -math.exp(-0.5) * _sigmoid(z)
        a_rate = _sigmoid(a0_ref[:, cols] + jnp.dot(ad, a2_ref[:, cols], preferred_element_type=F32))
        gate = jnp.dot(sig_gd, g2_ref[:, cols], preferred_element_type=F32)

        kraw = k * kk_ref[:, cols]
        sumsq = _dot(kraw * kraw, ones_bd)
        knorm = kraw * lax.rsqrt(jnp.maximum(sumsq, 1e-24))
        kmod = k * (1.0 + (a_rate - 1.0) * ka_ref[:, cols])
        bonus = _dot(r16.astype(F32) * kmod * rk_ref[:, cols], ones_bd) * v16.astype(F32)

        bonus_o[0, :, cols] = bonus.astype(BF16)
        g_o[0, :, cols] = gate.astype(BF16)
        k16, kn16, b16 = kmod.astype(BF16), knorm.astype(BF16), (knorm * a_rate).astype(BF16)
        for h in range(2):
            pair = 2 * c + h
            sl = slice(h * PAIR, (h + 1) * PAIR)
            r_o[0, pair] = r16[:, sl]
            k_o[0, pair] = k16[:, sl]
            v_o[0, pair] = v16[:, sl]
            lw_o[0, pair] = log_decay[:, sl]
            kn_o[0, pair] = kn16[:, sl]
            b_o[0, pair] = b16[:, sl]


def _rwkv_prep(p3, vecs, w2p, a2p, g2p, ones_bd, *, tm):
    bsz, lp, _ = p3.shape
    grid = (bsz, lp // tm)
    vec_spec = pl.BlockSpec((1, D_MODEL), lambda b, i: (0, 0))
    pair_spec = pl.BlockSpec((1, N_PAIRS, tm, PAIR), lambda b, i: (b, 0, i, 0))
    tok_spec = pl.BlockSpec((1, tm, D_MODEL), lambda b, i: (b, i, 0))
    pair_shape = jax.ShapeDtypeStruct((bsz, N_PAIRS, lp, PAIR), BF16)
    decay_shape = jax.ShapeDtypeStruct((bsz, N_PAIRS, lp, PAIR), F32)
    tok_shape = jax.ShapeDtypeStruct((bsz, lp, D_MODEL), BF16)
    return pl.pallas_call(
        _rwkv_prep_kernel,
        grid=grid,
        in_specs=[pl.BlockSpec((1, tm, RWKV_COLS_PAD), lambda b, i: (b, i, 0))]
        + [vec_spec] * 5
        + [pl.BlockSpec((LANE, D_MODEL), lambda b, i: (0, 0)),
           pl.BlockSpec((LANE, D_MODEL), lambda b, i: (0, 0)),
           pl.BlockSpec((GATE_LORA_PAD, D_MODEL), lambda b, i: (0, 0)),
           pl.BlockSpec((2 * PAIR, 2 * PAIR), lambda b, i: (0, 0))],
        out_specs=[pair_spec] * 6 + [tok_spec] * 2,
        out_shape=[pair_shape] * 3 + [decay_shape] + [pair_shape] * 2 + [tok_shape] * 2,
        compiler_params=_params(("arbitrary", "arbitrary")),
        name="rwkv_prep",
    )(p3, *vecs, w2p, a2p, g2p, ones_bd)


def _block_diag(x, lane_head0):
    return jnp.concatenate([jnp.where(lane_head0, x, 0.0), jnp.where(lane_head0, 0.0, x)], axis=0)


def _chunk_unit(r, k, v, lw, kn, b, state):
    n = 2 * CHUNK
    row = lax.broadcasted_iota(jnp.int32, (CHUNK, PAIR), 0)
    lane_head0 = lax.broadcasted_iota(jnp.int32, (CHUNK, PAIR), 1) < HEAD_DIM
    c = lw
    for s in (1, 2, 4, 8, 16, 32):
        c = c + jnp.where(row >= s, pltpu.roll(c, s, axis=0), 0.0)
    c_end = c[CHUNK - 1:CHUNK, :]
    e_neg = jnp.exp(-c)
    e_end = jnp.exp(c_end - c)
    a_t = _block_diag(-kn * jnp.exp(c - lw), lane_head0)
    r_t = _block_diag(r * jnp.exp(c), lane_head0)
    b_t = _block_diag(b * e_neg, lane_head0)
    k_t = _block_diag(k * e_neg, lane_head0)
    b_e = _block_diag(b * e_end, lane_head0)
    k_e = _block_diag(k * e_end, lane_head0)
    v_bd = _block_diag(v, lane_head0)

    yield
    g = _dot_nt(jnp.concatenate([a_t, r_t], axis=0), jnp.concatenate([b_t, k_t], axis=0))
    ri = lax.broadcasted_iota(jnp.int32, (n, n), 0)
    ci = lax.broadcasted_iota(jnp.int32, (n, n), 1)
    m_ab = jnp.where(ci < ri, g[:n, :n], 0.0)
    m_ak = jnp.where(ci < ri, g[:n, n:], 0.0)
    a_rb = jnp.where(ci <= ri, g[n:, :n], 0.0)
    a_rk = jnp.where(ci <= ri, g[n:, n:], 0.0)

    t_inv = jnp.where(ci == ri, 1.0, m_ab)
    yield
    power = _dot(m_ab, m_ab)
    w1 = _dot(m_ak, v_bd)
    for _ in range(4):
        yield
        prod = _dot(power, jnp.concatenate([power, t_inv], axis=1))
        power = prod[:, :n]
        t_inv = t_inv + prod[:, n:]
    yield
    t_inv = t_inv + _dot(power, t_inv)

    yield
    tw = _dot(t_inv, jnp.concatenate([w1, a_t], axis=1))
    u0 = tw[:, :n]
    a_hat = tw[:, n:]
    rhs = jnp.concatenate([jnp.concatenate([a_hat, u0], axis=1),
                           jnp.concatenate([jnp.zeros_like(v_bd), v_bd], axis=1)], axis=0)
    yield
    pq = _dot_tn(jnp.concatenate([b_e, k_e], axis=0), rhs)
    ry = _dot(jnp.concatenate([a_rb, a_rk], axis=1), rhs)
    decay_end = jnp.exp(jnp.broadcast_to(c_end, (n, n)))
    p_mat = pq[:, :n] + jnp.where(ci == ri, decay_end, 0.0)
    yield
    ys = _dot(jnp.concatenate([r_t + ry[:, :n], p_mat], axis=0), state)
    y_bd = ys[:n] + ry[:, n:]
    y = y_bd[:CHUNK] + y_bd[CHUNK:]
    new_state = ys[n:] + pq[:, n:]
    return y, new_state


def _interleave(generators):
    results = [None] * len(generators)
    pending = list(range(len(generators)))
    while pending:
        still = []
        for idx in pending:
            try:
                next(generators[idx])
                still.append(idx)
            except StopIteration as stop:
                results[idx] = stop.value
        pending = still
    return results


def _rwkv_chunk_kernel(r_ref, k_ref, v_ref, lw_ref, kn_ref, b_ref, y_ref, state_scr, *, n_units):
    @pl.when(pl.program_id(0) == 0)
    def _():
        state_scr[...] = jnp.zeros_like(state_scr)

    def body(g, carry):
        units = [g * UNIT_UNROLL + j for j in range(UNIT_UNROLL)]
        idx = [(u // N_PAIRS, u % N_PAIRS) for u in units]
        args = [(r_ref[bi, pi].astype(F32), k_ref[bi, pi].astype(F32), v_ref[bi, pi].astype(F32),
                 lw_ref[bi, pi], kn_ref[bi, pi].astype(F32), b_ref[bi, pi].astype(F32), state_scr[u])
                for u, (bi, pi) in zip(units, idx)]
        outs = _interleave([_chunk_unit(*a) for a in args])
        for u, (bi, pi), (y, new_state) in zip(units, idx, outs):
            y_ref[bi, pi] = y.astype(BF16)
            state_scr[u] = new_state
        return carry

    lax.fori_loop(0, n_units // UNIT_UNROLL, body, 0)


def _rwkv_chunk(r, k, v, lw, kn, b):
    bsz, _, lp, _ = r.shape
    n_units = bsz * N_PAIRS
    spec = pl.BlockSpec((bsz, N_PAIRS, CHUNK, PAIR), lambda c: (0, 0, c, 0))
    return pl.pallas_call(
        functools.partial(_rwkv_chunk_kernel, n_units=n_units),
        grid=(lp // CHUNK,),
        in_specs=[spec] * 6,
        out_specs=spec,
        out_shape=jax.ShapeDtypeStruct(r.shape, BF16),
        scratch_shapes=[pltpu.VMEM((n_units, PAIR, PAIR), F32)],
        compiler_params=_params(("arbitrary",)),
        name="rwkv_chunk",
    )(r, k, v, lw, kn, b)


def _diff_attn_kernel(q_ref, k_ref, v_ref, lq1_ref, lk1_ref, lq2_ref, lk2_ref, subw_ref,
                      *rest, tq, kv_tiles, block0, aliased):
    o_ref, m_scr, acc_scr, s0_scr = rest[1:] if aliased else rest
    kb = ATTN_KEY_BLOCK
    diag = block0 + (pl.program_id(2) * tq) // kb
    head0 = lax.broadcasted_iota(jnp.int32, (tq, PAIR), 1) < HEAD_DIM
    qq = []
    for t in range(ATTN_HEADS_PER_STEP):
        q = q_ref[0, :, t * PAIR:(t + 1) * PAIR]
        zero = jnp.zeros_like(q)
        qq.append(jnp.concatenate([jnp.where(head0, q, zero), jnp.where(head0, zero, q)], axis=0))

    def scores(t, j, width):
        start = pl.multiple_of(j * kb, kb)
        return _dot_nt(qq[t], k_ref[0, pl.ds(start, width), t * PAIR:(t + 1) * PAIR])

    def step(j, width, masked, s_first=None, emit_next_first=None):
        start = pl.multiple_of(j * kb, kb)
        ones = jnp.ones((width, PAIR), BF16)
        s_next = scores(0, j, width) if s_first is None else s_first
        for t in range(ATTN_HEADS_PER_STEP):
            s = s_next
            if t + 1 < ATTN_HEADS_PER_STEP:
                s_next = scores(t + 1, j, width)
            elif emit_next_first is not None:
                emit_next_first()
            if masked:
                rr = lax.broadcasted_iota(jnp.int32, (2 * tq, width), 0) % tq
                cc = lax.broadcasted_iota(jnp.int32, (2 * tq, width), 1)
                s = jnp.where(cc <= rr, s, -1e30)
            row_max = jnp.max(s, axis=1, keepdims=True)
            if masked:
                m_next = jnp.broadcast_to(row_max, (2 * tq, PAIR))
            else:
                m_prev = m_scr[t]
                m_next = jnp.maximum(m_prev, row_max)
            p = jnp.exp2(s - jnp.tile(m_next, (1, width // LANE))).astype(BF16)
            v_ext = jnp.concatenate([v_ref[0, pl.ds(start, width), t * PAIR:(t + 1) * PAIR], ones], axis=1)
            pv = jnp.dot(p, v_ext, preferred_element_type=F32)
            if masked:
                acc_scr[t] = pv
            else:
                acc_scr[t] = jnp.tile(jnp.exp2(m_prev - m_next), (1, 2)) * acc_scr[t] + pv
            m_scr[t] = m_next

    n_big = diag // kv_tiles
    big = kv_tiles * kb

    def big_body(j, carry):
        def emit_next_first():
            j_next = jnp.minimum(j + 1, n_big - 1)
            s0_scr[...] = scores(0, j_next * kv_tiles, big)

        step(j * kv_tiles, big, False, s_first=s0_scr[...], emit_next_first=emit_next_first)
        return carry

    def small_body(j, carry):
        step(j, kb, False)
        return carry

    def emit_first_big():
        s0_scr[...] = scores(0, 0, big)

    step(diag, max(tq, kb), True, emit_next_first=emit_first_big)
    lax.fori_loop(0, n_big, big_body, 0)
    lax.fori_loop(n_big * kv_tiles, diag, small_body, 0)

    lam = (jnp.exp(jnp.sum(lq1_ref[...] * lk1_ref[...], axis=1, keepdims=True))
           - jnp.exp(jnp.sum(lq2_ref[...] * lk2_ref[...], axis=1, keepdims=True)) + LAMBDA_INIT)
    for t in range(ATTN_HEADS_PER_STEP):
        acc = acc_scr[t]
        o = acc[:, :PAIR] / acc[:, PAIR:]
        od = o[:tq] - lam * o[tq:]
        o_ref[0, :tq, t * PAIR:(t + 1) * PAIR] = (
            _rms(od, subw_ref[...], SUBLN_EPS) * (1.0 - LAMBDA_INIT)).astype(o_ref.dtype)
    if o_ref.shape[1] > tq:
        o_ref[0, tq:, :] = jnp.zeros((o_ref.shape[1] - tq, o_ref.shape[2]), o_ref.dtype)


def _diff_attn(qk3, vg3, lams, subw, *, tq, n_tiles, row0, kv_tiles, out_rows=None, prev_out=None):
    bsz, lp, _ = qk3.shape
    width = ATTN_HEADS_PER_STEP * PAIR
    n_groups = D_MODEL // width
    out_rows = tq if out_rows is None else out_rows
    assert row0 % tq == 0 and row0 % out_rows == 0 and row0 % ATTN_KEY_BLOCK == 0
    assert tq % ATTN_KEY_BLOCK == 0 or n_tiles == 1
    q_spec = pl.BlockSpec((1, tq, width), lambda b, g, i: (b, row0 // tq + i, g))
    k_spec = pl.BlockSpec((1, lp, width), lambda b, g, i: (b, 0, n_groups + g))
    v_spec = pl.BlockSpec((1, lp, width), lambda b, g, i: (b, 0, g))
    lam_spec = pl.BlockSpec((1, HEAD_DIM), lambda b, g, i: (0, 0))
    in_specs = [q_spec, k_spec, v_spec, lam_spec, lam_spec, lam_spec, lam_spec,
                pl.BlockSpec((1, PAIR), lambda b, g, i: (0, 0))]
    args = (qk3, qk3, vg3, *lams, subw)
    aliases = {}
    if prev_out is not None:
        aliases = {len(args): 0}
        in_specs.append(pl.BlockSpec(memory_space=pl.ANY))
        args = args + (prev_out,)
    return pl.pallas_call(
        functools.partial(_diff_attn_kernel, tq=tq, kv_tiles=kv_tiles, block0=row0 // ATTN_KEY_BLOCK,
                          aliased=prev_out is not None),
        grid=(bsz, n_groups, n_tiles),
        in_specs=in_specs,
        out_specs=pl.BlockSpec((1, out_rows, width), lambda b, g, i: (b, row0 // out_rows + i, g)),
        out_shape=jax.ShapeDtypeStruct((bsz, lp, D_MODEL), BF16),
        input_output_aliases=aliases,
        scratch_shapes=[pltpu.VMEM((ATTN_HEADS_PER_STEP, 2 * tq, PAIR), F32),
                        pltpu.VMEM((ATTN_HEADS_PER_STEP, 2 * tq, 2 * PAIR), F32),
                        pltpu.VMEM((2 * tq, kv_tiles * ATTN_KEY_BLOCK), F32)],
        compiler_params=_params(("arbitrary", "arbitrary", "arbitrary")),
        name=f"diff_attn_q{tq}",
    )(*args)


def _merge_kernel(y_ref, bonus_ref, g_ref, od_ref, gate_r_ref, gate_d_ref, res_ref, lnw_ref, lnb_ref,
                  ones_ref, wo_r_ref, wo_d_ref, wout_ref, o_ref):
    ones_bd = ones_ref[...]
    parts = []
    for pair in range(0, N_PAIRS, 2):
        y = jnp.concatenate([y_ref[0, pair], y_ref[0, pair + 1]], axis=1).astype(F32)
        mean = _dot(y, ones_bd) * (1.0 / HEAD_DIM)
        yc = y - mean
        var = _dot(yc * yc, ones_bd) * (1.0 / HEAD_DIM)
        parts.append(yc * lax.rsqrt(var + RWKV_LN_EPS))
    yn = jnp.concatenate(parts, axis=1)
    o_rwkv = (yn * lnw_ref[...] + lnb_ref[...] + bonus_ref[0].astype(F32)) * g_ref[0].astype(F32)
    br_rwkv = _dot(o_rwkv, wo_r_ref[...])
    br_diff = jnp.dot(od_ref[0], wo_d_ref[...], preferred_element_type=F32)
    merged = (_sigmoid(gate_r_ref[0].astype(F32)) * br_rwkv
              + _sigmoid(gate_d_ref[0].astype(F32)) * br_diff)
    o_ref[0] = res_ref[0] + _dot(merged, wout_ref[...])


def _merge(y, bonus, g, od, a3, res, lnw, lnb, ones_pair, wo_r, wo_d, wout, *, tm):
    bsz, lp, _ = res.shape
    grid = (bsz, lp // tm)
    tok = lambda c: pl.BlockSpec((1, tm, c), lambda b, i: (b, i, 0))
    vec = pl.BlockSpec((1, D_MODEL), lambda b, i: (0, 0))
    mat = pl.BlockSpec((D_MODEL, D_MODEL), lambda b, i: (0, 0))
    gate_r_spec = pl.BlockSpec((1, tm, D_MODEL), lambda b, i: (b, i, 1))
    gate_d_spec = pl.BlockSpec((1, tm, D_MODEL), lambda b, i: (b, i, 2))
    return pl.pallas_call(
        _merge_kernel,
        grid=grid,
        in_specs=[pl.BlockSpec((1, N_PAIRS, tm, PAIR), lambda b, i: (b, 0, i, 0)),
                  tok(D_MODEL), tok(D_MODEL), tok(D_MODEL), gate_r_spec, gate_d_spec, tok(D_MODEL), vec, vec,
                  pl.BlockSpec((2 * PAIR, 2 * PAIR), lambda b, i: (0, 0)), mat, mat, mat],
        out_specs=tok(D_MODEL),
        out_shape=jax.ShapeDtypeStruct(res.shape, F32),
        compiler_params=_params(("arbitrary", "arbitrary")),
        name="merge",
    )(y, bonus, g, od, a3, a3, res, lnw, lnb, ones_pair, wo_r, wo_d, wout)


def _mlp_kernel(x_ref, nw_ref, w1_ref, w2_ref, fw_ref, o_ref, *, n_ff_chunks):
    x = x_ref[...]
    h = _rms(x, nw_ref[...], NORM_EPS).astype(BF16)
    acc = x
    ff = D_FF // n_ff_chunks
    for c in range(n_ff_chunks):
        z = jnp.dot(h, w1_ref[:, c * ff:(c + 1) * ff], preferred_element_type=F32)
        act = jnp.square(jnp.maximum(z, 0.0))
        acc = acc + _dot(act, w2_ref[c * ff:(c + 1) * ff, :])
    o_ref[...] = _rms(acc, fw_ref[...], NORM_EPS)


def _mlp(x3, nw, w1, w2, fw, *, seq, row0, tm):
    bsz = x3.shape[0]
    x_spec = pl.BlockSpec((None, pl.Element(tm), pl.Element(D_MODEL)), lambda b, i: (b, pl.multiple_of(row0 + i * tm, SUBLANE), 0))
    vec = pl.BlockSpec((1, D_MODEL), lambda b, i: (0, 0))
    return pl.pallas_call(
        functools.partial(_mlp_kernel, n_ff_chunks=4),
        grid=(bsz, seq // tm),
        in_specs=[x_spec, vec, pl.BlockSpec((D_MODEL, D_FF), lambda b, i: (0, 0)),
                  pl.BlockSpec((D_FF, D_MODEL), lambda b, i: (0, 0)), vec],
        out_specs=pl.BlockSpec((None, tm, D_MODEL), lambda b, i: (b, i, 0)),
        out_shape=jax.ShapeDtypeStruct((bsz, seq, D_MODEL), F32),
        compiler_params=_params(("arbitrary", "arbitrary")),
        name="mlp",
    )(x3, nw, w1, w2, fw)


def _pad_cols(a, width):
    return jnp.pad(a, ((0, 0), (0, width - a.shape[1])))


def _pad_rows(a, height):
    return jnp.pad(a, ((0, height - a.shape[0]), (0, 0)))


def _rope_tables(lp):
    inv = ROPE_THETA ** (-jnp.arange(0, ROPE_DIM, 2, dtype=F32) / ROPE_DIM)
    ang = jnp.arange(lp, dtype=F32)[:, None] * inv[None, :]
    cos, sin = jnp.cos(ang), jnp.sin(ang)
    rest = HEAD_DIM - ROPE_DIM
    ones = jnp.ones((lp, rest), F32)
    zeros = jnp.zeros((lp, rest), F32)
    zh = jnp.zeros_like(sin)
    c = jnp.concatenate([cos, cos, ones], axis=1)
    s1 = jnp.concatenate([-sin, zh, zeros], axis=1)
    s2 = jnp.concatenate([zh, sin, zeros], axis=1)
    return tuple(jnp.tile(t, (1, LANE // HEAD_DIM)) for t in (c, s1, s2))


def _pick(n, candidates):
    for c in candidates:
        if n % c == 0:
            return c
    raise ValueError(n)


def kernel(x, meta_tokens, norm_mix_w, w_in, rwkv_mu, rwkv_w0, rwkv_w2, rwkv_a0, rwkv_a2, rwkv_g2, rwkv_k_k, rwkv_k_a, rwkv_r_k, rwkv_ln_w, rwkv_ln_b, rwkv_w_o, diff_lq1, diff_lk1, diff_lq2, diff_lk2, diff_subln_w, diff_w_o, w_out, norm_mlp_w, mlp_w1, mlp_w2, final_norm_w):
    bsz, seq, _ = x.shape
    seqlen = seq + N_META_TOKENS
    lp = -(-seqlen // SEQ_ALIGN) * SEQ_ALIGN
    tm_proj = _pick(lp, (768, 512, 256))
    tm_tok = _pick(lp, (384, 256))

    meta = jnp.broadcast_to(meta_tokens[None].astype(x.dtype), (bsz, N_META_TOKENS, D_MODEL))
    h_res = jnp.concatenate([meta, x, jnp.zeros((bsz, lp - seqlen, D_MODEL), x.dtype)], axis=1)
    hf = h_res.reshape(bsz * lp, D_MODEL)

    w = w_in.reshape(D_MODEL, -1).astype(BF16)
    c3 = 3 * D_MODEL
    w_rwkv = jnp.concatenate(
        [w[:, :c3], _pad_cols(w[:, c3:c3 + DECAY_LORA], LANE),
         _pad_cols(w[:, c3 + DECAY_LORA:c3 + DECAY_LORA + AAA_LORA], LANE),
         _pad_cols(w[:, c3 + DECAY_LORA + AAA_LORA:c3 + DECAY_LORA + AAA_LORA + GATE_LORA], GATE_LORA_PAD)],
        axis=1)
    rwkv_cols = c3 + DECAY_LORA + AAA_LORA + GATE_LORA
    w_diff = w[:, rwkv_cols:]
    mu = rwkv_mu[0][None, :]
    mu_p = jnp.concatenate(
        [mu[:, :c3], _pad_cols(mu[:, c3:c3 + DECAY_LORA], LANE),
         _pad_cols(mu[:, c3 + DECAY_LORA:c3 + DECAY_LORA + AAA_LORA], LANE),
         _pad_cols(mu[:, c3 + DECAY_LORA + AAA_LORA:], GATE_LORA_PAD)], axis=1)
    nw = norm_mix_w[0][None, :]

    tiles_per_seq = lp // tm_proj
    proj = functools.partial(_in_proj, hf, nw, tm=tm_proj, tiles_per_seq=tiles_per_seq)
    p_rwkv = proj(w_rwkv, mode="shift", mu=mu_p)
    qk3 = proj(w_diff[:, :2 * D_MODEL], mode="rope", rope=_rope_tables(lp)).reshape(bsz, lp, 2 * D_MODEL)
    a3 = proj(w_diff[:, 2 * D_MODEL:], mode="plain").reshape(bsz, lp, 3 * D_MODEL)

    head_id = jnp.arange(2 * PAIR) // HEAD_DIM
    ones_bd = (head_id[:, None] == head_id[None, :]).astype(BF16)
    vecs = (rwkv_w0[0][None, :], rwkv_a0[0][None, :], rwkv_k_k[0][None, :], rwkv_k_a[0][None, :],
            rwkv_r_k[0].reshape(1, D_MODEL))
    r, kmod, v, lw, kn, b, bonus, g = _rwkv_prep(
        p_rwkv.reshape(bsz, lp, RWKV_COLS_PAD), vecs,
        _pad_rows(rwkv_w2[0], LANE).astype(BF16), _pad_rows(rwkv_a2[0], LANE).astype(BF16),
        _pad_rows(rwkv_g2[0], GATE_LORA_PAD).astype(BF16), ones_bd, tm=tm_tok)
    y = _rwkv_chunk(r, kmod, v, lw, kn, b)

    lams = (diff_lq1, diff_lk1, diff_lq2, diff_lk2)
    subw = diff_subln_w[0][None, :]
    kv_tiles = min(4, lp // ATTN_KEY_BLOCK)
    o_diff, row0 = None, 0
    for tq in (ATTN_Q_TILE, ATTN_KEY_BLOCK):
        n_tiles = (seqlen - row0) // tq
        if n_tiles:
            o_diff = _diff_attn(qk3, a3, lams, subw, tq=tq, n_tiles=n_tiles, row0=row0, kv_tiles=kv_tiles,
                                prev_out=o_diff)
            row0 += n_tiles * tq
    if row0 < lp:
        tail = max(-(-(seqlen - row0) // ATTN_TAIL_ROWS) * ATTN_TAIL_ROWS, ATTN_TAIL_ROWS)
        o_diff = _diff_attn(qk3, a3, lams, subw, tq=tail, n_tiles=1, row0=row0, kv_tiles=kv_tiles,
                            out_rows=lp - row0, prev_out=o_diff)

    h_mid = _merge(y, bonus, g, o_diff, a3, h_res, rwkv_ln_w[0][None, :], rwkv_ln_b[0][None, :],
                   ones_bd, rwkv_w_o[0].astype(BF16), diff_w_o[0].astype(BF16),
                   w_out[0].astype(BF16), tm=_pick(lp, (768, 384, 256)))
    return _mlp(h_mid, norm_mlp_w[0][None, :], mlp_w1[0].astype(BF16), mlp_w2[0].astype(BF16),
                final_norm_w[None, :], seq=seq, row0=N_META_TOKENS, tm=_pick(seq, (512, 256, 128)))
```

```python
import functools
import math

import jax
import jax.numpy as jnp
from jax import lax
from jax.experimental import pallas as pl
from jax.experimental.pallas import tpu as pltpu

F32 = jnp.float32
BF16 = jnp.bfloat16

D_MODEL = 1024
N_META_TOKENS = 16
HEAD_DIM = 64
PAIR = 2 * HEAD_DIM
N_PAIRS = D_MODEL // PAIR
DECAY_LORA = 64
AAA_LORA = 64
GATE_LORA = 160
RWKV_LN_EPS = 64e-5
ROPE_THETA = 500000.0
ROPE_DIM = HEAD_DIM // 4
ROPE_HALF = ROPE_DIM // 2
D_FF = 4 * D_MODEL
NORM_EPS = 1e-5
SUBLN_EPS = 1e-5
LAMBDA_INIT = 0.8 - 0.6 * math.exp(-0.3 * 0)

LANE = 128
SUBLANE = 8
PREV_ROWS = 16
CHUNK = 64
UNIT_UNROLL = 16
PROJ_COL_CHUNK = 256
ATTN_KEY_BLOCK = 256
ATTN_Q_TILE = 512
ATTN_TAIL_ROWS = 16
ATTN_HEADS_PER_STEP = 4
Q_SCALE = HEAD_DIM ** -0.5 * math.log2(math.e)
SEQ_ALIGN = 256
VMEM_LIMIT = 56 * 1024 * 1024

LORA_W_OFF = 3 * D_MODEL
LORA_A_OFF = LORA_W_OFF + LANE
LORA_G_OFF = LORA_A_OFF + LANE
GATE_LORA_PAD = 2 * LANE
RWKV_COLS_PAD = LORA_G_OFF + GATE_LORA_PAD


def _params(sem):
    return pltpu.CompilerParams(dimension_semantics=sem, vmem_limit_bytes=VMEM_LIMIT)


def _rms(x, w, eps):
    return x * lax.rsqrt(jnp.mean(x * x, axis=-1, keepdims=True) + eps) * w


def _sigmoid(x):
    return 1.0 / (1.0 + jnp.exp(-x))


def _dot(a, b):
    return jnp.dot(a.astype(BF16), b.astype(BF16), preferred_element_type=F32)


def _dot_nt(a, b):
    return lax.dot_general(a.astype(BF16), b.astype(BF16), (((1,), (1,)), ((), ())),
                           preferred_element_type=F32)


def _dot_tn(a, b):
    return lax.dot_general(a.astype(BF16), b.astype(BF16), (((0,), (0,)), ((), ())),
                           preferred_element_type=F32)


def _in_proj_kernel(*refs, mode, tiles_per_seq):
    if mode == "shift":
        x_ref, xprev_ref, nw_ref, w_ref, mu_ref, o_ref = refs
    elif mode == "rope":
        x_ref, nw_ref, w_ref, cos_ref, s1_ref, s2_ref, o_ref = refs
    else:
        x_ref, nw_ref, w_ref, o_ref = refs
    tm, n_cols = o_ref.shape
    cc = PROJ_COL_CHUNK
    xn = _rms(x_ref[...], nw_ref[...], NORM_EPS).astype(BF16)
    if mode == "shift":
        xn = jnp.concatenate([xn, _rms(xprev_ref[...], nw_ref[...], NORM_EPS).astype(BF16)], axis=0)
        first = (pl.program_id(0) % tiles_per_seq) == 0
        row = lax.broadcasted_iota(jnp.int32, (tm, cc), 0)
    elif mode == "rope":
        reps = cc // LANE
        cos = jnp.tile(cos_ref[...], (1, reps))
        s1 = jnp.tile(s1_ref[...], (1, reps))
        s2 = jnp.tile(s2_ref[...], (1, reps))
    for c in range(n_cols // cc):
        cols = slice(c * cc, (c + 1) * cc)
        w = w_ref[:, cols]
        p = jnp.dot(xn, w, preferred_element_type=F32)
        if mode == "shift":
            prev_row = jnp.where(first, 0.0, p[tm + PREV_ROWS - 1:tm + PREV_ROWS, :])
            p = p[:tm]
            shifted = jnp.where(row == 0, prev_row, pltpu.roll(p, 1, axis=0))
            p = p + (shifted - p) * mu_ref[:, cols]
        elif mode == "rope":
            nxt = pltpu.roll(p, cc - ROPE_HALF, axis=1)
            prv = pltpu.roll(p, ROPE_HALF, axis=1)
            scale = Q_SCALE if c * cc < D_MODEL else 1.0
            p = (p * cos + nxt * s1 + prv * s2) * scale
        o_ref[:, cols] = p.astype(o_ref.dtype)


def _in_proj(xf, nw, w, *, mode, mu=None, rope=None, tm, tiles_per_seq):
    t_tokens = xf.shape[0]
    n_cols = w.shape[1]
    x_spec = pl.BlockSpec((tm, D_MODEL), lambda i: (i, 0))
    nw_spec = pl.BlockSpec((1, D_MODEL), lambda i: (0, 0))
    w_spec = pl.BlockSpec((D_MODEL, n_cols), lambda i: (0, 0))
    if mode == "shift":
        blocks = tm // PREV_ROWS
        prev_spec = pl.BlockSpec((PREV_ROWS, D_MODEL), lambda i: (jnp.maximum(i * blocks - 1, 0), 0))
        in_specs = [x_spec, prev_spec, nw_spec, w_spec, pl.BlockSpec((1, n_cols), lambda i: (0, 0))]
        args = (xf, xf, nw, w, mu)
    elif mode == "rope":
        tab_spec = pl.BlockSpec((tm, LANE), lambda i: (i % tiles_per_seq, 0))
        in_specs = [x_spec, nw_spec, w_spec, tab_spec, tab_spec, tab_spec]
        args = (xf, nw, w) + tuple(rope)
    else:
        in_specs = [x_spec, nw_spec, w_spec]
        args = (xf, nw, w)
    return pl.pallas_call(
        functools.partial(_in_proj_kernel, mode=mode, tiles_per_seq=tiles_per_seq),
        grid=(t_tokens // tm,),
        in_specs=in_specs,
        out_specs=pl.BlockSpec((tm, n_cols), lambda i: (i, 0)),
        out_shape=jax.ShapeDtypeStruct((t_tokens, n_cols), BF16),
        compiler_params=_params(("arbitrary",)),
        name="in_proj_" + mode,
    )(*args)


def _rwkv_proj_kernel(x_ref, xprev_ref, nw_ref, w_ref, mu_ref, w0_ref, a0_ref, kk_ref, ka_ref, rk_ref,
                      w2_ref, a2_ref, g2_ref, ones_ref, r_o, k_o, v_o, lw_o, kn_o, b_o, bonus_o, g_o,
                      *, tiles_per_seq):
    tm = x_ref.shape[0]
    blk = 2 * PAIR
    xn = jnp.concatenate([_rms(x_ref[...], nw_ref[...], NORM_EPS).astype(BF16),
                          _rms(xprev_ref[...], nw_ref[...], NORM_EPS).astype(BF16)], axis=0)
    first = (pl.program_id(0) % tiles_per_seq) == 0
    row = lax.broadcasted_iota(jnp.int32, (tm, blk), 0)

    def proj(col0):
        cols = slice(col0, col0 + blk)
        p = jnp.dot(xn, w_ref[:, cols], preferred_element_type=F32)
        prev_row = jnp.where(first, 0.0, p[tm + PREV_ROWS - 1:tm + PREV_ROWS, :])
        p = p[:tm]
        shifted = jnp.where(row == 0, prev_row, pltpu.roll(p, 1, axis=0))
        return p + (shifted - p) * mu_ref[:, cols]

    def rkv(c):
        return proj(c * blk), proj(D_MODEL + c * blk), proj(2 * D_MODEL + c * blk)

    p_wa = proj(LORA_W_OFF)
    tanh_wd = jnp.tanh(p_wa[:, :LANE]).astype(BF16)
    ad = p_wa[:, LANE:].astype(BF16)
    sig_gd = _sigmoid(proj(LORA_G_OFF)).astype(BF16)
    ones_bd = ones_ref[...]
    n_blocks = D_MODEL // blk
    nxt = rkv(0)
    for c in range(n_blocks):
        cols = slice(c * blk, (c + 1) * blk)
        r, k, v = nxt
        if c + 1 < n_blocks:
            nxt = rkv(c + 1)
        z = w0_ref[:, cols] + jnp.dot(tanh_wd, w2_ref[:, cols], preferred_element_type=F32)
        log_decay = -math.exp(-0.5) * _sigmoid(z)
        a_rate = _sigmoid(a0_ref[:, cols] + jnp.dot(ad, a2_ref[:, cols], preferred_element_type=F32))
        gate = jnp.dot(sig_gd, g2_ref[:, cols], preferred_element_type=F32)

        kraw = k * kk_ref[:, cols]
        sumsq = _dot(kraw * kraw, ones_bd)
        knorm = kraw * lax.rsqrt(jnp.maximum(sumsq, 1e-24))
        kmod = k * (1.0 + (a_rate - 1.0) * ka_ref[:, cols])
        bonus = _dot(r * kmod * rk_ref[:, cols], ones_bd) * v

        bonus_o[0, :, cols] = bonus.astype(BF16)
        g_o[0, :, cols] = gate.astype(BF16)
        r16, k16, v16 = r.astype(BF16), kmod.astype(BF16), v.astype(BF16)
        kn16, b16 = knorm.astype(BF16), (knorm * a_rate).astype(BF16)
        for h in range(2):
            pair = 2 * c + h
            sl = slice(h * PAIR, (h + 1) * PAIR)
            r_o[0, pair] = r16[:, sl]
            k_o[0, pair] = k16[:, sl]
            v_o[0, pair] = v16[:, sl]
            lw_o[0, pair] = log_decay[:, sl]
            kn_o[0, pair] = kn16[:, sl]
            b_o[0, pair] = b16[:, sl]


def _rwkv_proj(xf, nw, w, mu, vecs, w2p, a2p, g2p, ones_bd, *, bsz, lp, tm):
    tiles_per_seq = lp // tm
    n_cols = w.shape[1]
    blocks = tm // PREV_ROWS
    const = lambda shape: pl.BlockSpec(shape, lambda i: (0, 0))
    pair_spec = pl.BlockSpec((1, N_PAIRS, tm, PAIR), lambda i: (i // tiles_per_seq, 0, i % tiles_per_seq, 0))
    tok_spec = pl.BlockSpec((1, tm, D_MODEL), lambda i: (i // tiles_per_seq, i % tiles_per_seq, 0))
    pair_shape = jax.ShapeDtypeStruct((bsz, N_PAIRS, lp, PAIR), BF16)
    decay_shape = jax.ShapeDtypeStruct((bsz, N_PAIRS, lp, PAIR), F32)
    tok_shape = jax.ShapeDtypeStruct((bsz, lp, D_MODEL), BF16)
    return pl.pallas_call(
        functools.partial(_rwkv_proj_kernel, tiles_per_seq=tiles_per_seq),
        grid=(bsz * tiles_per_seq,),
        in_specs=[pl.BlockSpec((tm, D_MODEL), lambda i: (i, 0)),
                  pl.BlockSpec((PREV_ROWS, D_MODEL), lambda i: (jnp.maximum(i * blocks - 1, 0), 0)),
                  const((1, D_MODEL)),
                  pl.BlockSpec((D_MODEL, n_cols), lambda i: (0, 0), pipeline_mode=pl.Buffered(1)),
                  const((1, n_cols))]
        + [const((1, D_MODEL))] * 5
        + [const((LANE, D_MODEL)), const((LANE, D_MODEL)), const((GATE_LORA_PAD, D_MODEL)),
           const((2 * PAIR, 2 * PAIR))],
        out_specs=[pair_spec] * 6 + [tok_spec] * 2,
        out_shape=[pair_shape] * 3 + [decay_shape] + [pair_shape] * 2 + [tok_shape] * 2,
        compiler_params=_params(("arbitrary",)),
        name="rwkv_proj",
    )(xf, xf, nw, w, mu, *vecs, w2p, a2p, g2p, ones_bd)


def _block_diag(x, lane_head0):
    return jnp.concatenate([jnp.where(lane_head0, x, 0.0), jnp.where(lane_head0, 0.0, x)], axis=0)


def _chunk_unit(r, k, v, lw, kn, b, state):
    n = 2 * CHUNK
    row = lax.broadcasted_iota(jnp.int32, (CHUNK, PAIR), 0)
    lane_head0 = lax.broadcasted_iota(jnp.int32, (CHUNK, PAIR), 1) < HEAD_DIM
    c = lw
    for s in (1, 2, 4, 8, 16, 32):
        c = c + jnp.where(row >= s, pltpu.roll(c, s, axis=0), 0.0)
    c_end = c[CHUNK - 1:CHUNK, :]
    e_neg = jnp.exp(-c)
    e_end = jnp.exp(c_end - c)
    a_t = _block_diag(-kn * jnp.exp(c - lw), lane_head0)
    r_t = _block_diag(r * jnp.exp(c), lane_head0)
    b_t = _block_diag(b * e_neg, lane_head0)
    k_t = _block_diag(k * e_neg, lane_head0)
    b_e = _block_diag(b * e_end, lane_head0)
    k_e = _block_diag(k * e_end, lane_head0)
    v_bd = _block_diag(v, lane_head0)

    yield
    g = _dot_nt(jnp.concatenate([a_t, r_t], axis=0), jnp.concatenate([b_t, k_t], axis=0))
    ri = lax.broadcasted_iota(jnp.int32, (n, n), 0)
    ci = lax.broadcasted_iota(jnp.int32, (n, n), 1)
    m_ab = jnp.where(ci < ri, g[:n, :n], 0.0)
    m_ak = jnp.where(ci < ri, g[:n, n:], 0.0)
    a_rb = jnp.where(ci <= ri, g[n:, :n], 0.0)
    a_rk = jnp.where(ci <= ri, g[n:, n:], 0.0)

    t_inv = jnp.where(ci == ri, 1.0, m_ab)
    yield
    power = _dot(m_ab, m_ab)
    w1 = _dot(m_ak, v_bd)
    for _ in range(4):
        yield
        prod = _dot(power, jnp.concatenate([power, t_inv], axis=1))
        power = prod[:, :n]
        t_inv = t_inv + prod[:, n:]
    yield
    t_inv = t_inv + _dot(power, t_inv)

    yield
    tw = _dot(t_inv, jnp.concatenate([w1, a_t], axis=1))
    u0 = tw[:, :n]
    a_hat = tw[:, n:]
    rhs = jnp.concatenate([jnp.concatenate([a_hat, u0], axis=1),
                           jnp.concatenate([jnp.zeros_like(v_bd), v_bd], axis=1)], axis=0)
    yield
    pq = _dot_tn(jnp.concatenate([b_e, k_e], axis=0), rhs)
    ry = _dot(jnp.concatenate([a_rb, a_rk], axis=1), rhs)
    decay_end = jnp.exp(jnp.broadcast_to(c_end, (n, n)))
    p_mat = pq[:, :n] + jnp.where(ci == ri, decay_end, 0.0)
    yield
    ys = _dot(jnp.concatenate([r_t + ry[:, :n], p_mat], axis=0), state)
    y_bd = ys[:n] + ry[:, n:]
    y = y_bd[:CHUNK] + y_bd[CHUNK:]
    new_state = ys[n:] + pq[:, n:]
    return y, new_state


def _interleave(generators):
    results = [None] * len(generators)
    pending = list(range(len(generators)))
    while pending:
        still = []
        for idx in pending:
            try:
                next(generators[idx])
                still.append(idx)
            except StopIteration as stop:
                results[idx] = stop.value
        pending = still
    return results


def _rwkv_chunk_kernel(r_ref, k_ref, v_ref, lw_ref, kn_ref, b_ref, y_ref, state_scr, *, n_units):
    @pl.when(pl.program_id(0) == 0)
    def _():
        state_scr[...] = jnp.zeros_like(state_scr)

    def body(g, carry):
        units = [g * UNIT_UNROLL + j for j in range(UNIT_UNROLL)]
        idx = [(u // N_PAIRS, u % N_PAIRS) for u in units]
        args = [(r_ref[bi, pi].astype(F32), k_ref[bi, pi].astype(F32), v_ref[bi, pi].astype(F32),
                 lw_ref[bi, pi], kn_ref[bi, pi].astype(F32), b_ref[bi, pi].astype(F32), state_scr[u])
                for u, (bi, pi) in zip(units, idx)]
        outs = _interleave([_chunk_unit(*a) for a in args])
        for u, (bi, pi), (y, new_state) in zip(units, idx, outs):
            y_ref[bi, pi] = y.astype(BF16)
            state_scr[u] = new_state
        return carry

    lax.fori_loop(0, n_units // UNIT_UNROLL, body, 0)


def _rwkv_chunk(r, k, v, lw, kn, b):
    bsz, _, lp, _ = r.shape
    n_units = bsz * N_PAIRS
    spec = pl.BlockSpec((bsz, N_PAIRS, CHUNK, PAIR), lambda c: (0, 0, c, 0))
    return pl.pallas_call(
        functools.partial(_rwkv_chunk_kernel, n_units=n_units),
        grid=(lp // CHUNK,),
        in_specs=[spec] * 6,
        out_specs=spec,
        out_shape=jax.ShapeDtypeStruct(r.shape, BF16),
        scratch_shapes=[pltpu.VMEM((n_units, PAIR, PAIR), F32)],
        compiler_params=_params(("arbitrary",)),
        name="rwkv_chunk",
    )(r, k, v, lw, kn, b)


def _diff_attn_kernel(q_ref, k_ref, v_ref, lq1_ref, lk1_ref, lq2_ref, lk2_ref, subw_ref,
                      *rest, tq, kv_tiles, block0, aliased):
    o_ref, m_scr, acc_scr, s0_scr = rest[1:] if aliased else rest
    kb = ATTN_KEY_BLOCK
    diag = block0 + (pl.program_id(2) * tq) // kb
    head0 = lax.broadcasted_iota(jnp.int32, (tq, PAIR), 1) < HEAD_DIM
    qq = []
    for t in range(ATTN_HEADS_PER_STEP):
        q = q_ref[0, :, t * PAIR:(t + 1) * PAIR]
        zero = jnp.zeros_like(q)
        qq.append(jnp.concatenate([jnp.where(head0, q, zero), jnp.where(head0, zero, q)], axis=0))

    def scores(t, j, width):
        start = pl.multiple_of(j * kb, kb)
        return _dot_nt(qq[t], k_ref[0, pl.ds(start, width), t * PAIR:(t + 1) * PAIR])

    def step(j, width, masked, s_first=None, emit_next_first=None):
        start = pl.multiple_of(j * kb, kb)
        ones = jnp.ones((width, PAIR), BF16)
        s_next = scores(0, j, width) if s_first is None else s_first
        for t in range(ATTN_HEADS_PER_STEP):
            s = s_next
            if t + 1 < ATTN_HEADS_PER_STEP:
                s_next = scores(t + 1, j, width)
            elif emit_next_first is not None:
                emit_next_first()
            if masked:
                rr = lax.broadcasted_iota(jnp.int32, (2 * tq, width), 0) % tq
                cc = lax.broadcasted_iota(jnp.int32, (2 * tq, width), 1)
                s = jnp.where(cc <= rr, s, -1e30)
            row_max = jnp.max(s, axis=1, keepdims=True)
            if masked:
                m_next = jnp.broadcast_to(row_max, (2 * tq, PAIR))
            else:
                m_prev = m_scr[t]
                m_next = jnp.maximum(m_prev, row_max)
            p = jnp.exp2(s - jnp.tile(m_next, (1, width // LANE))).astype(BF16)
            v_ext = jnp.concatenate([v_ref[0, pl.ds(start, width), t * PAIR:(t + 1) * PAIR], ones], axis=1)
            pv = jnp.dot(p, v_ext, preferred_element_type=F32)
            if masked:
                acc_scr[t] = pv
            else:
                acc_scr[t] = jnp.tile(jnp.exp2(m_prev - m_next), (1, 2)) * acc_scr[t] + pv
            m_scr[t] = m_next

    n_big = diag // kv_tiles
    big = kv_tiles * kb

    def big_body(j, carry):
        def emit_next_first():
            j_next = jnp.minimum(j + 1, n_big - 1)
            s0_scr[...] = scores(0, j_next * kv_tiles, big)

        step(j * kv_tiles, big, False, s_first=s0_scr[...], emit_next_first=emit_next_first)
        return carry

    def small_body(j, carry):
        step(j, kb, False)
        return carry

    def emit_first_big():
        s0_scr[...] = scores(0, 0, big)

    step(diag, max(tq, kb), True, emit_next_first=emit_first_big)
    lax.fori_loop(0, n_big, big_body, 0)
    lax.fori_loop(n_big * kv_tiles, diag, small_body, 0)

    lam = (jnp.exp(jnp.sum(lq1_ref[...] * lk1_ref[...], axis=1, keepdims=True))
           - jnp.exp(jnp.sum(lq2_ref[...] * lk2_ref[...], axis=1, keepdims=True)) + LAMBDA_INIT)
    for t in range(ATTN_HEADS_PER_STEP):
        acc = acc_scr[t]
        o = acc[:, :PAIR] / acc[:, PAIR:]
        od = o[:tq] - lam * o[tq:]
        o_ref[0, :tq, t * PAIR:(t + 1) * PAIR] = (
            _rms(od, subw_ref[...], SUBLN_EPS) * (1.0 - LAMBDA_INIT)).astype(o_ref.dtype)
    if o_ref.shape[1] > tq:
        o_ref[0, tq:, :] = jnp.zeros((o_ref.shape[1] - tq, o_ref.shape[2]), o_ref.dtype)


def _diff_attn(qk3, vg3, lams, subw, *, tq, n_tiles, row0, kv_tiles, out_rows=None, prev_out=None):
    bsz, lp, _ = qk3.shape
    width = ATTN_HEADS_PER_STEP * PAIR
    n_groups = D_MODEL // width
    out_rows = tq if out_rows is None else out_rows
    assert row0 % tq == 0 and row0 % out_rows == 0 and row0 % ATTN_KEY_BLOCK == 0
    assert tq % ATTN_KEY_BLOCK == 0 or n_tiles == 1
    q_spec = pl.BlockSpec((1, tq, width), lambda b, g, i: (b, row0 // tq + i, g))
    k_spec = pl.BlockSpec((1, lp, width), lambda b, g, i: (b, 0, n_groups + g))
    v_spec = pl.BlockSpec((1, lp, width), lambda b, g, i: (b, 0, g))
    lam_spec = pl.BlockSpec((1, HEAD_DIM), lambda b, g, i: (0, 0))
    in_specs = [q_spec, k_spec, v_spec, lam_spec, lam_spec, lam_spec, lam_spec,
                pl.BlockSpec((1, PAIR), lambda b, g, i: (0, 0))]
    args = (qk3, qk3, vg3, *lams, subw)
    aliases = {}
    if prev_out is not None:
        aliases = {len(args): 0}
        in_specs.append(pl.BlockSpec(memory_space=pl.ANY))
        args = args + (prev_out,)
    return pl.pallas_call(
        functools.partial(_diff_attn_kernel, tq=tq, kv_tiles=kv_tiles, block0=row0 // ATTN_KEY_BLOCK,
                          aliased=prev_out is not None),
        grid=(bsz, n_groups, n_tiles),
        in_specs=in_specs,
        out_specs=pl.BlockSpec((1, out_rows, width), lambda b, g, i: (b, row0 // out_rows + i, g)),
        out_shape=jax.ShapeDtypeStruct((bsz, lp, D_MODEL), BF16),
        input_output_aliases=aliases,
        scratch_shapes=[pltpu.VMEM((ATTN_HEADS_PER_STEP, 2 * tq, PAIR), F32),
                        pltpu.VMEM((ATTN_HEADS_PER_STEP, 2 * tq, 2 * PAIR), F32),
                        pltpu.VMEM((2 * tq, kv_tiles * ATTN_KEY_BLOCK), F32)],
        compiler_params=_params(("arbitrary", "arbitrary", "arbitrary")),
        name=f"diff_attn_q{tq}",
    )(*args)


def _merge_kernel(y_ref, bonus_ref, g_ref, od_ref, gate_r_ref, gate_d_ref, res_ref, lnw_ref, lnb_ref,
                  ones_ref, wo_r_ref, wo_d_ref, wout_ref, o_ref):
    ones_bd = ones_ref[...]
    parts = []
    for pair in range(0, N_PAIRS, 2):
        y = jnp.concatenate([y_ref[0, pair], y_ref[0, pair + 1]], axis=1).astype(F32)
        mean = _dot(y, ones_bd) * (1.0 / HEAD_DIM)
        yc = y - mean
        var = _dot(yc * yc, ones_bd) * (1.0 / HEAD_DIM)
        parts.append(yc * lax.rsqrt(var + RWKV_LN_EPS))
    yn = jnp.concatenate(parts, axis=1)
    o_rwkv = (yn * lnw_ref[...] + lnb_ref[...] + bonus_ref[0].astype(F32)) * g_ref[0].astype(F32)
    br_rwkv = _dot(o_rwkv, wo_r_ref[...])
    br_diff = jnp.dot(od_ref[0], wo_d_ref[...], preferred_element_type=F32)
    merged = (_sigmoid(gate_r_ref[0].astype(F32)) * br_rwkv
              + _sigmoid(gate_d_ref[0].astype(F32)) * br_diff)
    o_ref[0] = res_ref[0] + _dot(merged, wout_ref[...])


def _merge(y, bonus, g, od, a3, res, lnw, lnb, ones_pair, wo_r, wo_d, wout, *, tm):
    bsz, lp, _ = res.shape
    grid = (bsz, lp // tm)
    tok = lambda c: pl.BlockSpec((1, tm, c), lambda b, i: (b, i, 0))
    vec = pl.BlockSpec((1, D_MODEL), lambda b, i: (0, 0))
    mat = pl.BlockSpec((D_MODEL, D_MODEL), lambda b, i: (0, 0))
    gate_r_spec = pl.BlockSpec((1, tm, D_MODEL), lambda b, i: (b, i, 1))
    gate_d_spec = pl.BlockSpec((1, tm, D_MODEL), lambda b, i: (b, i, 2))
    return pl.pallas_call(
        _merge_kernel,
        grid=grid,
        in_specs=[pl.BlockSpec((1, N_PAIRS, tm, PAIR), lambda b, i: (b, 0, i, 0)),
                  tok(D_MODEL), tok(D_MODEL), tok(D_MODEL), gate_r_spec, gate_d_spec, tok(D_MODEL), vec, vec,
                  pl.BlockSpec((2 * PAIR, 2 * PAIR), lambda b, i: (0, 0)), mat, mat, mat],
        out_specs=tok(D_MODEL),
        out_shape=jax.ShapeDtypeStruct(res.shape, F32),
        compiler_params=_params(("arbitrary", "arbitrary")),
        name="merge",
    )(y, bonus, g, od, a3, a3, res, lnw, lnb, ones_pair, wo_r, wo_d, wout)


def _mlp_kernel(x_ref, nw_ref, w1_ref, w2_ref, fw_ref, o_ref, *, n_ff_chunks):
    x = x_ref[...]
    h = _rms(x, nw_ref[...], NORM_EPS).astype(BF16)
    acc = x
    ff = D_FF // n_ff_chunks
    for c in range(n_ff_chunks):
        z = jnp.dot(h, w1_ref[:, c * ff:(c + 1) * ff], preferred_element_type=F32)
        act = jnp.square(jnp.maximum(z, 0.0))
        acc = acc + _dot(act, w2_ref[c * ff:(c + 1) * ff, :])
    o_ref[...] = _rms(acc, fw_ref[...], NORM_EPS)


def _mlp(x3, nw, w1, w2, fw, *, seq, row0, tm):
    bsz = x3.shape[0]
    x_spec = pl.BlockSpec((None, pl.Element(tm), pl.Element(D_MODEL)), lambda b, i: (b, pl.multiple_of(row0 + i * tm, SUBLANE), 0))
    vec = pl.BlockSpec((1, D_MODEL), lambda b, i: (0, 0))
    return pl.pallas_call(
        functools.partial(_mlp_kernel, n_ff_chunks=4),
        grid=(bsz, seq // tm),
        in_specs=[x_spec, vec, pl.BlockSpec((D_MODEL, D_FF), lambda b, i: (0, 0)),
                  pl.BlockSpec((D_FF, D_MODEL), lambda b, i: (0, 0)), vec],
        out_specs=pl.BlockSpec((None, tm, D_MODEL), lambda b, i: (b, i, 0)),
        out_shape=jax.ShapeDtypeStruct((bsz, seq, D_MODEL), F32),
        compiler_params=_params(("arbitrary", "arbitrary")),
        name="mlp",
    )(x3, nw, w1, w2, fw)


def _pad_cols(a, width):
    return jnp.pad(a, ((0, 0), (0, width - a.shape[1])))


def _pad_rows(a, height):
    return jnp.pad(a, ((0, height - a.shape[0]), (0, 0)))


def _rope_tables(lp):
    inv = ROPE_THETA ** (-jnp.arange(0, ROPE_DIM, 2, dtype=F32) / ROPE_DIM)
    ang = jnp.arange(lp, dtype=F32)[:, None] * inv[None, :]
    cos, sin = jnp.cos(ang), jnp.sin(ang)
    rest = HEAD_DIM - ROPE_DIM
    ones = jnp.ones((lp, rest), F32)
    zeros = jnp.zeros((lp, rest), F32)
    zh = jnp.zeros_like(sin)
    c = jnp.concatenate([cos, cos, ones], axis=1)
    s1 = jnp.concatenate([-sin, zh, zeros], axis=1)
    s2 = jnp.concatenate([zh, sin, zeros], axis=1)
    return tuple(jnp.tile(t, (1, LANE // HEAD_DIM)) for t in (c, s1, s2))


def _pick(n, candidates):
    for c in candidates:
        if n % c == 0:
            return c
    raise ValueError(n)


def kernel(x, meta_tokens, norm_mix_w, w_in, rwkv_mu, rwkv_w0, rwkv_w2, rwkv_a0, rwkv_a2, rwkv_g2, rwkv_k_k, rwkv_k_a, rwkv_r_k, rwkv_ln_w, rwkv_ln_b, rwkv_w_o, diff_lq1, diff_lk1, diff_lq2, diff_lk2, diff_subln_w, diff_w_o, w_out, norm_mlp_w, mlp_w1, mlp_w2, final_norm_w):
    bsz, seq, _ = x.shape
    seqlen = seq + N_META_TOKENS
    lp = -(-seqlen // SEQ_ALIGN) * SEQ_ALIGN
    tm_proj = _pick(lp, (768, 512, 256))
    tm_tok = _pick(lp, (384, 256))

    meta = jnp.broadcast_to(meta_tokens[None].astype(x.dtype), (bsz, N_META_TOKENS, D_MODEL))
    h_res = jnp.concatenate([meta, x, jnp.zeros((bsz, lp - seqlen, D_MODEL), x.dtype)], axis=1)
    hf = h_res.reshape(bsz * lp, D_MODEL)

    w = w_in.reshape(D_MODEL, -1).astype(BF16)
    c3 = 3 * D_MODEL
    w_rwkv = jnp.concatenate(
        [w[:, :c3], _pad_cols(w[:, c3:c3 + DECAY_LORA], LANE),
         _pad_cols(w[:, c3 + DECAY_LORA:c3 + DECAY_LORA + AAA_LORA], LANE),
         _pad_cols(w[:, c3 + DECAY_LORA + AAA_LORA:c3 + DECAY_LORA + AAA_LORA + GATE_LORA], GATE_LORA_PAD)],
        axis=1)
    rwkv_cols = c3 + DECAY_LORA + AAA_LORA + GATE_LORA
    w_diff = w[:, rwkv_cols:]
    mu = rwkv_mu[0][None, :]
    mu_p = jnp.concatenate(
        [mu[:, :c3], _pad_cols(mu[:, c3:c3 + DECAY_LORA], LANE),
         _pad_cols(mu[:, c3 + DECAY_LORA:c3 + DECAY_LORA + AAA_LORA], LANE),
         _pad_cols(mu[:, c3 + DECAY_LORA + AAA_LORA:], GATE_LORA_PAD)], axis=1)
    nw = norm_mix_w[0][None, :]

    tiles_per_seq = lp // tm_proj
    proj = functools.partial(_in_proj, hf, nw, tm=tm_proj, tiles_per_seq=tiles_per_seq)
    qk3 = proj(w_diff[:, :2 * D_MODEL], mode="rope", rope=_rope_tables(lp)).reshape(bsz, lp, 2 * D_MODEL)
    a3 = proj(w_diff[:, 2 * D_MODEL:], mode="plain").reshape(bsz, lp, 3 * D_MODEL)

    head_id = jnp.arange(2 * PAIR) // HEAD_DIM
    ones_bd = (head_id[:, None] == head_id[None, :]).astype(BF16)
    vecs = (rwkv_w0[0][None, :], rwkv_a0[0][None, :], rwkv_k_k[0][None, :], rwkv_k_a[0][None, :],
            rwkv_r_k[0].reshape(1, D_MODEL))
    r, kmod, v, lw, kn, b, bonus, g = _rwkv_proj(
        hf, nw, w_rwkv, mu_p, vecs, _pad_rows(rwkv_w2[0], LANE).astype(BF16),
        _pad_rows(rwkv_a2[0], LANE).astype(BF16), _pad_rows(rwkv_g2[0], GATE_LORA_PAD).astype(BF16),
        ones_bd, bsz=bsz, lp=lp, tm=tm_proj)
    y = _rwkv_chunk(r, kmod, v, lw, kn, b)

    lams = (diff_lq1, diff_lk1, diff_lq2, diff_lk2)
    subw = diff_subln_w[0][None, :]
    kv_tiles = min(4, lp // ATTN_KEY_BLOCK)
    o_diff, row0 = None, 0
    for tq in (ATTN_Q_TILE, ATTN_KEY_BLOCK):
        n_tiles = (seqlen - row0) // tq
        if n_tiles:
            o_diff = _diff_attn(qk3, a3, lams, subw, tq=tq, n_tiles=n_tiles, row0=row0, kv_tiles=kv_tiles,
                                prev_out=o_diff)
            row0 += n_tiles * tq
    if row0 < lp:
        tail = max(-(-(seqlen - row0) // ATTN_TAIL_ROWS) * ATTN_TAIL_ROWS, ATTN_TAIL_ROWS)
        o_diff = _diff_attn(qk3, a3, lams, subw, tq=tail, n_tiles=1, row0=row0, kv_tiles=kv_tiles,
                            out_rows=lp - row0, prev_out=o_diff)

    h_mid = _merge(y, bonus, g, o_diff, a3, h_res, rwkv_ln_w[0][None, :], rwkv_ln_b[0][None, :],
                   ones_bd, rwkv_w_o[0].astype(BF16), diff_w_o[0].astype(BF16),
                   w_out[0].astype(BF16), tm=_pick(lp, (768, 384, 256)))
    return _mlp(h_mid, norm_mlp_w[0][None, :], mlp_w1[0].astype(BF16), mlp_w2[0].astype(BF16),
                final_norm_w[None, :], seq=seq, row0=N_META_TOKENS, tm=_pick(seq, (512, 256, 128)))
```

```python
import functools
import math

import jax
import jax.numpy as jnp
from jax import lax
from jax.experimental import pallas as pl
from jax.experimental.pallas import tpu as pltpu

F32 = jnp.float32
BF16 = jnp.bfloat16

D_MODEL = 1024
N_META_TOKENS = 16
HEAD_DIM = 64
PAIR = 2 * HEAD_DIM
N_PAIRS = D_MODEL // PAIR
DECAY_LORA = 64
AAA_LORA = 64
GATE_LORA = 160
RWKV_LN_EPS = 64e-5
ROPE_THETA = 500000.0
ROPE_DIM = HEAD_DIM // 4
ROPE_HALF = ROPE_DIM // 2
D_FF = 4 * D_MODEL
NORM_EPS = 1e-5
SUBLN_EPS = 1e-5
LAMBDA_INIT = 0.8 - 0.6 * math.exp(-0.3 * 0)

LANE = 128
SUBLANE = 8
PREV_ROWS = 16
CHUNK = 64
CHUNKS_PER_STEP = 2
PROJ_COL_CHUNK = 256
ATTN_KEY_BLOCK = 256
ATTN_Q_TILE = 512
ATTN_TAIL_ROWS = 16
ATTN_HEADS_PER_STEP = 4
Q_SCALE = HEAD_DIM ** -0.5 * math.log2(math.e)
SEQ_ALIGN = 256
VMEM_LIMIT = 56 * 1024 * 1024

LORA_W_OFF = 3 * D_MODEL
LORA_A_OFF = LORA_W_OFF + LANE
LORA_G_OFF = LORA_A_OFF + LANE
GATE_LORA_PAD = 2 * LANE
RWKV_COLS_PAD = LORA_G_OFF + GATE_LORA_PAD


def _params(sem):
    return pltpu.CompilerParams(dimension_semantics=sem, vmem_limit_bytes=VMEM_LIMIT)


def _rms(x, w, eps):
    return x * lax.rsqrt(jnp.mean(x * x, axis=-1, keepdims=True) + eps) * w


def _sigmoid(x):
    return 1.0 / (1.0 + jnp.exp(-x))


def _dot(a, b):
    return jnp.dot(a.astype(BF16), b.astype(BF16), preferred_element_type=F32)


def _dot_nt(a, b):
    return lax.dot_general(a.astype(BF16), b.astype(BF16), (((1,), (1,)), ((), ())),
                           preferred_element_type=F32)


def _dot_tn(a, b):
    return lax.dot_general(a.astype(BF16), b.astype(BF16), (((0,), (0,)), ((), ())),
                           preferred_element_type=F32)


def _in_proj_kernel(xn_ref, w_ref, cos_ref, s1_ref, s2_ref, o_ref):
    n_cols = o_ref.shape[1]
    cc = PROJ_COL_CHUNK
    xn = xn_ref[...]
    reps = cc // LANE
    cos = jnp.tile(cos_ref[...], (1, reps))
    s1 = jnp.tile(s1_ref[...], (1, reps))
    s2 = jnp.tile(s2_ref[...], (1, reps))
    for c in range(n_cols // cc):
        cols = slice(c * cc, (c + 1) * cc)
        p = jnp.dot(xn, w_ref[:, cols], preferred_element_type=F32)
        if c * cc < 2 * D_MODEL:
            nxt = pltpu.roll(p, cc - ROPE_HALF, axis=1)
            prv = pltpu.roll(p, ROPE_HALF, axis=1)
            scale = Q_SCALE if c * cc < D_MODEL else 1.0
            p = (p * cos + nxt * s1 + prv * s2) * scale
        o_ref[:, cols] = p.astype(o_ref.dtype)


def _in_proj(xn, w, rope, *, tm, tiles_per_seq):
    t_tokens = xn.shape[0]
    n_cols = w.shape[1]
    tab_spec = pl.BlockSpec((tm, LANE), lambda i: (i % tiles_per_seq, 0))
    return pl.pallas_call(
        _in_proj_kernel,
        grid=(t_tokens // tm,),
        in_specs=[pl.BlockSpec((tm, D_MODEL), lambda i: (i, 0)),
                  pl.BlockSpec((D_MODEL, n_cols), lambda i: (0, 0), pipeline_mode=pl.Buffered(1)),
                  tab_spec, tab_spec, tab_spec],
        out_specs=pl.BlockSpec((tm, n_cols), lambda i: (i, 0)),
        out_shape=jax.ShapeDtypeStruct((t_tokens, n_cols), BF16),
        compiler_params=_params(("arbitrary",)),
        name="in_proj_diff",
    )(xn, w, *rope)


def _rwkv_proj_kernel(x_ref, xprev_ref, nw_ref, w_ref, mu_ref, w0_ref, a0_ref, kk_ref, ka_ref, rk_ref,
                      w2_ref, a2_ref, g2_ref, ones_ref, r_o, k_o, v_o, lw_o, kn_o, b_o, bonus_o, g_o, xn_o,
                      *, tiles_per_seq):
    tm = x_ref.shape[0]
    blk = 2 * PAIR
    xn = jnp.concatenate([_rms(x_ref[...], nw_ref[...], NORM_EPS).astype(BF16),
                          _rms(xprev_ref[...], nw_ref[...], NORM_EPS).astype(BF16)], axis=0)
    first = (pl.program_id(0) % tiles_per_seq) == 0
    row = lax.broadcasted_iota(jnp.int32, (tm, blk), 0)
    xn_o[...] = xn[:tm]

    def proj(col0):
        cols = slice(col0, col0 + blk)
        p = jnp.dot(xn, w_ref[:, cols], preferred_element_type=F32)
        prev_row = jnp.where(first, 0.0, p[tm + PREV_ROWS - 1:tm + PREV_ROWS, :])
        p = p[:tm]
        shifted = jnp.where(row == 0, prev_row, pltpu.roll(p, 1, axis=0))
        return p + (shifted - p) * mu_ref[:, cols]

    def rkv(c):
        return proj(c * blk), proj(D_MODEL + c * blk), proj(2 * D_MODEL + c * blk)

    p_wa = proj(LORA_W_OFF)
    tanh_wd = jnp.tanh(p_wa[:, :LANE]).astype(BF16)
    ad = p_wa[:, LANE:].astype(BF16)
    sig_gd = _sigmoid(proj(LORA_G_OFF)).astype(BF16)
    ones_bd = ones_ref[...]
    n_blocks = D_MODEL // blk
    nxt = rkv(0)
    for c in range(n_blocks):
        cols = slice(c * blk, (c + 1) * blk)
        r, k, v = nxt
        if c + 1 < n_blocks:
            nxt = rkv(c + 1)
        z = w0_ref[:, cols] + jnp.dot(tanh_wd, w2_ref[:, cols], preferred_element_type=F32)
        log_decay = -math.exp(-0.5) * _sigmoid(z)
        a_rate = _sigmoid(a0_ref[:, cols] + jnp.dot(ad, a2_ref[:, cols], preferred_element_type=F32))
        gate = jnp.dot(sig_gd, g2_ref[:, cols], preferred_element_type=F32)

        kraw = k * kk_ref[:, cols]
        sumsq = _dot(kraw * kraw, ones_bd)
        knorm = kraw * lax.rsqrt(jnp.maximum(sumsq, 1e-24))
        kmod = k * (1.0 + (a_rate - 1.0) * ka_ref[:, cols])
        bonus = _dot(r * kmod * rk_ref[:, cols], ones_bd) * v

        bonus_o[0, :, cols] = bonus.astype(BF16)
        g_o[0, :, cols] = gate.astype(BF16)
        r16, k16, v16 = r.astype(BF16), kmod.astype(BF16), v.astype(BF16)
        kn16, b16 = knorm.astype(BF16), (knorm * a_rate).astype(BF16)
        for h in range(2):
            pair = 2 * c + h
            sl = slice(h * PAIR, (h + 1) * PAIR)
            r_o[0, pair] = r16[:, sl]
            k_o[0, pair] = k16[:, sl]
            v_o[0, pair] = v16[:, sl]
            lw_o[0, pair] = log_decay[:, sl]
            kn_o[0, pair] = kn16[:, sl]
            b_o[0, pair] = b16[:, sl]


def _rwkv_proj(xf, nw, w, mu, vecs, w2p, a2p, g2p, ones_bd, *, bsz, lp, tm):
    tiles_per_seq = lp // tm
    n_cols = w.shape[1]
    blocks = tm // PREV_ROWS
    const = lambda shape: pl.BlockSpec(shape, lambda i: (0, 0))
    pair_spec = pl.BlockSpec((1, N_PAIRS, tm, PAIR), lambda i: (i // tiles_per_seq, 0, i % tiles_per_seq, 0))
    tok_spec = pl.BlockSpec((1, tm, D_MODEL), lambda i: (i // tiles_per_seq, i % tiles_per_seq, 0))
    pair_shape = jax.ShapeDtypeStruct((bsz, N_PAIRS, lp, PAIR), BF16)
    decay_shape = jax.ShapeDtypeStruct((bsz, N_PAIRS, lp, PAIR), F32)
    tok_shape = jax.ShapeDtypeStruct((bsz, lp, D_MODEL), BF16)
    return pl.pallas_call(
        functools.partial(_rwkv_proj_kernel, tiles_per_seq=tiles_per_seq),
        grid=(bsz * tiles_per_seq,),
        in_specs=[pl.BlockSpec((tm, D_MODEL), lambda i: (i, 0)),
                  pl.BlockSpec((PREV_ROWS, D_MODEL), lambda i: (jnp.maximum(i * blocks - 1, 0), 0)),
                  const((1, D_MODEL)),
                  pl.BlockSpec((D_MODEL, n_cols), lambda i: (0, 0), pipeline_mode=pl.Buffered(1)),
                  const((1, n_cols))]
        + [const((1, D_MODEL))] * 5
        + [const((LANE, D_MODEL)), const((LANE, D_MODEL)), const((GATE_LORA_PAD, D_MODEL)),
           const((2 * PAIR, 2 * PAIR))],
        out_specs=[pair_spec] * 6 + [tok_spec] * 2 + [pl.BlockSpec((tm, D_MODEL), lambda i: (i, 0))],
        out_shape=[pair_shape] * 3 + [decay_shape] + [pair_shape] * 2 + [tok_shape] * 2
        + [jax.ShapeDtypeStruct((bsz * lp, D_MODEL), BF16)],
        compiler_params=_params(("arbitrary",)),
        name="rwkv_proj",
    )(xf, xf, nw, w, mu, *vecs, w2p, a2p, g2p, ones_bd)


def _block_diag(x, lane_head0):
    return jnp.concatenate([jnp.where(lane_head0, x, 0.0), jnp.where(lane_head0, 0.0, x)], axis=0)


def _chunk_unit(r, k, v, lw, kn, b, get_state):
    n = 2 * CHUNK
    row = lax.broadcasted_iota(jnp.int32, (CHUNK, PAIR), 0)
    lane_head0 = lax.broadcasted_iota(jnp.int32, (CHUNK, PAIR), 1) < HEAD_DIM
    c = lw
    for s in (1, 2, 4, 8, 16, 32):
        c = c + jnp.where(row >= s, pltpu.roll(c, s, axis=0), 0.0)
    c_end = c[CHUNK - 1:CHUNK, :]
    e_neg = jnp.exp(-c)
    e_end = jnp.exp(c_end - c)
    a_t = _block_diag(-kn * jnp.exp(c - lw), lane_head0)
    r_t = _block_diag(r * jnp.exp(c), lane_head0)
    b_t = _block_diag(b * e_neg, lane_head0)
    k_t = _block_diag(k * e_neg, lane_head0)
    b_e = _block_diag(b * e_end, lane_head0)
    k_e = _block_diag(k * e_end, lane_head0)
    v_bd = _block_diag(v, lane_head0)

    yield
    g = _dot_nt(jnp.concatenate([a_t, r_t], axis=0), jnp.concatenate([b_t, k_t], axis=0))
    ri = lax.broadcasted_iota(jnp.int32, (n, n), 0)
    ci = lax.broadcasted_iota(jnp.int32, (n, n), 1)
    m_ab = jnp.where(ci < ri, g[:n, :n], 0.0)
    m_ak = jnp.where(ci < ri, g[:n, n:], 0.0)
    a_rb = jnp.where(ci <= ri, g[n:, :n], 0.0)
    a_rk = jnp.where(ci <= ri, g[n:, n:], 0.0)

    t_inv = jnp.where(ci == ri, 1.0, m_ab)
    yield
    power = _dot(m_ab, m_ab)
    w1 = _dot(m_ak, v_bd)
    for _ in range(4):
        yield
        prod = _dot(power, jnp.concatenate([power, t_inv], axis=1))
        power = prod[:, :n]
        t_inv = t_inv + prod[:, n:]
    yield
    t_inv = t_inv + _dot(power, t_inv)

    yield
    tw = _dot(t_inv, jnp.concatenate([w1, a_t], axis=1))
    u0 = tw[:, :n]
    a_hat = tw[:, n:]
    rhs = jnp.concatenate([jnp.concatenate([a_hat, u0], axis=1),
                           jnp.concatenate([jnp.zeros_like(v_bd), v_bd], axis=1)], axis=0)
    yield
    pq = _dot_tn(jnp.concatenate([b_e, k_e], axis=0), rhs)
    ry = _dot(jnp.concatenate([a_rb, a_rk], axis=1), rhs)
    decay_end = jnp.exp(jnp.broadcast_to(c_end, (n, n)))
    p_mat = pq[:, :n] + jnp.where(ci == ri, decay_end, 0.0)
    yield
    state = get_state()
    ys = _dot(jnp.concatenate([r_t + ry[:, :n], p_mat], axis=0), state)
    y_bd = ys[:n] + ry[:, n:]
    y = y_bd[:CHUNK] + y_bd[CHUNK:]
    new_state = ys[n:] + pq[:, n:]
    return y, new_state


def _interleave(generators, results):
    pending = list(range(len(generators)))
    while pending:
        still = []
        for idx in pending:
            try:
                next(generators[idx])
                still.append(idx)
            except StopIteration as stop:
                results[idx] = stop.value
        pending = still


def _rwkv_chunk_kernel(r_ref, k_ref, v_ref, lw_ref, kn_ref, b_ref, y_ref, state_scr, *, n_units):
    @pl.when(pl.program_id(0) == 0)
    def _():
        state_scr[...] = jnp.zeros_like(state_scr)

    jobs = []
    outs = [None] * (CHUNKS_PER_STEP * n_units)
    for sub in range(CHUNKS_PER_STEP):
        rows = slice(sub * CHUNK, (sub + 1) * CHUNK)
        for u in range(n_units):
            bi, pi = u // N_PAIRS, u % N_PAIRS
            if sub == 0:
                get_state = lambda u=u: state_scr[u]
            else:
                get_state = lambda u=u, sub=sub: outs[(sub - 1) * n_units + u][1]
            jobs.append(_chunk_unit(r_ref[bi, pi, rows].astype(F32), k_ref[bi, pi, rows].astype(F32),
                                    v_ref[bi, pi, rows].astype(F32), lw_ref[bi, pi, rows],
                                    kn_ref[bi, pi, rows].astype(F32), b_ref[bi, pi, rows].astype(F32),
                                    get_state))
    _interleave(jobs, outs)
    for sub in range(CHUNKS_PER_STEP):
        rows = slice(sub * CHUNK, (sub + 1) * CHUNK)
        for u in range(n_units):
            bi, pi = u // N_PAIRS, u % N_PAIRS
            y, new_state = outs[sub * n_units + u]
            y_ref[bi, pi, rows] = y.astype(BF16)
            if sub == CHUNKS_PER_STEP - 1:
                state_scr[u] = new_state


def _rwkv_chunk(r, k, v, lw, kn, b):
    bsz, _, lp, _ = r.shape
    n_units = bsz * N_PAIRS
    rows = CHUNKS_PER_STEP * CHUNK
    spec = pl.BlockSpec((bsz, N_PAIRS, rows, PAIR), lambda c: (0, 0, c, 0))
    return pl.pallas_call(
        functools.partial(_rwkv_chunk_kernel, n_units=n_units),
        grid=(lp // rows,),
        in_specs=[spec] * 6,
        out_specs=spec,
        out_shape=jax.ShapeDtypeStruct(r.shape, BF16),
        scratch_shapes=[pltpu.VMEM((n_units, PAIR, PAIR), F32)],
        compiler_params=_params(("arbitrary",)),
        name="rwkv_chunk",
    )(r, k, v, lw, kn, b)


def _diff_attn_kernel(q_ref, k_ref, v_ref, lq1_ref, lk1_ref, lq2_ref, lk2_ref, subw_ref,
                      *rest, tq, kv_tiles, block0, aliased):
    o_ref, m_scr, acc_scr, s0_scr = rest[1:] if aliased else rest
    kb = ATTN_KEY_BLOCK
    diag = block0 + (pl.program_id(2) * tq) // kb
    head0 = lax.broadcasted_iota(jnp.int32, (tq, PAIR), 1) < HEAD_DIM
    qq = []
    for t in range(ATTN_HEADS_PER_STEP):
        q = q_ref[0, :, t * PAIR:(t + 1) * PAIR]
        zero = jnp.zeros_like(q)
        qq.append(jnp.concatenate([jnp.where(head0, q, zero), jnp.where(head0, zero, q)], axis=0))

    def scores(t, j, width):
        start = pl.multiple_of(j * kb, kb)
        return _dot_nt(qq[t], k_ref[0, pl.ds(start, width), t * PAIR:(t + 1) * PAIR])

    def step(j, width, masked, s_first=None, emit_next_first=None):
        start = pl.multiple_of(j * kb, kb)
        ones = jnp.ones((width, PAIR), BF16)
        s_next = scores(0, j, width) if s_first is None else s_first
        for t in range(ATTN_HEADS_PER_STEP):
            s = s_next
            if t + 1 < ATTN_HEADS_PER_STEP:
                s_next = scores(t + 1, j, width)
            elif emit_next_first is not None:
                emit_next_first()
            if masked:
                rr = lax.broadcasted_iota(jnp.int32, (2 * tq, width), 0) % tq
                cc = lax.broadcasted_iota(jnp.int32, (2 * tq, width), 1)
                s = jnp.where(cc <= rr, s, -1e30)
            row_max = jnp.max(s, axis=1, keepdims=True)
            if masked:
                m_next = jnp.broadcast_to(row_max, (2 * tq, PAIR))
            else:
                m_prev = m_scr[t]
                m_next = jnp.maximum(m_prev, row_max)
            p = jnp.exp2(s - jnp.tile(m_next, (1, width // LANE))).astype(BF16)
            v_ext = jnp.concatenate([v_ref[0, pl.ds(start, width), t * PAIR:(t + 1) * PAIR], ones], axis=1)
            pv = jnp.dot(p, v_ext, preferred_element_type=F32)
            if masked:
                acc_scr[t] = pv
            else:
                acc_scr[t] = jnp.tile(jnp.exp2(m_prev - m_next), (1, 2)) * acc_scr[t] + pv
            m_scr[t] = m_next

    n_big = diag // kv_tiles
    big = kv_tiles * kb

    def big_body(j, carry):
        def emit_next_first():
            j_next = jnp.minimum(j + 1, n_big - 1)
            s0_scr[...] = scores(0, j_next * kv_tiles, big)

        step(j * kv_tiles, big, False, s_first=s0_scr[...], emit_next_first=emit_next_first)
        return carry

    def small_body(j, carry):
        step(j, kb, False)
        return carry

    def emit_first_big():
        s0_scr[...] = scores(0, 0, big)

    step(diag, max(tq, kb), True, emit_next_first=emit_first_big)
    lax.fori_loop(0, n_big, big_body, 0)
    lax.fori_loop(n_big * kv_tiles, diag, small_body, 0)

    lam = (jnp.exp(jnp.sum(lq1_ref[...] * lk1_ref[...], axis=1, keepdims=True))
           - jnp.exp(jnp.sum(lq2_ref[...] * lk2_ref[...], axis=1, keepdims=True)) + LAMBDA_INIT)
    for t in range(ATTN_HEADS_PER_STEP):
        acc = acc_scr[t]
        o = acc[:, :PAIR] / acc[:, PAIR:]
        od = o[:tq] - lam * o[tq:]
        o_ref[0, :tq, t * PAIR:(t + 1) * PAIR] = (
            _rms(od, subw_ref[...], SUBLN_EPS) * (1.0 - LAMBDA_INIT)).astype(o_ref.dtype)
    if o_ref.shape[1] > tq:
        o_ref[0, tq:, :] = jnp.zeros((o_ref.shape[1] - tq, o_ref.shape[2]), o_ref.dtype)


def _diff_attn(a5, lams, subw, *, tq, n_tiles, row0, kv_tiles, out_rows=None, prev_out=None):
    bsz, lp, _ = a5.shape
    width = ATTN_HEADS_PER_STEP * PAIR
    n_groups = D_MODEL // width
    out_rows = tq if out_rows is None else out_rows
    assert row0 % tq == 0 and row0 % out_rows == 0 and row0 % ATTN_KEY_BLOCK == 0
    assert tq % ATTN_KEY_BLOCK == 0 or n_tiles == 1
    q_spec = pl.BlockSpec((1, tq, width), lambda b, g, i: (b, row0 // tq + i, g))
    k_spec = pl.BlockSpec((1, lp, width), lambda b, g, i: (b, 0, n_groups + g))
    v_spec = pl.BlockSpec((1, lp, width), lambda b, g, i: (b, 0, 2 * n_groups + g))
    lam_spec = pl.BlockSpec((1, HEAD_DIM), lambda b, g, i: (0, 0))
    in_specs = [q_spec, k_spec, v_spec, lam_spec, lam_spec, lam_spec, lam_spec,
                pl.BlockSpec((1, PAIR), lambda b, g, i: (0, 0))]
    args = (a5, a5, a5, *lams, subw)
    aliases = {}
    if prev_out is not None:
        aliases = {len(args): 0}
        in_specs.append(pl.BlockSpec(memory_space=pl.ANY))
        args = args + (prev_out,)
    return pl.pallas_call(
        functools.partial(_diff_attn_kernel, tq=tq, kv_tiles=kv_tiles, block0=row0 // ATTN_KEY_BLOCK,
                          aliased=prev_out is not None),
        grid=(bsz, n_groups, n_tiles),
        in_specs=in_specs,
        out_specs=pl.BlockSpec((1, out_rows, width), lambda b, g, i: (b, row0 // out_rows + i, g)),
        out_shape=jax.ShapeDtypeStruct((bsz, lp, D_MODEL), BF16),
        input_output_aliases=aliases,
        scratch_shapes=[pltpu.VMEM((ATTN_HEADS_PER_STEP, 2 * tq, PAIR), F32),
                        pltpu.VMEM((ATTN_HEADS_PER_STEP, 2 * tq, 2 * PAIR), F32),
                        pltpu.VMEM((2 * tq, kv_tiles * ATTN_KEY_BLOCK), F32)],
        compiler_params=_params(("arbitrary", "arbitrary", "arbitrary")),
        name=f"diff_attn_q{tq}",
    )(*args)


def _merge_kernel(y_ref, bonus_ref, g_ref, od_ref, gate_r_ref, gate_d_ref, res_ref, lnw_ref, lnb_ref,
                  ones_ref, wo_r_ref, wo_d_ref, wout_ref, o_ref):
    ones_bd = ones_ref[...]
    parts = []
    for pair in range(0, N_PAIRS, 2):
        y = jnp.concatenate([y_ref[0, pair], y_ref[0, pair + 1]], axis=1).astype(F32)
        mean = _dot(y, ones_bd) * (1.0 / HEAD_DIM)
        yc = y - mean
        var = _dot(yc * yc, ones_bd) * (1.0 / HEAD_DIM)
        parts.append(yc * lax.rsqrt(var + RWKV_LN_EPS))
    yn = jnp.concatenate(parts, axis=1)
    o_rwkv = (yn * lnw_ref[...] + lnb_ref[...] + bonus_ref[0].astype(F32)) * g_ref[0].astype(F32)
    br_rwkv = _dot(o_rwkv, wo_r_ref[...])
    br_diff = jnp.dot(od_ref[0], wo_d_ref[...], preferred_element_type=F32)
    merged = (_sigmoid(gate_r_ref[0].astype(F32)) * br_rwkv
              + _sigmoid(gate_d_ref[0].astype(F32)) * br_diff)
    o_ref[0] = res_ref[0] + _dot(merged, wout_ref[...])


def _merge(y, bonus, g, od, a5, res, lnw, lnb, ones_pair, wo_r, wo_d, wout, *, tm):
    bsz, lp, _ = res.shape
    grid = (bsz, lp // tm)
    tok = lambda c: pl.BlockSpec((1, tm, c), lambda b, i: (b, i, 0))
    vec = pl.BlockSpec((1, D_MODEL), lambda b, i: (0, 0))
    mat = pl.BlockSpec((D_MODEL, D_MODEL), lambda b, i: (0, 0))
    gate_r_spec = pl.BlockSpec((1, tm, D_MODEL), lambda b, i: (b, i, 3))
    gate_d_spec = pl.BlockSpec((1, tm, D_MODEL), lambda b, i: (b, i, 4))
    return pl.pallas_call(
        _merge_kernel,
        grid=grid,
        in_specs=[pl.BlockSpec((1, N_PAIRS, tm, PAIR), lambda b, i: (b, 0, i, 0)),
                  tok(D_MODEL), tok(D_MODEL), tok(D_MODEL), gate_r_spec, gate_d_spec, tok(D_MODEL), vec, vec,
                  pl.BlockSpec((2 * PAIR, 2 * PAIR), lambda b, i: (0, 0)), mat, mat, mat],
        out_specs=tok(D_MODEL),
        out_shape=jax.ShapeDtypeStruct(res.shape, F32),
        compiler_params=_params(("arbitrary", "arbitrary")),
        name="merge",
    )(y, bonus, g, od, a5, a5, res, lnw, lnb, ones_pair, wo_r, wo_d, wout)


def _mlp_kernel(x_ref, nw_ref, w1_ref, w2_ref, fw_ref, o_ref, *, n_ff_chunks):
    x = x_ref[...]
    h = _rms(x, nw_ref[...], NORM_EPS).astype(BF16)
    acc = x
    ff = D_FF // n_ff_chunks
    for c in range(n_ff_chunks):
        z = jnp.dot(h, w1_ref[:, c * ff:(c + 1) * ff], preferred_element_type=F32)
        act = jnp.square(jnp.maximum(z, 0.0))
        acc = acc + _dot(act, w2_ref[c * ff:(c + 1) * ff, :])
    o_ref[...] = _rms(acc, fw_ref[...], NORM_EPS)


def _mlp(x3, nw, w1, w2, fw, *, seq, row0, tm):
    bsz = x3.shape[0]
    x_spec = pl.BlockSpec((None, pl.Element(tm), pl.Element(D_MODEL)), lambda b, i: (b, pl.multiple_of(row0 + i * tm, SUBLANE), 0))
    vec = pl.BlockSpec((1, D_MODEL), lambda b, i: (0, 0))
    return pl.pallas_call(
        functools.partial(_mlp_kernel, n_ff_chunks=4),
        grid=(bsz, seq // tm),
        in_specs=[x_spec, vec, pl.BlockSpec((D_MODEL, D_FF), lambda b, i: (0, 0)),
                  pl.BlockSpec((D_FF, D_MODEL), lambda b, i: (0, 0)), vec],
        out_specs=pl.BlockSpec((None, tm, D_MODEL), lambda b, i: (b, i, 0)),
        out_shape=jax.ShapeDtypeStruct((bsz, seq, D_MODEL), F32),
        compiler_params=_params(("arbitrary", "arbitrary")),
        name="mlp",
    )(x3, nw, w1, w2, fw)


def _pad_cols(a, width):
    return jnp.pad(a, ((0, 0), (0, width - a.shape[1])))


def _pad_rows(a, height):
    return jnp.pad(a, ((0, height - a.shape[0]), (0, 0)))


def _rope_tables(lp):
    inv = ROPE_THETA ** (-jnp.arange(0, ROPE_DIM, 2, dtype=F32) / ROPE_DIM)
    ang = jnp.arange(lp, dtype=F32)[:, None] * inv[None, :]
    cos, sin = jnp.cos(ang), jnp.sin(ang)
    rest = HEAD_DIM - ROPE_DIM
    ones = jnp.ones((lp, rest), F32)
    zeros = jnp.zeros((lp, rest), F32)
    zh = jnp.zeros_like(sin)
    c = jnp.concatenate([cos, cos, ones], axis=1)
    s1 = jnp.concatenate([-sin, zh, zeros], axis=1)
    s2 = jnp.concatenate([zh, sin, zeros], axis=1)
    return tuple(jnp.tile(t, (1, LANE // HEAD_DIM)) for t in (c, s1, s2))


def _pick(n, candidates):
    for c in candidates:
        if n % c == 0:
            return c
    raise ValueError(n)


def kernel(x, meta_tokens, norm_mix_w, w_in, rwkv_mu, rwkv_w0, rwkv_w2, rwkv_a0, rwkv_a2, rwkv_g2, rwkv_k_k, rwkv_k_a, rwkv_r_k, rwkv_ln_w, rwkv_ln_b, rwkv_w_o, diff_lq1, diff_lk1, diff_lq2, diff_lk2, diff_subln_w, diff_w_o, w_out, norm_mlp_w, mlp_w1, mlp_w2, final_norm_w):
    bsz, seq, _ = x.shape
    seqlen = seq + N_META_TOKENS
    lp = -(-seqlen // SEQ_ALIGN) * SEQ_ALIGN
    tm_proj = _pick(lp, (768, 512, 256))

    meta = jnp.broadcast_to(meta_tokens[None].astype(x.dtype), (bsz, N_META_TOKENS, D_MODEL))
    h_res = jnp.concatenate([meta, x, jnp.zeros((bsz, lp - seqlen, D_MODEL), x.dtype)], axis=1)
    hf = h_res.reshape(bsz * lp, D_MODEL)

    w = w_in.reshape(D_MODEL, -1).astype(BF16)
    c3 = 3 * D_MODEL
    w_rwkv = jnp.concatenate(
        [w[:, :c3], _pad_cols(w[:, c3:c3 + DECAY_LORA], LANE),
         _pad_cols(w[:, c3 + DECAY_LORA:c3 + DECAY_LORA + AAA_LORA], LANE),
         _pad_cols(w[:, c3 + DECAY_LORA + AAA_LORA:c3 + DECAY_LORA + AAA_LORA + GATE_LORA], GATE_LORA_PAD)],
        axis=1)
    rwkv_cols = c3 + DECAY_LORA + AAA_LORA + GATE_LORA
    w_diff = w[:, rwkv_cols:]
    mu = rwkv_mu[0][None, :]
    mu_p = jnp.concatenate(
        [mu[:, :c3], _pad_cols(mu[:, c3:c3 + DECAY_LORA], LANE),
         _pad_cols(mu[:, c3 + DECAY_LORA:c3 + DECAY_LORA + AAA_LORA], LANE),
         _pad_cols(mu[:, c3 + DECAY_LORA + AAA_LORA:], GATE_LORA_PAD)], axis=1)
    nw = norm_mix_w[0][None, :]

    tiles_per_seq = lp // tm_proj
    head_id = jnp.arange(2 * PAIR) // HEAD_DIM
    ones_bd = (head_id[:, None] == head_id[None, :]).astype(BF16)
    vecs = (rwkv_w0[0][None, :], rwkv_a0[0][None, :], rwkv_k_k[0][None, :], rwkv_k_a[0][None, :],
            rwkv_r_k[0].reshape(1, D_MODEL))
    r, kmod, v, lw, kn, b, bonus, g, xn = _rwkv_proj(
        hf, nw, w_rwkv, mu_p, vecs, _pad_rows(rwkv_w2[0], LANE).astype(BF16),
        _pad_rows(rwkv_a2[0], LANE).astype(BF16), _pad_rows(rwkv_g2[0], GATE_LORA_PAD).astype(BF16),
        ones_bd, bsz=bsz, lp=lp, tm=tm_proj)
    y = _rwkv_chunk(r, kmod, v, lw, kn, b)
    a5 = _in_proj(xn, w_diff, _rope_tables(lp), tm=tm_proj, tiles_per_seq=tiles_per_seq).reshape(
        bsz, lp, 5 * D_MODEL)

    lams = (diff_lq1, diff_lk1, diff_lq2, diff_lk2)
    subw = diff_subln_w[0][None, :]
    kv_tiles = min(4, lp // ATTN_KEY_BLOCK)
    o_diff, row0 = None, 0
    for tq in (ATTN_Q_TILE, ATTN_KEY_BLOCK):
        n_tiles = (seqlen - row0) // tq
        if n_tiles:
            o_diff = _diff_attn(a5, lams, subw, tq=tq, n_tiles=n_tiles, row0=row0, kv_tiles=kv_tiles,
                                prev_out=o_diff)
            row0 += n_tiles * tq
    if row0 < lp:
        tail = max(-(-(seqlen - row0) // ATTN_TAIL_ROWS) * ATTN_TAIL_ROWS, ATTN_TAIL_ROWS)
        o_diff = _diff_attn(a5, lams, subw, tq=tail, n_tiles=1, row0=row0, kv_tiles=kv_tiles,
                            out_rows=lp - row0, prev_out=o_diff)

    h_mid = _merge(y, bonus, g, o_diff, a5, h_res, rwkv_ln_w[0][None, :], rwkv_ln_b[0][None, :],
                   ones_bd, rwkv_w_o[0].astype(BF16), diff_w_o[0].astype(BF16),
                   w_out[0].astype(BF16), tm=_pick(lp, (768, 384, 256)))
    return _mlp(h_mid, norm_mlp_w[0][None, :], mlp_w1[0].astype(BF16), mlp_w2[0].astype(BF16),
                final_norm_w[None, :], seq=seq, row0=N_META_TOKENS, tm=_pick(seq, (512, 256, 128)))
```

```python
import functools
import math
from typing import NamedTuple

import jax
import jax.numpy as jnp
from jax import lax
from jax.experimental import pallas as pl
from jax.experimental.pallas import tpu as pltpu

F32 = jnp.float32
BF16 = jnp.bfloat16

D_MODEL = 1024
N_META_TOKENS = 16
HEAD_DIM = 64
PAIR = 2 * HEAD_DIM
N_PAIRS = D_MODEL // PAIR
DECAY_LORA = 64
AAA_LORA = 64
GATE_LORA = 160
RWKV_LN_EPS = 64e-5
ROPE_THETA = 500000.0
ROPE_DIM = HEAD_DIM // 4
ROPE_HALF = ROPE_DIM // 2
D_FF = 4 * D_MODEL
NORM_EPS = 1e-5
SUBLN_EPS = 1e-5
LAMBDA_INIT = 0.8 - 0.6 * math.exp(-0.3 * 0)

LANE = 128
SUBLANE = 8
PREV_ROWS = 16
CHUNK = 64
CHUNKS_PER_STEP = 2
PROJ_COL_CHUNK = 256
ATTN_KEY_BLOCK = 256
ATTN_Q_TILE = 512
ATTN_TAIL_ROWS = 16
ATTN_HEADS_PER_STEP = 4
ATTN_BIG_STEP_BLOCKS = 4
MLP_FF_CHUNKS = 4
PROJ_TILES = (768, 512, 256)
MERGE_TILES = (768, 384, 256)
MLP_TILES = (512, 256, 128)
Q_SCALE = HEAD_DIM ** -0.5 * math.log2(math.e)
SEQ_ALIGN = 256
VMEM_LIMIT = 56 * 1024 * 1024

LORA_W_OFF = 3 * D_MODEL
LORA_A_OFF = LORA_W_OFF + LANE
LORA_G_OFF = LORA_A_OFF + LANE
GATE_LORA_PAD = 2 * LANE


def _params(sem):
    return pltpu.CompilerParams(dimension_semantics=sem, vmem_limit_bytes=VMEM_LIMIT)


def _rms(x, w, eps):
    return x * lax.rsqrt(jnp.mean(x * x, axis=-1, keepdims=True) + eps) * w


def _sigmoid(x):
    return 1.0 / (1.0 + jnp.exp(-x))


def _dot(a, b):
    return jnp.dot(a.astype(BF16), b.astype(BF16), preferred_element_type=F32)


def _dot_nt(a, b):
    return lax.dot_general(a.astype(BF16), b.astype(BF16), (((1,), (1,)), ((), ())),
                           preferred_element_type=F32)


def _dot_tn(a, b):
    return lax.dot_general(a.astype(BF16), b.astype(BF16), (((0,), (0,)), ((), ())),
                           preferred_element_type=F32)


def _in_proj_kernel(xn_ref, w_ref, cos_ref, s1_ref, s2_ref, o_ref):
    n_cols = o_ref.shape[1]
    cc = PROJ_COL_CHUNK
    xn = xn_ref[...]
    reps = cc // LANE
    cos = jnp.tile(cos_ref[...], (1, reps))
    s1 = jnp.tile(s1_ref[...], (1, reps))
    s2 = jnp.tile(s2_ref[...], (1, reps))
    for c in range(n_cols // cc):
        cols = slice(c * cc, (c + 1) * cc)
        p = jnp.dot(xn, w_ref[:, cols], preferred_element_type=F32)
        if c * cc < 2 * D_MODEL:
            nxt = pltpu.roll(p, cc - ROPE_HALF, axis=1)
            prv = pltpu.roll(p, ROPE_HALF, axis=1)
            scale = Q_SCALE if c * cc < D_MODEL else 1.0
            p = (p * cos + nxt * s1 + prv * s2) * scale
        o_ref[:, cols] = p.astype(o_ref.dtype)


def _in_proj(xn, w, rope, *, tm, tiles_per_seq):
    t_tokens = xn.shape[0]
    n_cols = w.shape[1]
    tab_spec = pl.BlockSpec((tm, LANE), lambda i: (i % tiles_per_seq, 0))
    return pl.pallas_call(
        _in_proj_kernel,
        grid=(t_tokens // tm,),
        in_specs=[pl.BlockSpec((tm, D_MODEL), lambda i: (i, 0)),
                  pl.BlockSpec((D_MODEL, n_cols), lambda i: (0, 0), pipeline_mode=pl.Buffered(1)),
                  tab_spec, tab_spec, tab_spec],
        out_specs=pl.BlockSpec((tm, n_cols), lambda i: (i, 0)),
        out_shape=jax.ShapeDtypeStruct((t_tokens, n_cols), BF16),
        compiler_params=_params(("arbitrary",)),
        name="in_proj_diff",
    )(xn, w, *rope)


def _rwkv_proj_kernel(x_ref, xprev_ref, nw_ref, w_ref, mu_ref, w0_ref, a0_ref, kk_ref, ka_ref, rk_ref,
                      w2_ref, a2_ref, g2_ref, ones_ref, r_o, k_o, v_o, lw_o, kn_o, b_o, bonus_o, g_o, xn_o,
                      *, tiles_per_seq):
    tm = x_ref.shape[0]
    blk = 2 * PAIR
    xn = jnp.concatenate([_rms(x_ref[...], nw_ref[...], NORM_EPS).astype(BF16),
                          _rms(xprev_ref[...], nw_ref[...], NORM_EPS).astype(BF16)], axis=0)
    first = (pl.program_id(0) % tiles_per_seq) == 0
    row = lax.broadcasted_iota(jnp.int32, (tm, blk), 0)
    xn_o[...] = xn[:tm]

    def proj(col0):
        cols = slice(col0, col0 + blk)
        p = jnp.dot(xn, w_ref[:, cols], preferred_element_type=F32)
        prev_row = jnp.where(first, 0.0, p[tm + PREV_ROWS - 1:tm + PREV_ROWS, :])
        p = p[:tm]
        shifted = jnp.where(row == 0, prev_row, pltpu.roll(p, 1, axis=0))
        return p + (shifted - p) * mu_ref[:, cols]

    def rkv(c):
        return proj(c * blk), proj(D_MODEL + c * blk), proj(2 * D_MODEL + c * blk)

    p_wa = proj(LORA_W_OFF)
    tanh_wd = jnp.tanh(p_wa[:, :LANE]).astype(BF16)
    ad = p_wa[:, LANE:].astype(BF16)
    sig_gd = _sigmoid(proj(LORA_G_OFF)).astype(BF16)
    ones_bd = ones_ref[...]
    n_blocks = D_MODEL // blk
    nxt = rkv(0)
    for c in range(n_blocks):
        cols = slice(c * blk, (c + 1) * blk)
        r, k, v = nxt
        if c + 1 < n_blocks:
            nxt = rkv(c + 1)
        z = w0_ref[:, cols] + jnp.dot(tanh_wd, w2_ref[:, cols], preferred_element_type=F32)
        log_decay = -math.exp(-0.5) * _sigmoid(z)
        a_rate = _sigmoid(a0_ref[:, cols] + jnp.dot(ad, a2_ref[:, cols], preferred_element_type=F32))
        gate = jnp.dot(sig_gd, g2_ref[:, cols], preferred_element_type=F32)

        kraw = k * kk_ref[:, cols]
        sumsq = _dot(kraw * kraw, ones_bd)
        knorm = kraw * lax.rsqrt(jnp.maximum(sumsq, 1e-24))
        kmod = k * (1.0 + (a_rate - 1.0) * ka_ref[:, cols])
        bonus = _dot(r * kmod * rk_ref[:, cols], ones_bd) * v

        bonus_o[0, :, cols] = bonus.astype(BF16)
        g_o[0, :, cols] = gate.astype(BF16)
        r16, k16, v16 = r.astype(BF16), kmod.astype(BF16), v.astype(BF16)
        kn16, b16 = knorm.astype(BF16), (knorm * a_rate).astype(BF16)
        for h in range(2):
            pair = 2 * c + h
            sl = slice(h * PAIR, (h + 1) * PAIR)
            r_o[0, pair] = r16[:, sl]
            k_o[0, pair] = k16[:, sl]
            v_o[0, pair] = v16[:, sl]
            lw_o[0, pair] = log_decay[:, sl]
            kn_o[0, pair] = kn16[:, sl]
            b_o[0, pair] = b16[:, sl]


def _rwkv_proj(xf, nw, w, mu, vecs, w2p, a2p, g2p, ones_bd, *, bsz, lp, tm):
    tiles_per_seq = lp // tm
    n_cols = w.shape[1]
    blocks = tm // PREV_ROWS
    const = lambda shape: pl.BlockSpec(shape, lambda i: (0, 0))
    pair_spec = pl.BlockSpec((1, N_PAIRS, tm, PAIR), lambda i: (i // tiles_per_seq, 0, i % tiles_per_seq, 0))
    tok_spec = pl.BlockSpec((1, tm, D_MODEL), lambda i: (i // tiles_per_seq, i % tiles_per_seq, 0))
    pair_shape = jax.ShapeDtypeStruct((bsz, N_PAIRS, lp, PAIR), BF16)
    decay_shape = jax.ShapeDtypeStruct((bsz, N_PAIRS, lp, PAIR), F32)
    tok_shape = jax.ShapeDtypeStruct((bsz, lp, D_MODEL), BF16)
    return pl.pallas_call(
        functools.partial(_rwkv_proj_kernel, tiles_per_seq=tiles_per_seq),
        grid=(bsz * tiles_per_seq,),
        in_specs=[pl.BlockSpec((tm, D_MODEL), lambda i: (i, 0)),
                  pl.BlockSpec((PREV_ROWS, D_MODEL), lambda i: (jnp.maximum(i * blocks - 1, 0), 0)),
                  const((1, D_MODEL)),
                  pl.BlockSpec((D_MODEL, n_cols), lambda i: (0, 0), pipeline_mode=pl.Buffered(1)),
                  const((1, n_cols))]
        + [const((1, D_MODEL))] * 5
        + [const((LANE, D_MODEL)), const((LANE, D_MODEL)), const((GATE_LORA_PAD, D_MODEL)),
           const((2 * PAIR, 2 * PAIR))],
        out_specs=[pair_spec] * 6 + [tok_spec] * 2 + [pl.BlockSpec((tm, D_MODEL), lambda i: (i, 0))],
        out_shape=[pair_shape] * 3 + [decay_shape] + [pair_shape] * 2 + [tok_shape] * 2
        + [jax.ShapeDtypeStruct((bsz * lp, D_MODEL), BF16)],
        compiler_params=_params(("arbitrary",)),
        name="rwkv_proj",
    )(xf, xf, nw, w, mu, *vecs, w2p, a2p, g2p, ones_bd)


def _block_diag(x, lane_head0):
    return jnp.concatenate([jnp.where(lane_head0, x, 0.0), jnp.where(lane_head0, 0.0, x)], axis=0)


def _chunk_unit(r, k, v, lw, kn, b, get_state):
    n = 2 * CHUNK
    row = lax.broadcasted_iota(jnp.int32, (CHUNK, PAIR), 0)
    lane_head0 = lax.broadcasted_iota(jnp.int32, (CHUNK, PAIR), 1) < HEAD_DIM
    c = lw
    for s in (1, 2, 4, 8, 16, 32):
        c = c + jnp.where(row >= s, pltpu.roll(c, s, axis=0), 0.0)
    c_end = c[CHUNK - 1:CHUNK, :]
    e_neg = jnp.exp(-c)
    e_end = jnp.exp(c_end - c)
    a_t = _block_diag(-kn * jnp.exp(c - lw), lane_head0)
    r_t = _block_diag(r * jnp.exp(c), lane_head0)
    b_t = _block_diag(b * e_neg, lane_head0)
    k_t = _block_diag(k * e_neg, lane_head0)
    b_e = _block_diag(b * e_end, lane_head0)
    k_e = _block_diag(k * e_end, lane_head0)
    v_bd = _block_diag(v, lane_head0)

    yield
    g = _dot_nt(jnp.concatenate([a_t, r_t], axis=0), jnp.concatenate([b_t, k_t], axis=0))
    ri = lax.broadcasted_iota(jnp.int32, (n, n), 0)
    ci = lax.broadcasted_iota(jnp.int32, (n, n), 1)
    m_ab = jnp.where(ci < ri, g[:n, :n], 0.0)
    m_ak = jnp.where(ci < ri, g[:n, n:], 0.0)
    a_rb = jnp.where(ci <= ri, g[n:, :n], 0.0)
    a_rk = jnp.where(ci <= ri, g[n:, n:], 0.0)

    t_inv = jnp.where(ci == ri, 1.0, m_ab)
    yield
    power = _dot(m_ab, m_ab)
    w1 = _dot(m_ak, v_bd)
    for _ in range(4):
        yield
        prod = _dot(power, jnp.concatenate([power, t_inv], axis=1))
        power = prod[:, :n]
        t_inv = t_inv + prod[:, n:]
    yield
    t_inv = t_inv + _dot(power, t_inv)

    yield
    tw = _dot(t_inv, jnp.concatenate([w1, a_t], axis=1))
    u0 = tw[:, :n]
    a_hat = tw[:, n:]
    rhs = jnp.concatenate([jnp.concatenate([a_hat, u0], axis=1),
                           jnp.concatenate([jnp.zeros_like(v_bd), v_bd], axis=1)], axis=0)
    yield
    pq = _dot_tn(jnp.concatenate([b_e, k_e], axis=0), rhs)
    ry = _dot(jnp.concatenate([a_rb, a_rk], axis=1), rhs)
    decay_end = jnp.exp(jnp.broadcast_to(c_end, (n, n)))
    p_mat = pq[:, :n] + jnp.where(ci == ri, decay_end, 0.0)
    yield
    state = get_state()
    ys = _dot(jnp.concatenate([r_t + ry[:, :n], p_mat], axis=0), state)
    y_bd = ys[:n] + ry[:, n:]
    y = y_bd[:CHUNK] + y_bd[CHUNK:]
    new_state = ys[n:] + pq[:, n:]
    return y, new_state


def _interleave(generators, results):
    pending = list(range(len(generators)))
    while pending:
        still = []
        for idx in pending:
            try:
                next(generators[idx])
                still.append(idx)
            except StopIteration as stop:
                results[idx] = stop.value
        pending = still


def _rwkv_chunk_kernel(r_ref, k_ref, v_ref, lw_ref, kn_ref, b_ref, y_ref, state_scr, *, n_units):
    @pl.when(pl.program_id(0) == 0)
    def _():
        state_scr[...] = jnp.zeros_like(state_scr)

    jobs = []
    outs = [None] * (CHUNKS_PER_STEP * n_units)
    for sub in range(CHUNKS_PER_STEP):
        rows = slice(sub * CHUNK, (sub + 1) * CHUNK)
        for u in range(n_units):
            bi, pi = u // N_PAIRS, u % N_PAIRS
            if sub == 0:
                get_state = lambda u=u: state_scr[u]
            else:
                get_state = lambda u=u, sub=sub: outs[(sub - 1) * n_units + u][1]
            jobs.append(_chunk_unit(r_ref[bi, pi, rows].astype(F32), k_ref[bi, pi, rows].astype(F32),
                                    v_ref[bi, pi, rows].astype(F32), lw_ref[bi, pi, rows],
                                    kn_ref[bi, pi, rows].astype(F32), b_ref[bi, pi, rows].astype(F32),
                                    get_state))
    _interleave(jobs, outs)
    for sub in range(CHUNKS_PER_STEP):
        rows = slice(sub * CHUNK, (sub + 1) * CHUNK)
        for u in range(n_units):
            bi, pi = u // N_PAIRS, u % N_PAIRS
            y, new_state = outs[sub * n_units + u]
            y_ref[bi, pi, rows] = y.astype(BF16)
            if sub == CHUNKS_PER_STEP - 1:
                state_scr[u] = new_state


def _rwkv_chunk(r, k, v, lw, kn, b):
    bsz, _, lp, _ = r.shape
    n_units = bsz * N_PAIRS
    rows = CHUNKS_PER_STEP * CHUNK
    spec = pl.BlockSpec((bsz, N_PAIRS, rows, PAIR), lambda c: (0, 0, c, 0))
    return pl.pallas_call(
        functools.partial(_rwkv_chunk_kernel, n_units=n_units),
        grid=(lp // rows,),
        in_specs=[spec] * 6,
        out_specs=spec,
        out_shape=jax.ShapeDtypeStruct(r.shape, BF16),
        scratch_shapes=[pltpu.VMEM((n_units, PAIR, PAIR), F32)],
        compiler_params=_params(("arbitrary",)),
        name="rwkv_chunk",
    )(r, k, v, lw, kn, b)


def _diff_attn_kernel(q_ref, k_ref, v_ref, lq1_ref, lk1_ref, lq2_ref, lk2_ref, subw_ref,
                      *rest, tq, kv_tiles, block0, aliased):
    o_ref, m_scr, acc_scr, s0_scr = rest[1:] if aliased else rest
    kb = ATTN_KEY_BLOCK
    diag = block0 + (pl.program_id(2) * tq) // kb
    head0 = lax.broadcasted_iota(jnp.int32, (tq, PAIR), 1) < HEAD_DIM
    qq = []
    for t in range(ATTN_HEADS_PER_STEP):
        q = q_ref[0, :, t * PAIR:(t + 1) * PAIR]
        zero = jnp.zeros_like(q)
        qq.append(jnp.concatenate([jnp.where(head0, q, zero), jnp.where(head0, zero, q)], axis=0))

    def scores(t, j, width):
        start = pl.multiple_of(j * kb, kb)
        return _dot_nt(qq[t], k_ref[0, pl.ds(start, width), t * PAIR:(t + 1) * PAIR])

    def step(j, width, masked, s_first=None, emit_next_first=None):
        start = pl.multiple_of(j * kb, kb)
        ones = jnp.ones((width, PAIR), BF16)
        s_next = scores(0, j, width) if s_first is None else s_first
        for t in range(ATTN_HEADS_PER_STEP):
            s = s_next
            if t + 1 < ATTN_HEADS_PER_STEP:
                s_next = scores(t + 1, j, width)
            elif emit_next_first is not None:
                emit_next_first()
            if masked:
                rr = lax.broadcasted_iota(jnp.int32, (2 * tq, width), 0) % tq
                cc = lax.broadcasted_iota(jnp.int32, (2 * tq, width), 1)
                s = jnp.where(cc <= rr, s, -1e30)
            row_max = jnp.max(s, axis=1, keepdims=True)
            if masked:
                m_next = jnp.broadcast_to(row_max, (2 * tq, PAIR))
            else:
                m_prev = m_scr[t]
                m_next = jnp.maximum(m_prev, row_max)
            p = jnp.exp2(s - jnp.tile(m_next, (1, width // LANE))).astype(BF16)
            v_ext = jnp.concatenate([v_ref[0, pl.ds(start, width), t * PAIR:(t + 1) * PAIR], ones], axis=1)
            pv = jnp.dot(p, v_ext, preferred_element_type=F32)
            if masked:
                acc_scr[t] = pv
            else:
                acc_scr[t] = jnp.tile(jnp.exp2(m_prev - m_next), (1, 2)) * acc_scr[t] + pv
            m_scr[t] = m_next

    n_big = diag // kv_tiles
    big = kv_tiles * kb

    def big_body(j, carry):
        def emit_next_first():
            j_next = jnp.minimum(j + 1, n_big - 1)
            s0_scr[...] = scores(0, j_next * kv_tiles, big)

        step(j * kv_tiles, big, False, s_first=s0_scr[...], emit_next_first=emit_next_first)
        return carry

    rest_blocks = max(tq // kb, 1)

    def rest_body(j, carry):
        step(n_big * kv_tiles + j * rest_blocks, rest_blocks * kb, False)
        return carry

    def emit_first_big():
        s0_scr[...] = scores(0, 0, big)

    step(diag, max(tq, kb), True, emit_next_first=emit_first_big)
    lax.fori_loop(0, n_big, big_body, 0)
    lax.fori_loop(0, (diag - n_big * kv_tiles) // rest_blocks, rest_body, 0)

    lam = (jnp.exp(jnp.sum(lq1_ref[...] * lk1_ref[...], axis=1, keepdims=True))
           - jnp.exp(jnp.sum(lq2_ref[...] * lk2_ref[...], axis=1, keepdims=True)) + LAMBDA_INIT)
    for t in range(ATTN_HEADS_PER_STEP):
        acc = acc_scr[t]
        o = acc[:, :PAIR] / acc[:, PAIR:]
        od = o[:tq] - lam * o[tq:]
        o_ref[0, :tq, t * PAIR:(t + 1) * PAIR] = (
            _rms(od, subw_ref[...], SUBLN_EPS) * (1.0 - LAMBDA_INIT)).astype(o_ref.dtype)
    if o_ref.shape[1] > tq:
        o_ref[0, tq:, :] = jnp.zeros((o_ref.shape[1] - tq, o_ref.shape[2]), o_ref.dtype)


def _diff_attn(a5, lams, subw, *, tq, n_tiles, row0, kv_tiles, out_rows=None, prev_out=None):
    bsz, lp, _ = a5.shape
    width = ATTN_HEADS_PER_STEP * PAIR
    n_groups = D_MODEL // width
    out_rows = tq if out_rows is None else out_rows
    assert row0 % tq == 0 and row0 % out_rows == 0 and row0 % ATTN_KEY_BLOCK == 0
    assert tq % ATTN_KEY_BLOCK == 0 or n_tiles == 1
    q_spec = pl.BlockSpec((1, tq, width), lambda b, g, i: (b, row0 // tq + i, g))
    k_spec = pl.BlockSpec((1, lp, width), lambda b, g, i: (b, 0, n_groups + g))
    v_spec = pl.BlockSpec((1, lp, width), lambda b, g, i: (b, 0, 2 * n_groups + g))
    lam_spec = pl.BlockSpec((1, HEAD_DIM), lambda b, g, i: (0, 0))
    in_specs = [q_spec, k_spec, v_spec, lam_spec, lam_spec, lam_spec, lam_spec,
                pl.BlockSpec((1, PAIR), lambda b, g, i: (0, 0))]
    args = (a5, a5, a5, *lams, subw)
    aliases = {}
    if prev_out is not None:
        aliases = {len(args): 0}
        in_specs.append(pl.BlockSpec(memory_space=pl.ANY))
        args = args + (prev_out,)
    return pl.pallas_call(
        functools.partial(_diff_attn_kernel, tq=tq, kv_tiles=kv_tiles, block0=row0 // ATTN_KEY_BLOCK,
                          aliased=prev_out is not None),
        grid=(bsz, n_groups, n_tiles),
        in_specs=in_specs,
        out_specs=pl.BlockSpec((1, out_rows, width), lambda b, g, i: (b, row0 // out_rows + i, g)),
        out_shape=jax.ShapeDtypeStruct((bsz, lp, D_MODEL), BF16),
        input_output_aliases=aliases,
        scratch_shapes=[pltpu.VMEM((ATTN_HEADS_PER_STEP, 2 * tq, PAIR), F32),
                        pltpu.VMEM((ATTN_HEADS_PER_STEP, 2 * tq, 2 * PAIR), F32),
                        pltpu.VMEM((2 * tq, kv_tiles * ATTN_KEY_BLOCK), F32)],
        compiler_params=_params(("arbitrary", "arbitrary", "arbitrary")),
        name=f"diff_attn_q{tq}",
    )(*args)


def _merge_kernel(y_ref, bonus_ref, g_ref, od_ref, gate_r_ref, gate_d_ref, res_ref, lnw_ref, lnb_ref,
                  ones_ref, wo_r_ref, wo_d_ref, wout_ref, o_ref):
    ones_bd = ones_ref[...]
    parts = []
    for pair in range(0, N_PAIRS, 2):
        y = jnp.concatenate([y_ref[0, pair], y_ref[0, pair + 1]], axis=1).astype(F32)
        mean = _dot(y, ones_bd) * (1.0 / HEAD_DIM)
        yc = y - mean
        var = _dot(yc * yc, ones_bd) * (1.0 / HEAD_DIM)
        parts.append(yc * lax.rsqrt(var + RWKV_LN_EPS))
    yn = jnp.concatenate(parts, axis=1)
    o_rwkv = (yn * lnw_ref[...] + lnb_ref[...] + bonus_ref[0].astype(F32)) * g_ref[0].astype(F32)
    br_rwkv = _dot(o_rwkv, wo_r_ref[...])
    br_diff = jnp.dot(od_ref[0], wo_d_ref[...], preferred_element_type=F32)
    merged = (_sigmoid(gate_r_ref[0].astype(F32)) * br_rwkv
              + _sigmoid(gate_d_ref[0].astype(F32)) * br_diff)
    o_ref[0] = res_ref[0] + _dot(merged, wout_ref[...])


def _merge(y, bonus, g, od, a5, res, lnw, lnb, ones_pair, wo_r, wo_d, wout, *, tm):
    bsz, lp, _ = res.shape
    grid = (bsz, lp // tm)
    tok = lambda c: pl.BlockSpec((1, tm, c), lambda b, i: (b, i, 0))
    vec = pl.BlockSpec((1, D_MODEL), lambda b, i: (0, 0))
    mat = pl.BlockSpec((D_MODEL, D_MODEL), lambda b, i: (0, 0))
    gate_r_spec = pl.BlockSpec((1, tm, D_MODEL), lambda b, i: (b, i, 3))
    gate_d_spec = pl.BlockSpec((1, tm, D_MODEL), lambda b, i: (b, i, 4))
    return pl.pallas_call(
        _merge_kernel,
        grid=grid,
        in_specs=[pl.BlockSpec((1, N_PAIRS, tm, PAIR), lambda b, i: (b, 0, i, 0)),
                  tok(D_MODEL), tok(D_MODEL), tok(D_MODEL), gate_r_spec, gate_d_spec, tok(D_MODEL), vec, vec,
                  pl.BlockSpec((2 * PAIR, 2 * PAIR), lambda b, i: (0, 0)), mat, mat, mat],
        out_specs=tok(D_MODEL),
        out_shape=jax.ShapeDtypeStruct(res.shape, F32),
        compiler_params=_params(("arbitrary", "arbitrary")),
        name="merge",
    )(y, bonus, g, od, a5, a5, res, lnw, lnb, ones_pair, wo_r, wo_d, wout)


def _mlp_kernel(x_ref, nw_ref, w1_ref, w2_ref, fw_ref, o_ref, *, n_ff_chunks):
    x = x_ref[...]
    h = _rms(x, nw_ref[...], NORM_EPS).astype(BF16)
    acc = x
    ff = D_FF // n_ff_chunks
    for c in range(n_ff_chunks):
        z = jnp.dot(h, w1_ref[:, c * ff:(c + 1) * ff], preferred_element_type=F32)
        act = jnp.square(jnp.maximum(z, 0.0))
        acc = acc + _dot(act, w2_ref[c * ff:(c + 1) * ff, :])
    o_ref[...] = _rms(acc, fw_ref[...], NORM_EPS)


def _mlp(x3, nw, w1, w2, fw, *, seq, row0, tm):
    bsz = x3.shape[0]
    x_spec = pl.BlockSpec((None, pl.Element(tm), pl.Element(D_MODEL)), lambda b, i: (b, pl.multiple_of(row0 + i * tm, SUBLANE), 0))
    vec = pl.BlockSpec((1, D_MODEL), lambda b, i: (0, 0))
    return pl.pallas_call(
        functools.partial(_mlp_kernel, n_ff_chunks=MLP_FF_CHUNKS),
        grid=(bsz, seq // tm),
        in_specs=[x_spec, vec, pl.BlockSpec((D_MODEL, D_FF), lambda b, i: (0, 0)),
                  pl.BlockSpec((D_FF, D_MODEL), lambda b, i: (0, 0)), vec],
        out_specs=pl.BlockSpec((None, tm, D_MODEL), lambda b, i: (b, i, 0)),
        out_shape=jax.ShapeDtypeStruct((bsz, seq, D_MODEL), F32),
        compiler_params=_params(("arbitrary", "arbitrary")),
        name="mlp",
    )(x3, nw, w1, w2, fw)


def _pad_cols(a, width):
    return jnp.pad(a, ((0, 0), (0, width - a.shape[1])))


def _pad_rows(a, height):
    return jnp.pad(a, ((0, height - a.shape[0]), (0, 0)))


def _rope_tables(lp):
    inv = ROPE_THETA ** (-jnp.arange(0, ROPE_DIM, 2, dtype=F32) / ROPE_DIM)
    ang = jnp.arange(lp, dtype=F32)[:, None] * inv[None, :]
    cos, sin = jnp.cos(ang), jnp.sin(ang)
    rest = HEAD_DIM - ROPE_DIM
    ones = jnp.ones((lp, rest), F32)
    zeros = jnp.zeros((lp, rest), F32)
    zh = jnp.zeros_like(sin)
    c = jnp.concatenate([cos, cos, ones], axis=1)
    s1 = jnp.concatenate([-sin, zh, zeros], axis=1)
    s2 = jnp.concatenate([zh, sin, zeros], axis=1)
    return tuple(jnp.tile(t, (1, LANE // HEAD_DIM)) for t in (c, s1, s2))


def _pick(n, candidates):
    for c in candidates:
        if n % c == 0:
            return c
    raise ValueError(n)


class _Tiles(NamedTuple):
    proj: int
    merge: int
    mlp: int
    kv_blocks: int


def _tiles(seq, lp):
    return _Tiles(proj=_pick(lp, PROJ_TILES), merge=_pick(lp, MERGE_TILES), mlp=_pick(seq, MLP_TILES),
                  kv_blocks=min(ATTN_BIG_STEP_BLOCKS, lp // ATTN_KEY_BLOCK))


def kernel(x, meta_tokens, norm_mix_w, w_in, rwkv_mu, rwkv_w0, rwkv_w2, rwkv_a0, rwkv_a2, rwkv_g2, rwkv_k_k, rwkv_k_a, rwkv_r_k, rwkv_ln_w, rwkv_ln_b, rwkv_w_o, diff_lq1, diff_lk1, diff_lq2, diff_lk2, diff_subln_w, diff_w_o, w_out, norm_mlp_w, mlp_w1, mlp_w2, final_norm_w):
    bsz, seq, _ = x.shape
    seqlen = seq + N_META_TOKENS
    lp = -(-seqlen // SEQ_ALIGN) * SEQ_ALIGN
    tiles = _tiles(seq, lp)

    meta = jnp.broadcast_to(meta_tokens[None].astype(x.dtype), (bsz, N_META_TOKENS, D_MODEL))
    h_res = jnp.concatenate([meta, x, jnp.zeros((bsz, lp - seqlen, D_MODEL), x.dtype)], axis=1)
    hf = h_res.reshape(bsz * lp, D_MODEL)

    w = w_in.reshape(D_MODEL, -1).astype(BF16)
    c3 = 3 * D_MODEL
    w_rwkv = jnp.concatenate(
        [w[:, :c3], _pad_cols(w[:, c3:c3 + DECAY_LORA], LANE),
         _pad_cols(w[:, c3 + DECAY_LORA:c3 + DECAY_LORA + AAA_LORA], LANE),
         _pad_cols(w[:, c3 + DECAY_LORA + AAA_LORA:c3 + DECAY_LORA + AAA_LORA + GATE_LORA], GATE_LORA_PAD)],
        axis=1)
    rwkv_cols = c3 + DECAY_LORA + AAA_LORA + GATE_LORA
    w_diff = w[:, rwkv_cols:]
    mu = rwkv_mu[0][None, :]
    mu_p = jnp.concatenate(
        [mu[:, :c3], _pad_cols(mu[:, c3:c3 + DECAY_LORA], LANE),
         _pad_cols(mu[:, c3 + DECAY_LORA:c3 + DECAY_LORA + AAA_LORA], LANE),
         _pad_cols(mu[:, c3 + DECAY_LORA + AAA_LORA:], GATE_LORA_PAD)], axis=1)
    nw = norm_mix_w[0][None, :]

    tiles_per_seq = lp // tiles.proj
    head_id = jnp.arange(2 * PAIR) // HEAD_DIM
    ones_bd = (head_id[:, None] == head_id[None, :]).astype(BF16)
    vecs = (rwkv_w0[0][None, :], rwkv_a0[0][None, :], rwkv_k_k[0][None, :], rwkv_k_a[0][None, :],
            rwkv_r_k[0].reshape(1, D_MODEL))
    r, kmod, v, lw, kn, b, bonus, g, xn = _rwkv_proj(
        hf, nw, w_rwkv, mu_p, vecs, _pad_rows(rwkv_w2[0], LANE).astype(BF16),
        _pad_rows(rwkv_a2[0], LANE).astype(BF16), _pad_rows(rwkv_g2[0], GATE_LORA_PAD).astype(BF16),
        ones_bd, bsz=bsz, lp=lp, tm=tiles.proj)
    y = _rwkv_chunk(r, kmod, v, lw, kn, b)
    a5 = _in_proj(xn, w_diff, _rope_tables(lp), tm=tiles.proj, tiles_per_seq=tiles_per_seq).reshape(
        bsz, lp, 5 * D_MODEL)

    lams = (diff_lq1, diff_lk1, diff_lq2, diff_lk2)
    subw = diff_subln_w[0][None, :]
    kv_tiles = tiles.kv_blocks
    o_diff, row0 = None, 0
    for tq in (ATTN_Q_TILE, ATTN_KEY_BLOCK):
        n_tiles = (seqlen - row0) // tq
        if n_tiles:
            o_diff = _diff_attn(a5, lams, subw, tq=tq, n_tiles=n_tiles, row0=row0, kv_tiles=kv_tiles,
                                prev_out=o_diff)
            row0 += n_tiles * tq
    if row0 < lp:
        tail = max(-(-(seqlen - row0) // ATTN_TAIL_ROWS) * ATTN_TAIL_ROWS, ATTN_TAIL_ROWS)
        o_diff = _diff_attn(a5, lams, subw, tq=tail, n_tiles=1, row0=row0, kv_tiles=kv_tiles,
                            out_rows=lp - row0, prev_out=o_diff)

    h_mid = _merge(y, bonus, g, o_diff, a5, h_res, rwkv_ln_w[0][None, :], rwkv_ln_b[0][None, :],
                   ones_bd, rwkv_w_o[0].astype(BF16), diff_w_o[0].astype(BF16),
                   w_out[0].astype(BF16), tm=tiles.merge)
    return _mlp(h_mid, norm_mlp_w[0][None, :], mlp_w1[0].astype(BF16), mlp_w2[0].astype(BF16),
                final_norm_w[None, :], seq=seq, row0=N_META_TOKENS, tm=tiles.mlp)
```

```python
import functools
import math
from typing import NamedTuple

import jax
import jax.numpy as jnp
from jax import lax
from jax.experimental import pallas as pl
from jax.experimental.pallas import tpu as pltpu

F32 = jnp.float32
BF16 = jnp.bfloat16

D_MODEL = 1024
N_META_TOKENS = 16
HEAD_DIM = 64
PAIR = 2 * HEAD_DIM
N_PAIRS = D_MODEL // PAIR
DECAY_LORA = 64
AAA_LORA = 64
GATE_LORA = 160
RWKV_LN_EPS = 64e-5
ROPE_THETA = 500000.0
ROPE_DIM = HEAD_DIM // 4
ROPE_HALF = ROPE_DIM // 2
D_FF = 4 * D_MODEL
NORM_EPS = 1e-5
SUBLN_EPS = 1e-5
LAMBDA_INIT = 0.8 - 0.6 * math.exp(-0.3 * 0)

LANE = 128
SUBLANE = 8
PREV_ROWS = 16
CHUNK = 64
CHUNKS_PER_STEP = 2
PROJ_COL_CHUNK = 256
ATTN_KEY_BLOCK = 256
ATTN_Q_TILE = 512
ATTN_TAIL_ROWS = 16
ATTN_HEADS_PER_STEP = 4
ATTN_BIG_STEP_BLOCKS = 4
N_ATTN_INPUTS = 8
MLP_FF_CHUNKS = 4
PROJ_TILES = (768, 512, 256)
MERGE_TILES = (768, 384, 256)
MLP_TILES = (512, 256, 128)
Q_SCALE = HEAD_DIM ** -0.5 * math.log2(math.e)
SEQ_ALIGN = 256
VMEM_LIMIT = 56 * 1024 * 1024

LORA_W_OFF = 3 * D_MODEL
LORA_A_OFF = LORA_W_OFF + LANE
LORA_G_OFF = LORA_A_OFF + LANE
GATE_LORA_PAD = 2 * LANE


def _params(sem):
    return pltpu.CompilerParams(dimension_semantics=sem, vmem_limit_bytes=VMEM_LIMIT)


def _rms(x, w, eps):
    return x * lax.rsqrt(jnp.mean(x * x, axis=-1, keepdims=True) + eps) * w


def _sigmoid(x):
    return 1.0 / (1.0 + jnp.exp(-x))


def _dot(a, b):
    return jnp.dot(a.astype(BF16), b.astype(BF16), preferred_element_type=F32)


def _dot_nt(a, b):
    return lax.dot_general(a.astype(BF16), b.astype(BF16), (((1,), (1,)), ((), ())),
                           preferred_element_type=F32)


def _dot_tn(a, b):
    return lax.dot_general(a.astype(BF16), b.astype(BF16), (((0,), (0,)), ((), ())),
                           preferred_element_type=F32)


def _in_proj_kernel(xn_ref, w_ref, cos_ref, s1_ref, s2_ref, o_ref):
    n_cols = o_ref.shape[1]
    cc = PROJ_COL_CHUNK
    xn = xn_ref[...]
    reps = cc // LANE
    cos = jnp.tile(cos_ref[...], (1, reps))
    s1 = jnp.tile(s1_ref[...], (1, reps))
    s2 = jnp.tile(s2_ref[...], (1, reps))
    for c in range(n_cols // cc):
        cols = slice(c * cc, (c + 1) * cc)
        p = jnp.dot(xn, w_ref[:, cols], preferred_element_type=F32)
        if c * cc < 2 * D_MODEL:
            nxt = pltpu.roll(p, cc - ROPE_HALF, axis=1)
            prv = pltpu.roll(p, ROPE_HALF, axis=1)
            scale = Q_SCALE if c * cc < D_MODEL else 1.0
            p = (p * cos + nxt * s1 + prv * s2) * scale
        o_ref[:, cols] = p.astype(o_ref.dtype)


def _in_proj(xn, w, rope, *, tm, tiles_per_seq):
    t_tokens = xn.shape[0]
    n_cols = w.shape[1]
    tab_spec = pl.BlockSpec((tm, LANE), lambda i: (i % tiles_per_seq, 0))
    return pl.pallas_call(
        _in_proj_kernel,
        grid=(t_tokens // tm,),
        in_specs=[pl.BlockSpec((tm, D_MODEL), lambda i: (i, 0)),
                  pl.BlockSpec((D_MODEL, n_cols), lambda i: (0, 0), pipeline_mode=pl.Buffered(1)),
                  tab_spec, tab_spec, tab_spec],
        out_specs=pl.BlockSpec((tm, n_cols), lambda i: (i, 0)),
        out_shape=jax.ShapeDtypeStruct((t_tokens, n_cols), BF16),
        compiler_params=_params(("arbitrary",)),
        name="in_proj_diff",
    )(xn, w, *rope)


def _rwkv_proj_kernel(x_ref, xprev_ref, nw_ref, w_ref, mu_ref, w0_ref, a0_ref, kk_ref, ka_ref, rk_ref,
                      w2_ref, a2_ref, g2_ref, ones_ref, r_o, k_o, v_o, lw_o, kn_o, b_o, bonus_o, g_o, xn_o,
                      *, tiles_per_seq):
    tm = x_ref.shape[0]
    blk = 2 * PAIR
    xn = jnp.concatenate([_rms(x_ref[...], nw_ref[...], NORM_EPS).astype(BF16),
                          _rms(xprev_ref[...], nw_ref[...], NORM_EPS).astype(BF16)], axis=0)
    first = (pl.program_id(0) % tiles_per_seq) == 0
    row = lax.broadcasted_iota(jnp.int32, (tm, blk), 0)
    xn_o[...] = xn[:tm]

    def proj(col0):
        cols = slice(col0, col0 + blk)
        p = jnp.dot(xn, w_ref[:, cols], preferred_element_type=F32)
        prev_row = jnp.where(first, 0.0, p[tm + PREV_ROWS - 1:tm + PREV_ROWS, :])
        p = p[:tm]
        shifted = jnp.where(row == 0, prev_row, pltpu.roll(p, 1, axis=0))
        return p + (shifted - p) * mu_ref[:, cols]

    def rkv(c):
        return proj(c * blk), proj(D_MODEL + c * blk), proj(2 * D_MODEL + c * blk)

    p_wa = proj(LORA_W_OFF)
    tanh_wd = jnp.tanh(p_wa[:, :LANE]).astype(BF16)
    ad = p_wa[:, LANE:].astype(BF16)
    sig_gd = _sigmoid(proj(LORA_G_OFF)).astype(BF16)
    ones_bd = ones_ref[...]
    n_blocks = D_MODEL // blk
    nxt = rkv(0)
    for c in range(n_blocks):
        cols = slice(c * blk, (c + 1) * blk)
        r, k, v = nxt
        if c + 1 < n_blocks:
            nxt = rkv(c + 1)
        z = w0_ref[:, cols] + jnp.dot(tanh_wd, w2_ref[:, cols], preferred_element_type=F32)
        log_decay = -math.exp(-0.5) * _sigmoid(z)
        a_rate = _sigmoid(a0_ref[:, cols] + jnp.dot(ad, a2_ref[:, cols], preferred_element_type=F32))
        gate = jnp.dot(sig_gd, g2_ref[:, cols], preferred_element_type=F32)

        kraw = k * kk_ref[:, cols]
        sumsq = _dot(kraw * kraw, ones_bd)
        knorm = kraw * lax.rsqrt(jnp.maximum(sumsq, 1e-24))
        kmod = k * (1.0 + (a_rate - 1.0) * ka_ref[:, cols])
        bonus = _dot(r * kmod * rk_ref[:, cols], ones_bd) * v

        bonus_o[0, :, cols] = bonus.astype(BF16)
        g_o[0, :, cols] = gate.astype(BF16)
        r16, k16, v16 = r.astype(BF16), kmod.astype(BF16), v.astype(BF16)
        kn16, b16 = knorm.astype(BF16), (knorm * a_rate).astype(BF16)
        for h in range(2):
            pair = 2 * c + h
            sl = slice(h * PAIR, (h + 1) * PAIR)
            r_o[0, pair] = r16[:, sl]
            k_o[0, pair] = k16[:, sl]
            v_o[0, pair] = v16[:, sl]
            lw_o[0, pair] = log_decay[:, sl]
            kn_o[0, pair] = kn16[:, sl]
            b_o[0, pair] = b16[:, sl]


def _rwkv_proj(xf, nw, w, mu, vecs, w2p, a2p, g2p, ones_bd, *, bsz, lp, tm):
    tiles_per_seq = lp // tm
    n_cols = w.shape[1]
    blocks = tm // PREV_ROWS
    const = lambda shape: pl.BlockSpec(shape, lambda i: (0, 0))
    pair_spec = pl.BlockSpec((1, N_PAIRS, tm, PAIR), lambda i: (i // tiles_per_seq, 0, i % tiles_per_seq, 0))
    tok_spec = pl.BlockSpec((1, tm, D_MODEL), lambda i: (i // tiles_per_seq, i % tiles_per_seq, 0))
    pair_shape = jax.ShapeDtypeStruct((bsz, N_PAIRS, lp, PAIR), BF16)
    decay_shape = jax.ShapeDtypeStruct((bsz, N_PAIRS, lp, PAIR), F32)
    tok_shape = jax.ShapeDtypeStruct((bsz, lp, D_MODEL), BF16)
    return pl.pallas_call(
        functools.partial(_rwkv_proj_kernel, tiles_per_seq=tiles_per_seq),
        grid=(bsz * tiles_per_seq,),
        in_specs=[pl.BlockSpec((tm, D_MODEL), lambda i: (i, 0)),
                  pl.BlockSpec((PREV_ROWS, D_MODEL), lambda i: (jnp.maximum(i * blocks - 1, 0), 0)),
                  const((1, D_MODEL)),
                  pl.BlockSpec((D_MODEL, n_cols), lambda i: (0, 0), pipeline_mode=pl.Buffered(1)),
                  const((1, n_cols))]
        + [const((1, D_MODEL))] * 5
        + [const((LANE, D_MODEL)), const((LANE, D_MODEL)), const((GATE_LORA_PAD, D_MODEL)),
           const((2 * PAIR, 2 * PAIR))],
        out_specs=[pair_spec] * 6 + [tok_spec] * 2 + [pl.BlockSpec((tm, D_MODEL), lambda i: (i, 0))],
        out_shape=[pair_shape] * 3 + [decay_shape] + [pair_shape] * 2 + [tok_shape] * 2
        + [jax.ShapeDtypeStruct((bsz * lp, D_MODEL), BF16)],
        compiler_params=_params(("arbitrary",)),
        name="rwkv_proj",
    )(xf, xf, nw, w, mu, *vecs, w2p, a2p, g2p, ones_bd)


def _block_diag(x, lane_head0):
    return jnp.concatenate([jnp.where(lane_head0, x, 0.0), jnp.where(lane_head0, 0.0, x)], axis=0)


def _chunk_unit(r, k, v, lw, kn, b, get_state):
    n = 2 * CHUNK
    row = lax.broadcasted_iota(jnp.int32, (CHUNK, PAIR), 0)
    lane_head0 = lax.broadcasted_iota(jnp.int32, (CHUNK, PAIR), 1) < HEAD_DIM
    c = lw
    for s in (1, 2, 4, 8, 16, 32):
        c = c + jnp.where(row >= s, pltpu.roll(c, s, axis=0), 0.0)
    c_end = c[CHUNK - 1:CHUNK, :]
    e_neg = jnp.exp(-c)
    e_end = jnp.exp(c_end - c)
    a_t = _block_diag(-kn * jnp.exp(c - lw), lane_head0)
    r_t = _block_diag(r * jnp.exp(c), lane_head0)
    b_t = _block_diag(b * e_neg, lane_head0)
    k_t = _block_diag(k * e_neg, lane_head0)
    b_e = _block_diag(b * e_end, lane_head0)
    k_e = _block_diag(k * e_end, lane_head0)
    v_bd = _block_diag(v, lane_head0)

    yield
    g = _dot_nt(jnp.concatenate([a_t, r_t], axis=0), jnp.concatenate([b_t, k_t], axis=0))
    ri = lax.broadcasted_iota(jnp.int32, (n, n), 0)
    ci = lax.broadcasted_iota(jnp.int32, (n, n), 1)
    m_ab = jnp.where(ci < ri, g[:n, :n], 0.0)
    m_ak = jnp.where(ci < ri, g[:n, n:], 0.0)
    a_rb = jnp.where(ci <= ri, g[n:, :n], 0.0)
    a_rk = jnp.where(ci <= ri, g[n:, n:], 0.0)

    t_inv = jnp.where(ci == ri, 1.0, m_ab)
    yield
    power = _dot(m_ab, m_ab)
    w1 = _dot(m_ak, v_bd)
    for _ in range(4):
        yield
        prod = _dot(power, jnp.concatenate([power, t_inv], axis=1))
        power = prod[:, :n]
        t_inv = t_inv + prod[:, n:]
    yield
    t_inv = t_inv + _dot(power, t_inv)

    yield
    tw = _dot(t_inv, jnp.concatenate([w1, a_t], axis=1))
    u0 = tw[:, :n]
    a_hat = tw[:, n:]
    rhs = jnp.concatenate([jnp.concatenate([a_hat, u0], axis=1),
                           jnp.concatenate([jnp.zeros_like(v_bd), v_bd], axis=1)], axis=0)
    yield
    pq = _dot_tn(jnp.concatenate([b_e, k_e], axis=0), rhs)
    ry = _dot(jnp.concatenate([a_rb, a_rk], axis=1), rhs)
    decay_end = jnp.exp(jnp.broadcast_to(c_end, (n, n)))
    p_mat = pq[:, :n] + jnp.where(ci == ri, decay_end, 0.0)
    yield
    state = get_state()
    ys = _dot(jnp.concatenate([r_t + ry[:, :n], p_mat], axis=0), state)
    y_bd = ys[:n] + ry[:, n:]
    y = y_bd[:CHUNK] + y_bd[CHUNK:]
    new_state = ys[n:] + pq[:, n:]
    return y, new_state


def _interleave(generators, results):
    pending = list(range(len(generators)))
    while pending:
        still = []
        for idx in pending:
            try:
                next(generators[idx])
                still.append(idx)
            except StopIteration as stop:
                results[idx] = stop.value
        pending = still


def _rwkv_chunk_kernel(r_ref, k_ref, v_ref, lw_ref, kn_ref, b_ref, y_ref, state_scr, *, n_units):
    @pl.when(pl.program_id(0) == 0)
    def _():
        state_scr[...] = jnp.zeros_like(state_scr)

    jobs = []
    outs = [None] * (CHUNKS_PER_STEP * n_units)
    for sub in range(CHUNKS_PER_STEP):
        rows = slice(sub * CHUNK, (sub + 1) * CHUNK)
        for u in range(n_units):
            bi, pi = u // N_PAIRS, u % N_PAIRS
            if sub == 0:
                get_state = lambda u=u: state_scr[u]
            else:
                get_state = lambda u=u, sub=sub: outs[(sub - 1) * n_units + u][1]
            jobs.append(_chunk_unit(r_ref[bi, pi, rows].astype(F32), k_ref[bi, pi, rows].astype(F32),
                                    v_ref[bi, pi, rows].astype(F32), lw_ref[bi, pi, rows],
                                    kn_ref[bi, pi, rows].astype(F32), b_ref[bi, pi, rows].astype(F32),
                                    get_state))
    _interleave(jobs, outs)
    for sub in range(CHUNKS_PER_STEP):
        rows = slice(sub * CHUNK, (sub + 1) * CHUNK)
        for u in range(n_units):
            bi, pi = u // N_PAIRS, u % N_PAIRS
            y, new_state = outs[sub * n_units + u]
            y_ref[bi, pi, rows] = y.astype(BF16)
            if sub == CHUNKS_PER_STEP - 1:
                state_scr[u] = new_state


def _rwkv_chunk(r, k, v, lw, kn, b):
    bsz, _, lp, _ = r.shape
    n_units = bsz * N_PAIRS
    rows = CHUNKS_PER_STEP * CHUNK
    spec = pl.BlockSpec((bsz, N_PAIRS, rows, PAIR), lambda c: (0, 0, c, 0))
    return pl.pallas_call(
        functools.partial(_rwkv_chunk_kernel, n_units=n_units),
        grid=(lp // rows,),
        in_specs=[spec] * 6,
        out_specs=spec,
        out_shape=jax.ShapeDtypeStruct(r.shape, BF16),
        scratch_shapes=[pltpu.VMEM((n_units, PAIR, PAIR), F32)],
        compiler_params=_params(("arbitrary",)),
        name="rwkv_chunk",
    )(r, k, v, lw, kn, b)


def _diff_attn_kernel(*refs, n_tiles, aliased, **tile_params):
    o_ref = refs[N_ATTN_INPUTS + 1] if aliased else refs[N_ATTN_INPUTS]
    tile = pl.program_id(2)

    @pl.when(tile < n_tiles)
    def _():
        _diff_attn_tile(*refs, aliased=aliased, **tile_params)

    @pl.when(tile >= n_tiles)
    def _():
        o_ref[...] = jnp.zeros_like(o_ref)


def _diff_attn_tile(q_ref, k_ref, v_ref, lq1_ref, lk1_ref, lq2_ref, lk2_ref, subw_ref,
                    *rest, tq, kv_tiles, block0, aliased):
    o_ref, m_scr, acc_scr, s0_scr = rest[1:] if aliased else rest
    kb = ATTN_KEY_BLOCK
    diag = block0 + (pl.program_id(2) * tq) // kb
    head0 = lax.broadcasted_iota(jnp.int32, (tq, PAIR), 1) < HEAD_DIM
    qq = []
    for t in range(ATTN_HEADS_PER_STEP):
        q = q_ref[0, :, t * PAIR:(t + 1) * PAIR]
        zero = jnp.zeros_like(q)
        qq.append(jnp.concatenate([jnp.where(head0, q, zero), jnp.where(head0, zero, q)], axis=0))

    def scores(t, j, width):
        start = pl.multiple_of(j * kb, kb)
        return _dot_nt(qq[t], k_ref[0, pl.ds(start, width), t * PAIR:(t + 1) * PAIR])

    def step(j, width, masked, s_first=None, emit_next_first=None):
        start = pl.multiple_of(j * kb, kb)
        ones = jnp.ones((width, PAIR), BF16)
        s_next = scores(0, j, width) if s_first is None else s_first
        for t in range(ATTN_HEADS_PER_STEP):
            s = s_next
            if t + 1 < ATTN_HEADS_PER_STEP:
                s_next = scores(t + 1, j, width)
            elif emit_next_first is not None:
                emit_next_first()
            if masked:
                rr = lax.broadcasted_iota(jnp.int32, (2 * tq, width), 0) % tq
                cc = lax.broadcasted_iota(jnp.int32, (2 * tq, width), 1)
                s = jnp.where(cc <= rr, s, -1e30)
            row_max = jnp.max(s, axis=1, keepdims=True)
            if masked:
                m_next = jnp.broadcast_to(row_max, (2 * tq, PAIR))
            else:
                m_prev = m_scr[t]
                m_next = jnp.maximum(m_prev, row_max)
            p = jnp.exp2(s - jnp.tile(m_next, (1, width // LANE))).astype(BF16)
            v_ext = jnp.concatenate([v_ref[0, pl.ds(start, width), t * PAIR:(t + 1) * PAIR], ones], axis=1)
            pv = jnp.dot(p, v_ext, preferred_element_type=F32)
            if masked:
                acc_scr[t] = pv
            else:
                acc_scr[t] = jnp.tile(jnp.exp2(m_prev - m_next), (1, 2)) * acc_scr[t] + pv
            m_scr[t] = m_next

    n_big = diag // kv_tiles
    big = kv_tiles * kb

    def big_body(j, carry):
        def emit_next_first():
            j_next = jnp.minimum(j + 1, n_big - 1)
            s0_scr[...] = scores(0, j_next * kv_tiles, big)

        step(j * kv_tiles, big, False, s_first=s0_scr[...], emit_next_first=emit_next_first)
        return carry

    rest_blocks = max(tq // kb, 1)

    def rest_body(j, carry):
        step(n_big * kv_tiles + j * rest_blocks, rest_blocks * kb, False)
        return carry

    def emit_first_big():
        s0_scr[...] = scores(0, 0, big)

    step(diag, max(tq, kb), True, emit_next_first=emit_first_big)
    lax.fori_loop(0, n_big, big_body, 0)
    lax.fori_loop(0, (diag - n_big * kv_tiles) // rest_blocks, rest_body, 0)

    lam = (jnp.exp(jnp.sum(lq1_ref[...] * lk1_ref[...], axis=1, keepdims=True))
           - jnp.exp(jnp.sum(lq2_ref[...] * lk2_ref[...], axis=1, keepdims=True)) + LAMBDA_INIT)
    for t in range(ATTN_HEADS_PER_STEP):
        acc = acc_scr[t]
        o = acc[:, :PAIR] / acc[:, PAIR:]
        od = o[:tq] - lam * o[tq:]
        o_ref[0, :tq, t * PAIR:(t + 1) * PAIR] = (
            _rms(od, subw_ref[...], SUBLN_EPS) * (1.0 - LAMBDA_INIT)).astype(o_ref.dtype)
    if o_ref.shape[1] > tq:
        o_ref[0, tq:, :] = jnp.zeros((o_ref.shape[1] - tq, o_ref.shape[2]), o_ref.dtype)


def _diff_attn(a5, lams, subw, *, tq, n_tiles, row0, kv_tiles, out_rows=None, prev_out=None):
    bsz, lp, _ = a5.shape
    width = ATTN_HEADS_PER_STEP * PAIR
    n_groups = D_MODEL // width
    out_rows = tq if out_rows is None else out_rows
    fill_steps = 0 if prev_out is not None else -(-(lp - row0 - n_tiles * tq) // out_rows)
    assert row0 % tq == 0 and row0 % out_rows == 0 and row0 % ATTN_KEY_BLOCK == 0
    assert tq % ATTN_KEY_BLOCK == 0 or n_tiles == 1
    q_spec = pl.BlockSpec((1, tq, width), lambda b, g, i: (b, row0 // tq + i, g))
    k_spec = pl.BlockSpec((1, lp, width), lambda b, g, i: (b, 0, n_groups + g))
    v_spec = pl.BlockSpec((1, lp, width), lambda b, g, i: (b, 0, 2 * n_groups + g))
    lam_spec = pl.BlockSpec((1, HEAD_DIM), lambda b, g, i: (0, 0))
    in_specs = [q_spec, k_spec, v_spec, lam_spec, lam_spec, lam_spec, lam_spec,
                pl.BlockSpec((1, PAIR), lambda b, g, i: (0, 0))]
    args = (a5, a5, a5, *lams, subw)
    aliases = {}
    if prev_out is not None:
        aliases = {len(args): 0}
        in_specs.append(pl.BlockSpec(memory_space=pl.ANY))
        args = args + (prev_out,)
    return pl.pallas_call(
        functools.partial(_diff_attn_kernel, n_tiles=n_tiles, tq=tq, kv_tiles=kv_tiles,
                          block0=row0 // ATTN_KEY_BLOCK, aliased=prev_out is not None),
        grid=(bsz, n_groups, n_tiles + fill_steps),
        in_specs=in_specs,
        out_specs=pl.BlockSpec((1, out_rows, width), lambda b, g, i: (b, row0 // out_rows + i, g)),
        out_shape=jax.ShapeDtypeStruct((bsz, lp, D_MODEL), BF16),
        input_output_aliases=aliases,
        scratch_shapes=[pltpu.VMEM((ATTN_HEADS_PER_STEP, 2 * tq, PAIR), F32),
                        pltpu.VMEM((ATTN_HEADS_PER_STEP, 2 * tq, 2 * PAIR), F32),
                        pltpu.VMEM((2 * tq, kv_tiles * ATTN_KEY_BLOCK), F32)],
        compiler_params=_params(("arbitrary", "arbitrary", "arbitrary")),
        name=f"diff_attn_q{tq}",
    )(*args)


def _merge_kernel(y_ref, bonus_ref, g_ref, od_ref, gate_r_ref, gate_d_ref, res_ref, lnw_ref, lnb_ref,
                  ones_ref, wo_r_ref, wo_d_ref, wout_ref, o_ref):
    ones_bd = ones_ref[...]
    parts = []
    for pair in range(0, N_PAIRS, 2):
        y = jnp.concatenate([y_ref[0, pair], y_ref[0, pair + 1]], axis=1).astype(F32)
        mean = _dot(y, ones_bd) * (1.0 / HEAD_DIM)
        yc = y - mean
        var = _dot(yc * yc, ones_bd) * (1.0 / HEAD_DIM)
        parts.append(yc * lax.rsqrt(var + RWKV_LN_EPS))
    yn = jnp.concatenate(parts, axis=1)
    o_rwkv = (yn * lnw_ref[...] + lnb_ref[...] + bonus_ref[0].astype(F32)) * g_ref[0].astype(F32)
    br_rwkv = _dot(o_rwkv, wo_r_ref[...])
    br_diff = jnp.dot(od_ref[0], wo_d_ref[...], preferred_element_type=F32)
    merged = (_sigmoid(gate_r_ref[0].astype(F32)) * br_rwkv
              + _sigmoid(gate_d_ref[0].astype(F32)) * br_diff)
    o_ref[0] = res_ref[0] + _dot(merged, wout_ref[...])


def _merge(y, bonus, g, od, a5, res, lnw, lnb, ones_pair, wo_r, wo_d, wout, *, tm):
    bsz, lp, _ = res.shape
    grid = (bsz, lp // tm)
    tok = lambda c: pl.BlockSpec((1, tm, c), lambda b, i: (b, i, 0))
    vec = pl.BlockSpec((1, D_MODEL), lambda b, i: (0, 0))
    mat = pl.BlockSpec((D_MODEL, D_MODEL), lambda b, i: (0, 0))
    gate_r_spec = pl.BlockSpec((1, tm, D_MODEL), lambda b, i: (b, i, 3))
    gate_d_spec = pl.BlockSpec((1, tm, D_MODEL), lambda b, i: (b, i, 4))
    return pl.pallas_call(
        _merge_kernel,
        grid=grid,
        in_specs=[pl.BlockSpec((1, N_PAIRS, tm, PAIR), lambda b, i: (b, 0, i, 0)),
                  tok(D_MODEL), tok(D_MODEL), tok(D_MODEL), gate_r_spec, gate_d_spec, tok(D_MODEL), vec, vec,
                  pl.BlockSpec((2 * PAIR, 2 * PAIR), lambda b, i: (0, 0)), mat, mat, mat],
        out_specs=tok(D_MODEL),
        out_shape=jax.ShapeDtypeStruct(res.shape, F32),
        compiler_params=_params(("arbitrary", "arbitrary")),
        name="merge",
    )(y, bonus, g, od, a5, a5, res, lnw, lnb, ones_pair, wo_r, wo_d, wout)


def _mlp_kernel(x_ref, nw_ref, w1_ref, w2_ref, fw_ref, o_ref, *, n_ff_chunks):
    x = x_ref[...]
    h = _rms(x, nw_ref[...], NORM_EPS).astype(BF16)
    acc = x
    ff = D_FF // n_ff_chunks
    for c in range(n_ff_chunks):
        z = jnp.dot(h, w1_ref[:, c * ff:(c + 1) * ff], preferred_element_type=F32)
        act = jnp.square(jnp.maximum(z, 0.0))
        acc = acc + _dot(act, w2_ref[c * ff:(c + 1) * ff, :])
    o_ref[...] = _rms(acc, fw_ref[...], NORM_EPS)


def _mlp(x3, nw, w1, w2, fw, *, seq, row0, tm):
    bsz = x3.shape[0]
    x_spec = pl.BlockSpec((None, pl.Element(tm), pl.Element(D_MODEL)), lambda b, i: (b, pl.multiple_of(row0 + i * tm, SUBLANE), 0))
    vec = pl.BlockSpec((1, D_MODEL), lambda b, i: (0, 0))
    return pl.pallas_call(
        functools.partial(_mlp_kernel, n_ff_chunks=MLP_FF_CHUNKS),
        grid=(bsz, seq // tm),
        in_specs=[x_spec, vec, pl.BlockSpec((D_MODEL, D_FF), lambda b, i: (0, 0)),
                  pl.BlockSpec((D_FF, D_MODEL), lambda b, i: (0, 0)), vec],
        out_specs=pl.BlockSpec((None, tm, D_MODEL), lambda b, i: (b, i, 0)),
        out_shape=jax.ShapeDtypeStruct((bsz, seq, D_MODEL), F32),
        compiler_params=_params(("arbitrary", "arbitrary")),
        name="mlp",
    )(x3, nw, w1, w2, fw)


def _pad_cols(a, width):
    return jnp.pad(a, ((0, 0), (0, width - a.shape[1])))


def _pad_rows(a, height):
    return jnp.pad(a, ((0, height - a.shape[0]), (0, 0)))


def _rope_tables(lp):
    inv = ROPE_THETA ** (-jnp.arange(0, ROPE_DIM, 2, dtype=F32) / ROPE_DIM)
    ang = jnp.arange(lp, dtype=F32)[:, None] * inv[None, :]
    cos, sin = jnp.cos(ang), jnp.sin(ang)
    rest = HEAD_DIM - ROPE_DIM
    ones = jnp.ones((lp, rest), F32)
    zeros = jnp.zeros((lp, rest), F32)
    zh = jnp.zeros_like(sin)
    c = jnp.concatenate([cos, cos, ones], axis=1)
    s1 = jnp.concatenate([-sin, zh, zeros], axis=1)
    s2 = jnp.concatenate([zh, sin, zeros], axis=1)
    return tuple(jnp.tile(t, (1, LANE // HEAD_DIM)) for t in (c, s1, s2))


def _pick(n, candidates):
    for c in candidates:
        if n % c == 0:
            return c
    raise ValueError(n)


class _Tiles(NamedTuple):
    proj: int
    merge: int
    mlp: int
    kv_blocks: int


def _tiles(seq, lp):
    return _Tiles(proj=_pick(lp, PROJ_TILES), merge=_pick(lp, MERGE_TILES), mlp=_pick(seq, MLP_TILES),
                  kv_blocks=min(ATTN_BIG_STEP_BLOCKS, lp // ATTN_KEY_BLOCK))


def kernel(x, meta_tokens, norm_mix_w, w_in, rwkv_mu, rwkv_w0, rwkv_w2, rwkv_a0, rwkv_a2, rwkv_g2, rwkv_k_k, rwkv_k_a, rwkv_r_k, rwkv_ln_w, rwkv_ln_b, rwkv_w_o, diff_lq1, diff_lk1, diff_lq2, diff_lk2, diff_subln_w, diff_w_o, w_out, norm_mlp_w, mlp_w1, mlp_w2, final_norm_w):
    bsz, seq, _ = x.shape
    seqlen = seq + N_META_TOKENS
    lp = -(-seqlen // SEQ_ALIGN) * SEQ_ALIGN
    tiles = _tiles(seq, lp)

    meta = jnp.broadcast_to(meta_tokens[None].astype(x.dtype), (bsz, N_META_TOKENS, D_MODEL))
    h_res = jnp.concatenate([meta, x, jnp.zeros((bsz, lp - seqlen, D_MODEL), x.dtype)], axis=1)
    hf = h_res.reshape(bsz * lp, D_MODEL)

    w = w_in.reshape(D_MODEL, -1).astype(BF16)
    c3 = 3 * D_MODEL
    w_rwkv = jnp.concatenate(
        [w[:, :c3], _pad_cols(w[:, c3:c3 + DECAY_LORA], LANE),
         _pad_cols(w[:, c3 + DECAY_LORA:c3 + DECAY_LORA + AAA_LORA], LANE),
         _pad_cols(w[:, c3 + DECAY_LORA + AAA_LORA:c3 + DECAY_LORA + AAA_LORA + GATE_LORA], GATE_LORA_PAD)],
        axis=1)
    rwkv_cols = c3 + DECAY_LORA + AAA_LORA + GATE_LORA
    w_diff = w[:, rwkv_cols:]
    mu = rwkv_mu[0][None, :]
    mu_p = jnp.concatenate(
        [mu[:, :c3], _pad_cols(mu[:, c3:c3 + DECAY_LORA], LANE),
         _pad_cols(mu[:, c3 + DECAY_LORA:c3 + DECAY_LORA + AAA_LORA], LANE),
         _pad_cols(mu[:, c3 + DECAY_LORA + AAA_LORA:], GATE_LORA_PAD)], axis=1)
    nw = norm_mix_w[0][None, :]

    tiles_per_seq = lp // tiles.proj
    head_id = jnp.arange(2 * PAIR) // HEAD_DIM
    ones_bd = (head_id[:, None] == head_id[None, :]).astype(BF16)
    vecs = (rwkv_w0[0][None, :], rwkv_a0[0][None, :], rwkv_k_k[0][None, :], rwkv_k_a[0][None, :],
            rwkv_r_k[0].reshape(1, D_MODEL))
    r, kmod, v, lw, kn, b, bonus, g, xn = _rwkv_proj(
        hf, nw, w_rwkv, mu_p, vecs, _pad_rows(rwkv_w2[0], LANE).astype(BF16),
        _pad_rows(rwkv_a2[0], LANE).astype(BF16), _pad_rows(rwkv_g2[0], GATE_LORA_PAD).astype(BF16),
        ones_bd, bsz=bsz, lp=lp, tm=tiles.proj)
    y = _rwkv_chunk(r, kmod, v, lw, kn, b)
    a5 = _in_proj(xn, w_diff, _rope_tables(lp), tm=tiles.proj, tiles_per_seq=tiles_per_seq).reshape(
        bsz, lp, 5 * D_MODEL)

    lams = (diff_lq1, diff_lk1, diff_lq2, diff_lk2)
    subw = diff_subln_w[0][None, :]
    kv_tiles = tiles.kv_blocks
    o_diff, row0 = None, 0
    for tq in (ATTN_Q_TILE, ATTN_KEY_BLOCK):
        n_tiles = (seqlen - row0) // tq
        if n_tiles:
            o_diff = _diff_attn(a5, lams, subw, tq=tq, n_tiles=n_tiles, row0=row0, kv_tiles=kv_tiles,
                                prev_out=o_diff)
            row0 += n_tiles * tq
    if row0 < lp:
        tail = max(-(-(seqlen - row0) // ATTN_TAIL_ROWS) * ATTN_TAIL_ROWS, ATTN_TAIL_ROWS)
        o_diff = _diff_attn(a5, lams, subw, tq=tail, n_tiles=1, row0=row0, kv_tiles=kv_tiles,
                            out_rows=lp - row0, prev_out=o_diff)

    h_mid = _merge(y, bonus, g, o_diff, a5, h_res, rwkv_ln_w[0][None, :], rwkv_ln_b[0][None, :],
                   ones_bd, rwkv_w_o[0].astype(BF16), diff_w_o[0].astype(BF16),
                   w_out[0].astype(BF16), tm=tiles.merge)
    return _mlp(h_mid, norm_mlp_w[0][None, :], mlp_w1[0].astype(BF16), mlp_w2[0].astype(BF16),
                final_norm_w[None, :], seq=seq, row0=N_META_TOKENS, tm=tiles.mlp)
```

```python
import functools
import math
from typing import NamedTuple

import jax
import jax.numpy as jnp
from jax import lax
from jax.experimental import pallas as pl
from jax.experimental.pallas import tpu as pltpu

F32 = jnp.float32
BF16 = jnp.bfloat16

D_MODEL = 1024
N_META_TOKENS = 16
HEAD_DIM = 64
PAIR = 2 * HEAD_DIM
N_PAIRS = D_MODEL // PAIR
DECAY_LORA = 64
AAA_LORA = 64
GATE_LORA = 160
RWKV_LN_EPS = 64e-5
ROPE_THETA = 500000.0
ROPE_DIM = HEAD_DIM // 4
ROPE_HALF = ROPE_DIM // 2
D_FF = 4 * D_MODEL
NORM_EPS = 1e-5
SUBLN_EPS = 1e-5
LAMBDA_INIT = 0.8 - 0.6 * math.exp(-0.3 * 0)

LANE = 128
SUBLANE = 8
PREV_ROWS = 16
CHUNK = 64
CHUNKS_PER_STEP = 2
PROJ_COL_CHUNK = 256
ATTN_KEY_BLOCK = 256
ATTN_Q_TILE = 512
ATTN_TAIL_ROWS = 16
ATTN_HEADS_PER_STEP = 4
ATTN_BIG_STEP_BLOCKS = 4
MLP_FF_CHUNKS = 4
PROJ_TILES = (768, 512, 256)
MERGE_TILES = (768, 384, 256)
MLP_TILES = (512, 256, 128)
Q_SCALE = HEAD_DIM ** -0.5 * math.log2(math.e)
SEQ_ALIGN = 256
VMEM_LIMIT = 56 * 1024 * 1024

LORA_W_OFF = 3 * D_MODEL
LORA_A_OFF = LORA_W_OFF + LANE
LORA_G_OFF = LORA_A_OFF + LANE
GATE_LORA_PAD = 2 * LANE


def _params(sem):
    return pltpu.CompilerParams(dimension_semantics=sem, vmem_limit_bytes=VMEM_LIMIT)


def _rms(x, w, eps):
    return x * lax.rsqrt(jnp.mean(x * x, axis=-1, keepdims=True) + eps) * w


def _sigmoid(x):
    return 1.0 / (1.0 + jnp.exp(-x))


def _dot(a, b):
    return jnp.dot(a.astype(BF16), b.astype(BF16), preferred_element_type=F32)


def _dot_nt(a, b):
    return lax.dot_general(a.astype(BF16), b.astype(BF16), (((1,), (1,)), ((), ())),
                           preferred_element_type=F32)


def _dot_tn(a, b):
    return lax.dot_general(a.astype(BF16), b.astype(BF16), (((0,), (0,)), ((), ())),
                           preferred_element_type=F32)


def _in_proj_kernel(xn_ref, w_ref, cos_ref, s1_ref, s2_ref, o_ref):
    n_cols = o_ref.shape[1]
    cc = PROJ_COL_CHUNK
    xn = xn_ref[...]
    reps = cc // LANE
    cos = jnp.tile(cos_ref[...], (1, reps))
    s1 = jnp.tile(s1_ref[...], (1, reps))
    s2 = jnp.tile(s2_ref[...], (1, reps))
    for c in range(n_cols // cc):
        cols = slice(c * cc, (c + 1) * cc)
        p = jnp.dot(xn, w_ref[:, cols], preferred_element_type=F32)
        if c * cc < 2 * D_MODEL:
            nxt = pltpu.roll(p, cc - ROPE_HALF, axis=1)
            prv = pltpu.roll(p, ROPE_HALF, axis=1)
            scale = Q_SCALE if c * cc < D_MODEL else 1.0
            p = (p * cos + nxt * s1 + prv * s2) * scale
        o_ref[:, cols] = p.astype(o_ref.dtype)


def _in_proj(xn, w, rope, *, tm, tiles_per_seq):
    t_tokens = xn.shape[0]
    n_cols = w.shape[1]
    tab_spec = pl.BlockSpec((tm, LANE), lambda i: (i % tiles_per_seq, 0))
    return pl.pallas_call(
        _in_proj_kernel,
        grid=(t_tokens // tm,),
        in_specs=[pl.BlockSpec((tm, D_MODEL), lambda i: (i, 0)),
                  pl.BlockSpec((D_MODEL, n_cols), lambda i: (0, 0), pipeline_mode=pl.Buffered(1)),
                  tab_spec, tab_spec, tab_spec],
        out_specs=pl.BlockSpec((tm, n_cols), lambda i: (i, 0)),
        out_shape=jax.ShapeDtypeStruct((t_tokens, n_cols), BF16),
        compiler_params=_params(("arbitrary",)),
        name="in_proj_diff",
    )(xn, w, *rope)


def _rwkv_proj_kernel(x_ref, xprev_ref, nw_ref, w_ref, mu_ref, w0_ref, a0_ref, kk_ref, ka_ref, rk_ref,
                      w2_ref, a2_ref, g2_ref, ones_ref, r_o, k_o, v_o, lw_o, kn_o, b_o, bonus_o, g_o, xn_o,
                      *, tiles_per_seq):
    tm = x_ref.shape[0]
    blk = 2 * PAIR
    xn = jnp.concatenate([_rms(x_ref[...], nw_ref[...], NORM_EPS).astype(BF16),
                          _rms(xprev_ref[...], nw_ref[...], NORM_EPS).astype(BF16)], axis=0)
    first = (pl.program_id(0) % tiles_per_seq) == 0
    row = lax.broadcasted_iota(jnp.int32, (tm, blk), 0)
    xn_o[...] = xn[:tm]

    def proj(col0):
        cols = slice(col0, col0 + blk)
        p = jnp.dot(xn, w_ref[:, cols], preferred_element_type=F32)
        prev_row = jnp.where(first, 0.0, p[tm + PREV_ROWS - 1:tm + PREV_ROWS, :])
        p = p[:tm]
        shifted = jnp.where(row == 0, prev_row, pltpu.roll(p, 1, axis=0))
        return p + (shifted - p) * mu_ref[:, cols]

    def rkv(c):
        return proj(c * blk), proj(D_MODEL + c * blk), proj(2 * D_MODEL + c * blk)

    p_wa = proj(LORA_W_OFF)
    tanh_wd = jnp.tanh(p_wa[:, :LANE]).astype(BF16)
    ad = p_wa[:, LANE:].astype(BF16)
    sig_gd = _sigmoid(proj(LORA_G_OFF)).astype(BF16)
    ones_bd = ones_ref[...]
    n_blocks = D_MODEL // blk
    nxt = rkv(0)
    for c in range(n_blocks):
        cols = slice(c * blk, (c + 1) * blk)
        r, k, v = nxt
        if c + 1 < n_blocks:
            nxt = rkv(c + 1)
        z = w0_ref[:, cols] + jnp.dot(tanh_wd, w2_ref[:, cols], preferred_element_type=F32)
        log_decay = -math.exp(-0.5) * _sigmoid(z)
        a_rate = _sigmoid(a0_ref[:, cols] + jnp.dot(ad, a2_ref[:, cols], preferred_element_type=F32))
        gate = jnp.dot(sig_gd, g2_ref[:, cols], preferred_element_type=F32)

        kraw = k * kk_ref[:, cols]
        sumsq = _dot(kraw * kraw, ones_bd)
        knorm = kraw * lax.rsqrt(jnp.maximum(sumsq, 1e-24))
        kmod = k * (1.0 + (a_rate - 1.0) * ka_ref[:, cols])
        bonus = _dot(r * kmod * rk_ref[:, cols], ones_bd) * v

        bonus_o[0, :, cols] = bonus.astype(BF16)
        g_o[0, :, cols] = gate.astype(BF16)
        r16, k16, v16 = r.astype(BF16), kmod.astype(BF16), v.astype(BF16)
        kn16, b16 = knorm.astype(BF16), (knorm * a_rate).astype(BF16)
        for h in range(2):
            pair = 2 * c + h
            sl = slice(h * PAIR, (h + 1) * PAIR)
            r_o[0, pair] = r16[:, sl]
            k_o[0, pair] = k16[:, sl]
            v_o[0, pair] = v16[:, sl]
            lw_o[0, pair] = log_decay[:, sl]
            kn_o[0, pair] = kn16[:, sl]
            b_o[0, pair] = b16[:, sl]


def _rwkv_proj(xf, nw, w, mu, vecs, w2p, a2p, g2p, ones_bd, *, bsz, lp, tm):
    tiles_per_seq = lp // tm
    n_cols = w.shape[1]
    blocks = tm // PREV_ROWS
    const = lambda shape: pl.BlockSpec(shape, lambda i: (0, 0))
    pair_spec = pl.BlockSpec((1, N_PAIRS, tm, PAIR), lambda i: (i // tiles_per_seq, 0, i % tiles_per_seq, 0))
    tok_spec = pl.BlockSpec((1, tm, D_MODEL), lambda i: (i // tiles_per_seq, i % tiles_per_seq, 0))
    pair_shape = jax.ShapeDtypeStruct((bsz, N_PAIRS, lp, PAIR), BF16)
    decay_shape = jax.ShapeDtypeStruct((bsz, N_PAIRS, lp, PAIR), F32)
    tok_shape = jax.ShapeDtypeStruct((bsz, lp, D_MODEL), BF16)
    return pl.pallas_call(
        functools.partial(_rwkv_proj_kernel, tiles_per_seq=tiles_per_seq),
        grid=(bsz * tiles_per_seq,),
        in_specs=[pl.BlockSpec((tm, D_MODEL), lambda i: (i, 0)),
                  pl.BlockSpec((PREV_ROWS, D_MODEL), lambda i: (jnp.maximum(i * blocks - 1, 0), 0)),
                  const((1, D_MODEL)),
                  pl.BlockSpec((D_MODEL, n_cols), lambda i: (0, 0), pipeline_mode=pl.Buffered(1)),
                  const((1, n_cols))]
        + [const((1, D_MODEL))] * 5
        + [const((LANE, D_MODEL)), const((LANE, D_MODEL)), const((GATE_LORA_PAD, D_MODEL)),
           const((2 * PAIR, 2 * PAIR))],
        out_specs=[pair_spec] * 6 + [tok_spec] * 2 + [pl.BlockSpec((tm, D_MODEL), lambda i: (i, 0))],
        out_shape=[pair_shape] * 3 + [decay_shape] + [pair_shape] * 2 + [tok_shape] * 2
        + [jax.ShapeDtypeStruct((bsz * lp, D_MODEL), BF16)],
        compiler_params=_params(("arbitrary",)),
        name="rwkv_proj",
    )(xf, xf, nw, w, mu, *vecs, w2p, a2p, g2p, ones_bd)


def _block_diag(x, lane_head0):
    return jnp.concatenate([jnp.where(lane_head0, x, 0.0), jnp.where(lane_head0, 0.0, x)], axis=0)


def _chunk_unit(r, k, v, lw, kn, b, get_state):
    n = 2 * CHUNK
    row = lax.broadcasted_iota(jnp.int32, (CHUNK, PAIR), 0)
    lane_head0 = lax.broadcasted_iota(jnp.int32, (CHUNK, PAIR), 1) < HEAD_DIM
    c = lw
    for s in (1, 2, 4, 8, 16, 32):
        c = c + jnp.where(row >= s, pltpu.roll(c, s, axis=0), 0.0)
    c_end = c[CHUNK - 1:CHUNK, :]
    e_neg = jnp.exp(-c)
    e_end = jnp.exp(c_end - c)
    a_t = _block_diag(-kn * jnp.exp(c - lw), lane_head0)
    r_t = _block_diag(r * jnp.exp(c), lane_head0)
    b_t = _block_diag(b * e_neg, lane_head0)
    k_t = _block_diag(k * e_neg, lane_head0)
    b_e = _block_diag(b * e_end, lane_head0)
    k_e = _block_diag(k * e_end, lane_head0)
    v_bd = _block_diag(v, lane_head0)

    yield
    g = _dot_nt(jnp.concatenate([a_t, r_t], axis=0), jnp.concatenate([b_t, k_t], axis=0))
    ri = lax.broadcasted_iota(jnp.int32, (n, n), 0)
    ci = lax.broadcasted_iota(jnp.int32, (n, n), 1)
    m_ab = jnp.where(ci < ri, g[:n, :n], 0.0)
    m_ak = jnp.where(ci < ri, g[:n, n:], 0.0)
    a_rb = jnp.where(ci <= ri, g[n:, :n], 0.0)
    a_rk = jnp.where(ci <= ri, g[n:, n:], 0.0)

    t_inv = jnp.where(ci == ri, 1.0, m_ab)
    yield
    power = _dot(m_ab, m_ab)
    w1 = _dot(m_ak, v_bd)
    for _ in range(4):
        yield
        prod = _dot(power, jnp.concatenate([power, t_inv], axis=1))
        power = prod[:, :n]
        t_inv = t_inv + prod[:, n:]
    yield
    t_inv = t_inv + _dot(power, t_inv)

    yield
    tw = _dot(t_inv, jnp.concatenate([w1, a_t], axis=1))
    u0 = tw[:, :n]
    a_hat = tw[:, n:]
    rhs = jnp.concatenate([jnp.concatenate([a_hat, u0], axis=1),
                           jnp.concatenate([jnp.zeros_like(v_bd), v_bd], axis=1)], axis=0)
    yield
    pq = _dot_tn(jnp.concatenate([b_e, k_e], axis=0), rhs)
    ry = _dot(jnp.concatenate([a_rb, a_rk], axis=1), rhs)
    decay_end = jnp.exp(jnp.broadcast_to(c_end, (n, n)))
    p_mat = pq[:, :n] + jnp.where(ci == ri, decay_end, 0.0)
    yield
    state = get_state()
    ys = _dot(jnp.concatenate([r_t + ry[:, :n], p_mat], axis=0), state)
    y_bd = ys[:n] + ry[:, n:]
    y = y_bd[:CHUNK] + y_bd[CHUNK:]
    new_state = ys[n:] + pq[:, n:]
    return y, new_state


def _interleave(generators, results):
    pending = list(range(len(generators)))
    while pending:
        still = []
        for idx in pending:
            try:
                next(generators[idx])
                still.append(idx)
            except StopIteration as stop:
                results[idx] = stop.value
        pending = still


def _rwkv_chunk_kernel(r_ref, k_ref, v_ref, lw_ref, kn_ref, b_ref, y_ref, state_scr, *, n_units):
    @pl.when(pl.program_id(0) == 0)
    def _():
        state_scr[...] = jnp.zeros_like(state_scr)

    jobs = []
    outs = [None] * (CHUNKS_PER_STEP * n_units)
    for sub in range(CHUNKS_PER_STEP):
        rows = slice(sub * CHUNK, (sub + 1) * CHUNK)
        for u in range(n_units):
            bi, pi = u // N_PAIRS, u % N_PAIRS
            if sub == 0:
                get_state = lambda u=u: state_scr[u]
            else:
                get_state = lambda u=u, sub=sub: outs[(sub - 1) * n_units + u][1]
            jobs.append(_chunk_unit(r_ref[bi, pi, rows].astype(F32), k_ref[bi, pi, rows].astype(F32),
                                    v_ref[bi, pi, rows].astype(F32), lw_ref[bi, pi, rows],
                                    kn_ref[bi, pi, rows].astype(F32), b_ref[bi, pi, rows].astype(F32),
                                    get_state))
    _interleave(jobs, outs)
    for sub in range(CHUNKS_PER_STEP):
        rows = slice(sub * CHUNK, (sub + 1) * CHUNK)
        for u in range(n_units):
            bi, pi = u // N_PAIRS, u % N_PAIRS
            y, new_state = outs[sub * n_units + u]
            y_ref[bi, pi, rows] = y.astype(BF16)
            if sub == CHUNKS_PER_STEP - 1:
                state_scr[u] = new_state


def _rwkv_chunk(r, k, v, lw, kn, b):
    bsz, _, lp, _ = r.shape
    n_units = bsz * N_PAIRS
    rows = CHUNKS_PER_STEP * CHUNK
    spec = pl.BlockSpec((bsz, N_PAIRS, rows, PAIR), lambda c: (0, 0, c, 0))
    return pl.pallas_call(
        functools.partial(_rwkv_chunk_kernel, n_units=n_units),
        grid=(lp // rows,),
        in_specs=[spec] * 6,
        out_specs=spec,
        out_shape=jax.ShapeDtypeStruct(r.shape, BF16),
        scratch_shapes=[pltpu.VMEM((n_units, PAIR, PAIR), F32)],
        compiler_params=_params(("arbitrary",)),
        name="rwkv_chunk",
    )(r, k, v, lw, kn, b)


def _diff_attn_kernel(q_ref, k_ref, v_ref, lq1_ref, lk1_ref, lq2_ref, lk2_ref, subw_ref, out_init_ref,
                      o_ref, m_scr, acc_scr, s0_scr, *, tq, kv_tiles, block0):
    del out_init_ref
    kb = ATTN_KEY_BLOCK
    diag = block0 + (pl.program_id(2) * tq) // kb
    head0 = lax.broadcasted_iota(jnp.int32, (tq, PAIR), 1) < HEAD_DIM
    qq = []
    for t in range(ATTN_HEADS_PER_STEP):
        q = q_ref[0, :, t * PAIR:(t + 1) * PAIR]
        zero = jnp.zeros_like(q)
        qq.append(jnp.concatenate([jnp.where(head0, q, zero), jnp.where(head0, zero, q)], axis=0))

    def scores(t, j, width):
        start = pl.multiple_of(j * kb, kb)
        return _dot_nt(qq[t], k_ref[0, pl.ds(start, width), t * PAIR:(t + 1) * PAIR])

    def step(j, width, masked, s_first=None, emit_next_first=None):
        start = pl.multiple_of(j * kb, kb)
        ones = jnp.ones((width, PAIR), BF16)
        s_next = scores(0, j, width) if s_first is None else s_first
        for t in range(ATTN_HEADS_PER_STEP):
            s = s_next
            if t + 1 < ATTN_HEADS_PER_STEP:
                s_next = scores(t + 1, j, width)
            elif emit_next_first is not None:
                emit_next_first()
            if masked:
                rr = lax.broadcasted_iota(jnp.int32, (2 * tq, width), 0) % tq
                cc = lax.broadcasted_iota(jnp.int32, (2 * tq, width), 1)
                s = jnp.where(cc <= rr, s, -1e30)
            row_max = jnp.max(s, axis=1, keepdims=True)
            if masked:
                m_next = jnp.broadcast_to(row_max, (2 * tq, PAIR))
            else:
                m_prev = m_scr[t]
                m_next = jnp.maximum(m_prev, row_max)
            p = jnp.exp2(s - jnp.tile(m_next, (1, width // LANE))).astype(BF16)
            v_ext = jnp.concatenate([v_ref[0, pl.ds(start, width), t * PAIR:(t + 1) * PAIR], ones], axis=1)
            pv = jnp.dot(p, v_ext, preferred_element_type=F32)
            if masked:
                acc_scr[t] = pv
            else:
                acc_scr[t] = jnp.tile(jnp.exp2(m_prev - m_next), (1, 2)) * acc_scr[t] + pv
            m_scr[t] = m_next

    n_big = diag // kv_tiles
    big = kv_tiles * kb

    def big_body(j, carry):
        def emit_next_first():
            j_next = jnp.minimum(j + 1, n_big - 1)
            s0_scr[...] = scores(0, j_next * kv_tiles, big)

        step(j * kv_tiles, big, False, s_first=s0_scr[...], emit_next_first=emit_next_first)
        return carry

    rest_blocks = max(tq // kb, 1)

    def rest_body(j, carry):
        step(n_big * kv_tiles + j * rest_blocks, rest_blocks * kb, False)
        return carry

    def emit_first_big():
        s0_scr[...] = scores(0, 0, big)

    step(diag, max(tq, kb), True, emit_next_first=emit_first_big)
    lax.fori_loop(0, n_big, big_body, 0)
    lax.fori_loop(0, (diag - n_big * kv_tiles) // rest_blocks, rest_body, 0)

    lam = (jnp.exp(jnp.sum(lq1_ref[...] * lk1_ref[...], axis=1, keepdims=True))
           - jnp.exp(jnp.sum(lq2_ref[...] * lk2_ref[...], axis=1, keepdims=True)) + LAMBDA_INIT)
    for t in range(ATTN_HEADS_PER_STEP):
        acc = acc_scr[t]
        o = acc[:, :PAIR] / acc[:, PAIR:]
        od = o[:tq] - lam * o[tq:]
        o_ref[0, :tq, t * PAIR:(t + 1) * PAIR] = (
            _rms(od, subw_ref[...], SUBLN_EPS) * (1.0 - LAMBDA_INIT)).astype(o_ref.dtype)
    if o_ref.shape[1] > tq:
        o_ref[0, tq:, :] = jnp.zeros((o_ref.shape[1] - tq, o_ref.shape[2]), o_ref.dtype)


def _diff_attn(a5, lams, subw, out, *, tq, n_tiles, row0, kv_tiles, out_rows=None):
    bsz, lp, _ = a5.shape
    width = ATTN_HEADS_PER_STEP * PAIR
    n_groups = D_MODEL // width
    out_rows = tq if out_rows is None else out_rows
    assert row0 % tq == 0 and row0 % out_rows == 0 and row0 % ATTN_KEY_BLOCK == 0
    assert tq % ATTN_KEY_BLOCK == 0 or n_tiles == 1
    q_spec = pl.BlockSpec((1, tq, width), lambda b, g, i: (b, row0 // tq + i, g))
    k_spec = pl.BlockSpec((1, lp, width), lambda b, g, i: (b, 0, n_groups + g))
    v_spec = pl.BlockSpec((1, lp, width), lambda b, g, i: (b, 0, 2 * n_groups + g))
    lam_spec = pl.BlockSpec((1, HEAD_DIM), lambda b, g, i: (0, 0))
    in_specs = [q_spec, k_spec, v_spec, lam_spec, lam_spec, lam_spec, lam_spec,
                pl.BlockSpec((1, PAIR), lambda b, g, i: (0, 0))]
    in_specs.append(pl.BlockSpec(memory_space=pl.ANY))
    args = (a5, a5, a5, *lams, subw, out)
    return pl.pallas_call(
        functools.partial(_diff_attn_kernel, tq=tq, kv_tiles=kv_tiles, block0=row0 // ATTN_KEY_BLOCK),
        grid=(bsz, n_groups, n_tiles),
        in_specs=in_specs,
        out_specs=pl.BlockSpec((1, out_rows, width), lambda b, g, i: (b, row0 // out_rows + i, g)),
        out_shape=jax.ShapeDtypeStruct((bsz, lp, D_MODEL), BF16),
        input_output_aliases={len(args) - 1: 0},
        scratch_shapes=[pltpu.VMEM((ATTN_HEADS_PER_STEP, 2 * tq, PAIR), F32),
                        pltpu.VMEM((ATTN_HEADS_PER_STEP, 2 * tq, 2 * PAIR), F32),
                        pltpu.VMEM((2 * tq, kv_tiles * ATTN_KEY_BLOCK), F32)],
        compiler_params=_params(("arbitrary", "arbitrary", "arbitrary")),
        name=f"diff_attn_q{tq}",
    )(*args)


def _merge_kernel(y_ref, bonus_ref, g_ref, od_ref, gate_r_ref, gate_d_ref, res_ref, lnw_ref, lnb_ref,
                  ones_ref, wo_r_ref, wo_d_ref, wout_ref, o_ref):
    ones_bd = ones_ref[...]
    parts = []
    for pair in range(0, N_PAIRS, 2):
        y = jnp.concatenate([y_ref[0, pair], y_ref[0, pair + 1]], axis=1).astype(F32)
        mean = _dot(y, ones_bd) * (1.0 / HEAD_DIM)
        yc = y - mean
        var = _dot(yc * yc, ones_bd) * (1.0 / HEAD_DIM)
        parts.append(yc * lax.rsqrt(var + RWKV_LN_EPS))
    yn = jnp.concatenate(parts, axis=1)
    o_rwkv = (yn * lnw_ref[...] + lnb_ref[...] + bonus_ref[0].astype(F32)) * g_ref[0].astype(F32)
    br_rwkv = _dot(o_rwkv, wo_r_ref[...])
    br_diff = jnp.dot(od_ref[0], wo_d_ref[...], preferred_element_type=F32)
    merged = (_sigmoid(gate_r_ref[0].astype(F32)) * br_rwkv
              + _sigmoid(gate_d_ref[0].astype(F32)) * br_diff)
    o_ref[0] = res_ref[0] + _dot(merged, wout_ref[...])


def _merge(y, bonus, g, od, a5, res, lnw, lnb, ones_pair, wo_r, wo_d, wout, *, tm):
    bsz, lp, _ = res.shape
    grid = (bsz, lp // tm)
    tok = lambda c: pl.BlockSpec((1, tm, c), lambda b, i: (b, i, 0))
    vec = pl.BlockSpec((1, D_MODEL), lambda b, i: (0, 0))
    mat = pl.BlockSpec((D_MODEL, D_MODEL), lambda b, i: (0, 0))
    gate_r_spec = pl.BlockSpec((1, tm, D_MODEL), lambda b, i: (b, i, 3))
    gate_d_spec = pl.BlockSpec((1, tm, D_MODEL), lambda b, i: (b, i, 4))
    return pl.pallas_call(
        _merge_kernel,
        grid=grid,
        in_specs=[pl.BlockSpec((1, N_PAIRS, tm, PAIR), lambda b, i: (b, 0, i, 0)),
                  tok(D_MODEL), tok(D_MODEL), tok(D_MODEL), gate_r_spec, gate_d_spec, tok(D_MODEL), vec, vec,
                  pl.BlockSpec((2 * PAIR, 2 * PAIR), lambda b, i: (0, 0)), mat, mat, mat],
        out_specs=tok(D_MODEL),
        out_shape=jax.ShapeDtypeStruct(res.shape, F32),
        compiler_params=_params(("arbitrary", "arbitrary")),
        name="merge",
    )(y, bonus, g, od, a5, a5, res, lnw, lnb, ones_pair, wo_r, wo_d, wout)


def _mlp_kernel(x_ref, nw_ref, w1_ref, w2_ref, fw_ref, o_ref, *, n_ff_chunks):
    x = x_ref[...]
    h = _rms(x, nw_ref[...], NORM_EPS).astype(BF16)
    acc = x
    ff = D_FF // n_ff_chunks
    for c in range(n_ff_chunks):
        z = jnp.dot(h, w1_ref[:, c * ff:(c + 1) * ff], preferred_element_type=F32)
        act = jnp.square(jnp.maximum(z, 0.0))
        acc = acc + _dot(act, w2_ref[c * ff:(c + 1) * ff, :])
    o_ref[...] = _rms(acc, fw_ref[...], NORM_EPS)


def _mlp(x3, nw, w1, w2, fw, *, seq, row0, tm):
    bsz = x3.shape[0]
    x_spec = pl.BlockSpec((None, pl.Element(tm), pl.Element(D_MODEL)), lambda b, i: (b, pl.multiple_of(row0 + i * tm, SUBLANE), 0))
    vec = pl.BlockSpec((1, D_MODEL), lambda b, i: (0, 0))
    return pl.pallas_call(
        functools.partial(_mlp_kernel, n_ff_chunks=MLP_FF_CHUNKS),
        grid=(bsz, seq // tm),
        in_specs=[x_spec, vec, pl.BlockSpec((D_MODEL, D_FF), lambda b, i: (0, 0)),
                  pl.BlockSpec((D_FF, D_MODEL), lambda b, i: (0, 0)), vec],
        out_specs=pl.BlockSpec((None, tm, D_MODEL), lambda b, i: (b, i, 0)),
        out_shape=jax.ShapeDtypeStruct((bsz, seq, D_MODEL), F32),
        compiler_params=_params(("arbitrary", "arbitrary")),
        name="mlp",
    )(x3, nw, w1, w2, fw)


def _pad_cols(a, width):
    return jnp.pad(a, ((0, 0), (0, width - a.shape[1])))


def _pad_rows(a, height):
    return jnp.pad(a, ((0, height - a.shape[0]), (0, 0)))


def _rope_tables(lp):
    inv = ROPE_THETA ** (-jnp.arange(0, ROPE_DIM, 2, dtype=F32) / ROPE_DIM)
    ang = jnp.arange(lp, dtype=F32)[:, None] * inv[None, :]
    cos, sin = jnp.cos(ang), jnp.sin(ang)
    rest = HEAD_DIM - ROPE_DIM
    ones = jnp.ones((lp, rest), F32)
    zeros = jnp.zeros((lp, rest), F32)
    zh = jnp.zeros_like(sin)
    c = jnp.concatenate([cos, cos, ones], axis=1)
    s1 = jnp.concatenate([-sin, zh, zeros], axis=1)
    s2 = jnp.concatenate([zh, sin, zeros], axis=1)
    return tuple(jnp.tile(t, (1, LANE // HEAD_DIM)) for t in (c, s1, s2))


def _pick(n, candidates):
    for c in candidates:
        if n % c == 0:
            return c
    raise ValueError(n)


class _Tiles(NamedTuple):
    proj: int
    merge: int
    mlp: int
    kv_blocks: int


def _tiles(seq, lp):
    return _Tiles(proj=_pick(lp, PROJ_TILES), merge=_pick(lp, MERGE_TILES), mlp=_pick(seq, MLP_TILES),
                  kv_blocks=min(ATTN_BIG_STEP_BLOCKS, lp // ATTN_KEY_BLOCK))


def kernel(x, meta_tokens, norm_mix_w, w_in, rwkv_mu, rwkv_w0, rwkv_w2, rwkv_a0, rwkv_a2, rwkv_g2, rwkv_k_k, rwkv_k_a, rwkv_r_k, rwkv_ln_w, rwkv_ln_b, rwkv_w_o, diff_lq1, diff_lk1, diff_lq2, diff_lk2, diff_subln_w, diff_w_o, w_out, norm_mlp_w, mlp_w1, mlp_w2, final_norm_w):
    bsz, seq, _ = x.shape
    seqlen = seq + N_META_TOKENS
    lp = -(-seqlen // SEQ_ALIGN) * SEQ_ALIGN
    tiles = _tiles(seq, lp)

    meta = jnp.broadcast_to(meta_tokens[None].astype(x.dtype), (bsz, N_META_TOKENS, D_MODEL))
    h_res = jnp.concatenate([meta, x, jnp.zeros((bsz, lp - seqlen, D_MODEL), x.dtype)], axis=1)
    hf = h_res.reshape(bsz * lp, D_MODEL)

    w = w_in.reshape(D_MODEL, -1).astype(BF16)
    c3 = 3 * D_MODEL
    w_rwkv = jnp.concatenate(
        [w[:, :c3], _pad_cols(w[:, c3:c3 + DECAY_LORA], LANE),
         _pad_cols(w[:, c3 + DECAY_LORA:c3 + DECAY_LORA + AAA_LORA], LANE),
         _pad_cols(w[:, c3 + DECAY_LORA + AAA_LORA:c3 + DECAY_LORA + AAA_LORA + GATE_LORA], GATE_LORA_PAD)],
        axis=1)
    rwkv_cols = c3 + DECAY_LORA + AAA_LORA + GATE_LORA
    w_diff = w[:, rwkv_cols:]
    mu = rwkv_mu[0][None, :]
    mu_p = jnp.concatenate(
        [mu[:, :c3], _pad_cols(mu[:, c3:c3 + DECAY_LORA], LANE),
         _pad_cols(mu[:, c3 + DECAY_LORA:c3 + DECAY_LORA + AAA_LORA], LANE),
         _pad_cols(mu[:, c3 + DECAY_LORA + AAA_LORA:], GATE_LORA_PAD)], axis=1)
    nw = norm_mix_w[0][None, :]

    tiles_per_seq = lp // tiles.proj
    head_id = jnp.arange(2 * PAIR) // HEAD_DIM
    ones_bd = (head_id[:, None] == head_id[None, :]).astype(BF16)
    vecs = (rwkv_w0[0][None, :], rwkv_a0[0][None, :], rwkv_k_k[0][None, :], rwkv_k_a[0][None, :],
            rwkv_r_k[0].reshape(1, D_MODEL))
    r, kmod, v, lw, kn, b, bonus, g, xn = _rwkv_proj(
        hf, nw, w_rwkv, mu_p, vecs, _pad_rows(rwkv_w2[0], LANE).astype(BF16),
        _pad_rows(rwkv_a2[0], LANE).astype(BF16), _pad_rows(rwkv_g2[0], GATE_LORA_PAD).astype(BF16),
        ones_bd, bsz=bsz, lp=lp, tm=tiles.proj)
    y = _rwkv_chunk(r, kmod, v, lw, kn, b)
    a5 = _in_proj(xn, w_diff, _rope_tables(lp), tm=tiles.proj, tiles_per_seq=tiles_per_seq).reshape(
        bsz, lp, 5 * D_MODEL)

    lams = (diff_lq1, diff_lk1, diff_lq2, diff_lk2)
    subw = diff_subln_w[0][None, :]
    kv_tiles = tiles.kv_blocks
    o_diff, row0 = jnp.zeros((bsz, lp, D_MODEL), BF16), 0
    for tq in (ATTN_Q_TILE, ATTN_KEY_BLOCK):
        n_tiles = (seqlen - row0) // tq
        if n_tiles:
            o_diff = _diff_attn(a5, lams, subw, o_diff, tq=tq, n_tiles=n_tiles, row0=row0, kv_tiles=kv_tiles)
            row0 += n_tiles * tq
    if row0 < lp:
        tail = max(-(-(seqlen - row0) // ATTN_TAIL_ROWS) * ATTN_TAIL_ROWS, ATTN_TAIL_ROWS)
        o_diff = _diff_attn(a5, lams, subw, o_diff, tq=tail, n_tiles=1, row0=row0, kv_tiles=kv_tiles,
                            out_rows=lp - row0)

    h_mid = _merge(y, bonus, g, o_diff, a5, h_res, rwkv_ln_w[0][None, :], rwkv_ln_b[0][None, :],
                   ones_bd, rwkv_w_o[0].astype(BF16), diff_w_o[0].astype(BF16),
                   w_out[0].astype(BF16), tm=tiles.merge)
    return _mlp(h_mid, norm_mlp_w[0][None, :], mlp_w1[0].astype(BF16), mlp_w2[0].astype(BF16),
                final_norm_w[None, :], seq=seq, row0=N_META_TOKENS, tm=tiles.mlp)
```

```python
import functools
import math
from typing import NamedTuple

import jax
import jax.numpy as jnp
from jax import lax
from jax.experimental import pallas as pl
from jax.experimental.pallas import tpu as pltpu

F32 = jnp.float32
BF16 = jnp.bfloat16

D_MODEL = 1024
N_META_TOKENS = 16
HEAD_DIM = 64
PAIR = 2 * HEAD_DIM
N_PAIRS = D_MODEL // PAIR
DECAY_LORA = 64
AAA_LORA = 64
GATE_LORA = 160
RWKV_LN_EPS = 64e-5
ROPE_THETA = 500000.0
ROPE_DIM = HEAD_DIM // 4
ROPE_HALF = ROPE_DIM // 2
D_FF = 4 * D_MODEL
NORM_EPS = 1e-5
SUBLN_EPS = 1e-5
LAMBDA_INIT = 0.8 - 0.6 * math.exp(-0.3 * 0)

LANE = 128
SUBLANE = 8
PREV_ROWS = 16
CHUNK = 64
CHUNKS_PER_STEP = 4
PROJ_COL_CHUNK = 256
ATTN_KEY_BLOCK = 256
ATTN_Q_TILE = 512
ATTN_TAIL_ROWS = 16
ATTN_HEADS_PER_STEP = 4
ATTN_BIG_STEP_BLOCKS = 4
MLP_FF_CHUNKS = 4
PROJ_TILES = (768, 512, 256)
MERGE_TILES = (768, 384, 256)
MLP_TILES = (512, 256, 128)
Q_SCALE = HEAD_DIM ** -0.5 * math.log2(math.e)
SEQ_ALIGN = 256
VMEM_LIMIT = 56 * 1024 * 1024

LORA_W_OFF = 3 * D_MODEL
LORA_A_OFF = LORA_W_OFF + LANE
LORA_G_OFF = LORA_A_OFF + LANE
GATE_LORA_PAD = 2 * LANE


def _params(sem):
    return pltpu.CompilerParams(dimension_semantics=sem, vmem_limit_bytes=VMEM_LIMIT)


def _rms(x, w, eps):
    return x * lax.rsqrt(jnp.mean(x * x, axis=-1, keepdims=True) + eps) * w


def _sigmoid(x):
    return 1.0 / (1.0 + jnp.exp(-x))


def _dot(a, b):
    return jnp.dot(a.astype(BF16), b.astype(BF16), preferred_element_type=F32)


def _dot_nt(a, b):
    return lax.dot_general(a.astype(BF16), b.astype(BF16), (((1,), (1,)), ((), ())),
                           preferred_element_type=F32)


def _dot_tn(a, b):
    return lax.dot_general(a.astype(BF16), b.astype(BF16), (((0,), (0,)), ((), ())),
                           preferred_element_type=F32)


def _in_proj_kernel(xn_ref, w_ref, cos_ref, s1_ref, s2_ref, o_ref):
    n_cols = o_ref.shape[1]
    cc = PROJ_COL_CHUNK
    xn = xn_ref[...]
    reps = cc // LANE
    cos = jnp.tile(cos_ref[...], (1, reps))
    s1 = jnp.tile(s1_ref[...], (1, reps))
    s2 = jnp.tile(s2_ref[...], (1, reps))
    for c in range(n_cols // cc):
        cols = slice(c * cc, (c + 1) * cc)
        p = jnp.dot(xn, w_ref[:, cols], preferred_element_type=F32)
        if c * cc < 2 * D_MODEL:
            nxt = pltpu.roll(p, cc - ROPE_HALF, axis=1)
            prv = pltpu.roll(p, ROPE_HALF, axis=1)
            scale = Q_SCALE if c * cc < D_MODEL else 1.0
            p = (p * cos + nxt * s1 + prv * s2) * scale
        o_ref[:, cols] = p.astype(o_ref.dtype)


def _in_proj(xn, w, rope, *, tm, tiles_per_seq):
    t_tokens = xn.shape[0]
    n_cols = w.shape[1]
    tab_spec = pl.BlockSpec((tm, LANE), lambda i: (i % tiles_per_seq, 0))
    return pl.pallas_call(
        _in_proj_kernel,
        grid=(t_tokens // tm,),
        in_specs=[pl.BlockSpec((tm, D_MODEL), lambda i: (i, 0)),
                  pl.BlockSpec((D_MODEL, n_cols), lambda i: (0, 0), pipeline_mode=pl.Buffered(1)),
                  tab_spec, tab_spec, tab_spec],
        out_specs=pl.BlockSpec((tm, n_cols), lambda i: (i, 0)),
        out_shape=jax.ShapeDtypeStruct((t_tokens, n_cols), BF16),
        compiler_params=_params(("arbitrary",)),
        name="in_proj_diff",
    )(xn, w, *rope)


def _rwkv_proj_kernel(x_ref, xprev_ref, nw_ref, w_ref, mu_ref, w0_ref, a0_ref, kk_ref, ka_ref, rk_ref,
                      w2_ref, a2_ref, g2_ref, ones_ref, r_o, k_o, v_o, lw_o, kn_o, b_o, bonus_o, g_o, xn_o,
                      *, tiles_per_seq):
    tm = x_ref.shape[0]
    blk = 2 * PAIR
    xn = jnp.concatenate([_rms(x_ref[...], nw_ref[...], NORM_EPS).astype(BF16),
                          _rms(xprev_ref[...], nw_ref[...], NORM_EPS).astype(BF16)], axis=0)
    first = (pl.program_id(0) % tiles_per_seq) == 0
    row = lax.broadcasted_iota(jnp.int32, (tm, blk), 0)
    xn_o[...] = xn[:tm]

    def proj(col0):
        cols = slice(col0, col0 + blk)
        p = jnp.dot(xn, w_ref[:, cols], preferred_element_type=F32)
        prev_row = jnp.where(first, 0.0, p[tm + PREV_ROWS - 1:tm + PREV_ROWS, :])
        p = p[:tm]
        shifted = jnp.where(row == 0, prev_row, pltpu.roll(p, 1, axis=0))
        return p + (shifted - p) * mu_ref[:, cols]

    def rkv(c):
        return proj(c * blk), proj(D_MODEL + c * blk), proj(2 * D_MODEL + c * blk)

    p_wa = proj(LORA_W_OFF)
    tanh_wd = jnp.tanh(p_wa[:, :LANE]).astype(BF16)
    ad = p_wa[:, LANE:].astype(BF16)
    sig_gd = _sigmoid(proj(LORA_G_OFF)).astype(BF16)
    ones_bd = ones_ref[...]
    n_blocks = D_MODEL // blk
    nxt = rkv(0)
    for c in range(n_blocks):
        cols = slice(c * blk, (c + 1) * blk)
        r, k, v = nxt
        if c + 1 < n_blocks:
            nxt = rkv(c + 1)
        z = w0_ref[:, cols] + jnp.dot(tanh_wd, w2_ref[:, cols], preferred_element_type=F32)
        log_decay = -math.exp(-0.5) * _sigmoid(z)
        a_rate = _sigmoid(a0_ref[:, cols] + jnp.dot(ad, a2_ref[:, cols], preferred_element_type=F32))
        gate = jnp.dot(sig_gd, g2_ref[:, cols], preferred_element_type=F32)

        kraw = k * kk_ref[:, cols]
        sumsq = _dot(kraw * kraw, ones_bd)
        knorm = kraw * lax.rsqrt(jnp.maximum(sumsq, 1e-24))
        kmod = k * (1.0 + (a_rate - 1.0) * ka_ref[:, cols])
        bonus = _dot(r * kmod * rk_ref[:, cols], ones_bd) * v

        bonus_o[0, :, cols] = bonus.astype(BF16)
        g_o[0, :, cols] = gate.astype(BF16)
        r16, k16, v16 = r.astype(BF16), kmod.astype(BF16), v.astype(BF16)
        kn16, b16 = knorm.astype(BF16), (knorm * a_rate).astype(BF16)
        for h in range(2):
            pair = 2 * c + h
            sl = slice(h * PAIR, (h + 1) * PAIR)
            r_o[0, pair] = r16[:, sl]
            k_o[0, pair] = k16[:, sl]
            v_o[0, pair] = v16[:, sl]
            lw_o[0, pair] = log_decay[:, sl]
            kn_o[0, pair] = kn16[:, sl]
            b_o[0, pair] = b16[:, sl]


def _rwkv_proj(xf, nw, w, mu, vecs, w2p, a2p, g2p, ones_bd, *, bsz, lp, tm):
    tiles_per_seq = lp // tm
    n_cols = w.shape[1]
    blocks = tm // PREV_ROWS
    const = lambda shape: pl.BlockSpec(shape, lambda i: (0, 0))
    pair_spec = pl.BlockSpec((1, N_PAIRS, tm, PAIR), lambda i: (i // tiles_per_seq, 0, i % tiles_per_seq, 0))
    tok_spec = pl.BlockSpec((1, tm, D_MODEL), lambda i: (i // tiles_per_seq, i % tiles_per_seq, 0))
    pair_shape = jax.ShapeDtypeStruct((bsz, N_PAIRS, lp, PAIR), BF16)
    decay_shape = jax.ShapeDtypeStruct((bsz, N_PAIRS, lp, PAIR), F32)
    tok_shape = jax.ShapeDtypeStruct((bsz, lp, D_MODEL), BF16)
    return pl.pallas_call(
        functools.partial(_rwkv_proj_kernel, tiles_per_seq=tiles_per_seq),
        grid=(bsz * tiles_per_seq,),
        in_specs=[pl.BlockSpec((tm, D_MODEL), lambda i: (i, 0)),
                  pl.BlockSpec((PREV_ROWS, D_MODEL), lambda i: (jnp.maximum(i * blocks - 1, 0), 0)),
                  const((1, D_MODEL)),
                  pl.BlockSpec((D_MODEL, n_cols), lambda i: (0, 0), pipeline_mode=pl.Buffered(1)),
                  const((1, n_cols))]
        + [const((1, D_MODEL))] * 5
        + [const((LANE, D_MODEL)), const((LANE, D_MODEL)), const((GATE_LORA_PAD, D_MODEL)),
           const((2 * PAIR, 2 * PAIR))],
        out_specs=[pair_spec] * 6 + [tok_spec] * 2 + [pl.BlockSpec((tm, D_MODEL), lambda i: (i, 0))],
        out_shape=[pair_shape] * 3 + [decay_shape] + [pair_shape] * 2 + [tok_shape] * 2
        + [jax.ShapeDtypeStruct((bsz * lp, D_MODEL), BF16)],
        compiler_params=_params(("arbitrary",)),
        name="rwkv_proj",
    )(xf, xf, nw, w, mu, *vecs, w2p, a2p, g2p, ones_bd)


def _block_diag(x, lane_head0):
    return jnp.concatenate([jnp.where(lane_head0, x, 0.0), jnp.where(lane_head0, 0.0, x)], axis=0)


def _chunk_unit(r, k, v, lw, kn, b, get_state):
    n = 2 * CHUNK
    row = lax.broadcasted_iota(jnp.int32, (CHUNK, PAIR), 0)
    lane_head0 = lax.broadcasted_iota(jnp.int32, (CHUNK, PAIR), 1) < HEAD_DIM
    c = lw
    for s in (1, 2, 4, 8, 16, 32):
        c = c + jnp.where(row >= s, pltpu.roll(c, s, axis=0), 0.0)
    c_end = c[CHUNK - 1:CHUNK, :]
    e_neg = jnp.exp(-c)
    e_end = jnp.exp(c_end - c)
    a_t = _block_diag(-kn * jnp.exp(c - lw), lane_head0)
    r_t = _block_diag(r * jnp.exp(c), lane_head0)
    b_t = _block_diag(b * e_neg, lane_head0)
    k_t = _block_diag(k * e_neg, lane_head0)
    b_e = _block_diag(b * e_end, lane_head0)
    k_e = _block_diag(k * e_end, lane_head0)
    v_bd = _block_diag(v, lane_head0)

    yield
    g = _dot_nt(jnp.concatenate([a_t, r_t], axis=0), jnp.concatenate([b_t, k_t], axis=0))
    ri = lax.broadcasted_iota(jnp.int32, (n, n), 0)
    ci = lax.broadcasted_iota(jnp.int32, (n, n), 1)
    m_ab = jnp.where(ci < ri, g[:n, :n], 0.0)
    m_ak = jnp.where(ci < ri, g[:n, n:], 0.0)
    a_rb = jnp.where(ci <= ri, g[n:, :n], 0.0)
    a_rk = jnp.where(ci <= ri, g[n:, n:], 0.0)

    t_inv = jnp.where(ci == ri, 1.0, m_ab)
    yield
    power = _dot(m_ab, m_ab)
    w1 = _dot(m_ak, v_bd)
    for _ in range(4):
        yield
        prod = _dot(power, jnp.concatenate([power, t_inv], axis=1))
        power = prod[:, :n]
        t_inv = t_inv + prod[:, n:]
    yield
    t_inv = t_inv + _dot(power, t_inv)

    yield
    tw = _dot(t_inv, jnp.concatenate([w1, a_t], axis=1))
    u0 = tw[:, :n]
    a_hat = tw[:, n:]
    rhs = jnp.concatenate([jnp.concatenate([a_hat, u0], axis=1),
                           jnp.concatenate([jnp.zeros_like(v_bd), v_bd], axis=1)], axis=0)
    yield
    pq = _dot_tn(jnp.concatenate([b_e, k_e], axis=0), rhs)
    ry = _dot(jnp.concatenate([a_rb, a_rk], axis=1), rhs)
    decay_end = jnp.exp(jnp.broadcast_to(c_end, (n, n)))
    p_mat = pq[:, :n] + jnp.where(ci == ri, decay_end, 0.0)
    yield
    state = get_state()
    ys = _dot(jnp.concatenate([r_t + ry[:, :n], p_mat], axis=0), state)
    y_bd = ys[:n] + ry[:, n:]
    y = y_bd[:CHUNK] + y_bd[CHUNK:]
    new_state = ys[n:] + pq[:, n:]
    return y, new_state


def _interleave(generators, results):
    pending = list(range(len(generators)))
    while pending:
        still = []
        for idx in pending:
            try:
                next(generators[idx])
                still.append(idx)
            except StopIteration as stop:
                results[idx] = stop.value
        pending = still


def _rwkv_chunk_kernel(r_ref, k_ref, v_ref, lw_ref, kn_ref, b_ref, y_ref, state_scr, *, n_units):
    @pl.when(pl.program_id(0) == 0)
    def _():
        state_scr[...] = jnp.zeros_like(state_scr)

    jobs = []
    outs = [None] * (CHUNKS_PER_STEP * n_units)
    for sub in range(CHUNKS_PER_STEP):
        rows = slice(sub * CHUNK, (sub + 1) * CHUNK)
        for u in range(n_units):
            bi, pi = u // N_PAIRS, u % N_PAIRS
            if sub == 0:
                get_state = lambda u=u: state_scr[u]
            else:
                get_state = lambda u=u, sub=sub: outs[(sub - 1) * n_units + u][1]
            jobs.append(_chunk_unit(r_ref[bi, pi, rows].astype(F32), k_ref[bi, pi, rows].astype(F32),
                                    v_ref[bi, pi, rows].astype(F32), lw_ref[bi, pi, rows],
                                    kn_ref[bi, pi, rows].astype(F32), b_ref[bi, pi, rows].astype(F32),
                                    get_state))
    _interleave(jobs, outs)
    for sub in range(CHUNKS_PER_STEP):
        rows = slice(sub * CHUNK, (sub + 1) * CHUNK)
        for u in range(n_units):
            bi, pi = u // N_PAIRS, u % N_PAIRS
            y, new_state = outs[sub * n_units + u]
            y_ref[bi, pi, rows] = y.astype(BF16)
            if sub == CHUNKS_PER_STEP - 1:
                state_scr[u] = new_state


def _rwkv_chunk(r, k, v, lw, kn, b):
    bsz, _, lp, _ = r.shape
    n_units = bsz * N_PAIRS
    rows = CHUNKS_PER_STEP * CHUNK
    spec = pl.BlockSpec((bsz, N_PAIRS, rows, PAIR), lambda c: (0, 0, c, 0))
    return pl.pallas_call(
        functools.partial(_rwkv_chunk_kernel, n_units=n_units),
        grid=(lp // rows,),
        in_specs=[spec] * 6,
        out_specs=spec,
        out_shape=jax.ShapeDtypeStruct(r.shape, BF16),
        scratch_shapes=[pltpu.VMEM((n_units, PAIR, PAIR), F32)],
        compiler_params=_params(("arbitrary",)),
        name="rwkv_chunk",
    )(r, k, v, lw, kn, b)


def _diff_attn_kernel(q_ref, k_ref, v_ref, lq1_ref, lk1_ref, lq2_ref, lk2_ref, subw_ref, out_init_ref,
                      o_ref, m_scr, acc_scr, s0_scr, *, tq, kv_tiles, block0):
    del out_init_ref
    kb = ATTN_KEY_BLOCK
    diag = block0 + (pl.program_id(2) * tq) // kb
    head0 = lax.broadcasted_iota(jnp.int32, (tq, PAIR), 1) < HEAD_DIM
    qq = []
    for t in range(ATTN_HEADS_PER_STEP):
        q = q_ref[0, :, t * PAIR:(t + 1) * PAIR]
        zero = jnp.zeros_like(q)
        qq.append(jnp.concatenate([jnp.where(head0, q, zero), jnp.where(head0, zero, q)], axis=0))

    def scores(t, j, width):
        start = pl.multiple_of(j * kb, kb)
        return _dot_nt(qq[t], k_ref[0, pl.ds(start, width), t * PAIR:(t + 1) * PAIR])

    def step(j, width, masked, s_first=None, emit_next_first=None):
        start = pl.multiple_of(j * kb, kb)
        ones = jnp.ones((width, PAIR), BF16)
        s_next = scores(0, j, width) if s_first is None else s_first
        for t in range(ATTN_HEADS_PER_STEP):
            s = s_next
            if t + 1 < ATTN_HEADS_PER_STEP:
                s_next = scores(t + 1, j, width)
            elif emit_next_first is not None:
                emit_next_first()
            if masked:
                rr = lax.broadcasted_iota(jnp.int32, (2 * tq, width), 0) % tq
                cc = lax.broadcasted_iota(jnp.int32, (2 * tq, width), 1)
                s = jnp.where(cc <= rr, s, -1e30)
            row_max = jnp.max(s, axis=1, keepdims=True)
            if masked:
                m_next = jnp.broadcast_to(row_max, (2 * tq, PAIR))
            else:
                m_prev = m_scr[t]
                m_next = jnp.maximum(m_prev, row_max)
            p = jnp.exp2(s - jnp.tile(m_next, (1, width // LANE))).astype(BF16)
            v_ext = jnp.concatenate([v_ref[0, pl.ds(start, width), t * PAIR:(t + 1) * PAIR], ones], axis=1)
            pv = jnp.dot(p, v_ext, preferred_element_type=F32)
            if masked:
                acc_scr[t] = pv
            else:
                acc_scr[t] = jnp.tile(jnp.exp2(m_prev - m_next), (1, 2)) * acc_scr[t] + pv
            m_scr[t] = m_next

    n_big = diag // kv_tiles
    big = kv_tiles * kb

    def big_body(j, carry):
        def emit_next_first():
            j_next = jnp.minimum(j + 1, n_big - 1)
            s0_scr[...] = scores(0, j_next * kv_tiles, big)

        step(j * kv_tiles, big, False, s_first=s0_scr[...], emit_next_first=emit_next_first)
        return carry

    rest_blocks = max(tq // kb, 1)

    def rest_body(j, carry):
        step(n_big * kv_tiles + j * rest_blocks, rest_blocks * kb, False)
        return carry

    def emit_first_big():
        s0_scr[...] = scores(0, 0, big)

    step(diag, max(tq, kb), True, emit_next_first=emit_first_big)
    lax.fori_loop(0, n_big, big_body, 0)
    lax.fori_loop(0, (diag - n_big * kv_tiles) // rest_blocks, rest_body, 0)

    lam = (jnp.exp(jnp.sum(lq1_ref[...] * lk1_ref[...], axis=1, keepdims=True))
           - jnp.exp(jnp.sum(lq2_ref[...] * lk2_ref[...], axis=1, keepdims=True)) + LAMBDA_INIT)
    for t in range(ATTN_HEADS_PER_STEP):
        acc = acc_scr[t]
        o = acc[:, :PAIR] / acc[:, PAIR:]
        od = o[:tq] - lam * o[tq:]
        o_ref[0, :tq, t * PAIR:(t + 1) * PAIR] = (
            _rms(od, subw_ref[...], SUBLN_EPS) * (1.0 - LAMBDA_INIT)).astype(o_ref.dtype)
    if o_ref.shape[1] > tq:
        o_ref[0, tq:, :] = jnp.zeros((o_ref.shape[1] - tq, o_ref.shape[2]), o_ref.dtype)


def _diff_attn(a5, lams, subw, out, *, tq, n_tiles, row0, kv_tiles, out_rows=None):
    bsz, lp, _ = a5.shape
    width = ATTN_HEADS_PER_STEP * PAIR
    n_groups = D_MODEL // width
    out_rows = tq if out_rows is None else out_rows
    assert row0 % tq == 0 and row0 % out_rows == 0 and row0 % ATTN_KEY_BLOCK == 0
    assert tq % ATTN_KEY_BLOCK == 0 or n_tiles == 1
    q_spec = pl.BlockSpec((1, tq, width), lambda b, g, i: (b, row0 // tq + i, g))
    k_spec = pl.BlockSpec((1, lp, width), lambda b, g, i: (b, 0, n_groups + g))
    v_spec = pl.BlockSpec((1, lp, width), lambda b, g, i: (b, 0, 2 * n_groups + g))
    lam_spec = pl.BlockSpec((1, HEAD_DIM), lambda b, g, i: (0, 0))
    in_specs = [q_spec, k_spec, v_spec, lam_spec, lam_spec, lam_spec, lam_spec,
                pl.BlockSpec((1, PAIR), lambda b, g, i: (0, 0))]
    in_specs.append(pl.BlockSpec(memory_space=pl.ANY))
    args = (a5, a5, a5, *lams, subw, out)
    return pl.pallas_call(
        functools.partial(_diff_attn_kernel, tq=tq, kv_tiles=kv_tiles, block0=row0 // ATTN_KEY_BLOCK),
        grid=(bsz, n_groups, n_tiles),
        in_specs=in_specs,
        out_specs=pl.BlockSpec((1, out_rows, width), lambda b, g, i: (b, row0 // out_rows + i, g)),
        out_shape=jax.ShapeDtypeStruct((bsz, lp, D_MODEL), BF16),
        input_output_aliases={len(args) - 1: 0},
        scratch_shapes=[pltpu.VMEM((ATTN_HEADS_PER_STEP, 2 * tq, PAIR), F32),
                        pltpu.VMEM((ATTN_HEADS_PER_STEP, 2 * tq, 2 * PAIR), F32),
                        pltpu.VMEM((2 * tq, kv_tiles * ATTN_KEY_BLOCK), F32)],
        compiler_params=_params(("arbitrary", "arbitrary", "arbitrary")),
        name=f"diff_attn_q{tq}",
    )(*args)


def _merge_kernel(y_ref, bonus_ref, g_ref, od_ref, gate_r_ref, gate_d_ref, res_ref, lnw_ref, lnb_ref,
                  ones_ref, wo_r_ref, wo_d_ref, wout_ref, o_ref):
    ones_bd = ones_ref[...]
    parts = []
    for pair in range(0, N_PAIRS, 2):
        y = jnp.concatenate([y_ref[0, pair], y_ref[0, pair + 1]], axis=1).astype(F32)
        mean = _dot(y, ones_bd) * (1.0 / HEAD_DIM)
        yc = y - mean
        var = _dot(yc * yc, ones_bd) * (1.0 / HEAD_DIM)
        parts.append(yc * lax.rsqrt(var + RWKV_LN_EPS))
    yn = jnp.concatenate(parts, axis=1)
    o_rwkv = (yn * lnw_ref[...] + lnb_ref[...] + bonus_ref[0].astype(F32)) * g_ref[0].astype(F32)
    br_rwkv = _dot(o_rwkv, wo_r_ref[...])
    br_diff = jnp.dot(od_ref[0], wo_d_ref[...], preferred_element_type=F32)
    merged = (_sigmoid(gate_r_ref[0].astype(F32)) * br_rwkv
              + _sigmoid(gate_d_ref[0].astype(F32)) * br_diff)
    o_ref[0] = res_ref[0] + _dot(merged, wout_ref[...])


def _merge(y, bonus, g, od, a5, res, lnw, lnb, ones_pair, wo_r, wo_d, wout, *, tm):
    bsz, lp, _ = res.shape
    grid = (bsz, lp // tm)
    tok = lambda c: pl.BlockSpec((1, tm, c), lambda b, i: (b, i, 0))
    vec = pl.BlockSpec((1, D_MODEL), lambda b, i: (0, 0))
    mat = pl.BlockSpec((D_MODEL, D_MODEL), lambda b, i: (0, 0))
    gate_r_spec = pl.BlockSpec((1, tm, D_MODEL), lambda b, i: (b, i, 3))
    gate_d_spec = pl.BlockSpec((1, tm, D_MODEL), lambda b, i: (b, i, 4))
    return pl.pallas_call(
        _merge_kernel,
        grid=grid,
        in_specs=[pl.BlockSpec((1, N_PAIRS, tm, PAIR), lambda b, i: (b, 0, i, 0)),
                  tok(D_MODEL), tok(D_MODEL), tok(D_MODEL), gate_r_spec, gate_d_spec, tok(D_MODEL), vec, vec,
                  pl.BlockSpec((2 * PAIR, 2 * PAIR), lambda b, i: (0, 0)), mat, mat, mat],
        out_specs=tok(D_MODEL),
        out_shape=jax.ShapeDtypeStruct(res.shape, F32),
        compiler_params=_params(("arbitrary", "arbitrary")),
        name="merge",
    )(y, bonus, g, od, a5, a5, res, lnw, lnb, ones_pair, wo_r, wo_d, wout)


def _mlp_kernel(x_ref, nw_ref, w1_ref, w2_ref, fw_ref, o_ref, *, n_ff_chunks):
    x = x_ref[...]
    h = _rms(x, nw_ref[...], NORM_EPS).astype(BF16)
    acc = x
    ff = D_FF // n_ff_chunks
    for c in range(n_ff_chunks):
        z = jnp.dot(h, w1_ref[:, c * ff:(c + 1) * ff], preferred_element_type=F32)
        act = jnp.square(jnp.maximum(z, 0.0))
        acc = acc + _dot(act, w2_ref[c * ff:(c + 1) * ff, :])
    o_ref[...] = _rms(acc, fw_ref[...], NORM_EPS)


def _mlp(x3, nw, w1, w2, fw, *, seq, row0, tm):
    bsz = x3.shape[0]
    x_spec = pl.BlockSpec((None, pl.Element(tm), pl.Element(D_MODEL)), lambda b, i: (b, pl.multiple_of(row0 + i * tm, SUBLANE), 0))
    vec = pl.BlockSpec((1, D_MODEL), lambda b, i: (0, 0))
    return pl.pallas_call(
        functools.partial(_mlp_kernel, n_ff_chunks=MLP_FF_CHUNKS),
        grid=(bsz, seq // tm),
        in_specs=[x_spec, vec, pl.BlockSpec((D_MODEL, D_FF), lambda b, i: (0, 0)),
                  pl.BlockSpec((D_FF, D_MODEL), lambda b, i: (0, 0)), vec],
        out_specs=pl.BlockSpec((None, tm, D_MODEL), lambda b, i: (b, i, 0)),
        out_shape=jax.ShapeDtypeStruct((bsz, seq, D_MODEL), F32),
        compiler_params=_params(("arbitrary", "arbitrary")),
        name="mlp",
    )(x3, nw, w1, w2, fw)


def _pad_cols(a, width):
    return jnp.pad(a, ((0, 0), (0, width - a.shape[1])))


def _pad_rows(a, height):
    return jnp.pad(a, ((0, height - a.shape[0]), (0, 0)))


def _rope_tables(lp):
    inv = ROPE_THETA ** (-jnp.arange(0, ROPE_DIM, 2, dtype=F32) / ROPE_DIM)
    ang = jnp.arange(lp, dtype=F32)[:, None] * inv[None, :]
    cos, sin = jnp.cos(ang), jnp.sin(ang)
    rest = HEAD_DIM - ROPE_DIM
    ones = jnp.ones((lp, rest), F32)
    zeros = jnp.zeros((lp, rest), F32)
    zh = jnp.zeros_like(sin)
    c = jnp.concatenate([cos, cos, ones], axis=1)
    s1 = jnp.concatenate([-sin, zh, zeros], axis=1)
    s2 = jnp.concatenate([zh, sin, zeros], axis=1)
    return tuple(jnp.tile(t, (1, LANE // HEAD_DIM)) for t in (c, s1, s2))


def _pick(n, candidates):
    for c in candidates:
        if n % c == 0:
            return c
    raise ValueError(n)


class _Tiles(NamedTuple):
    proj: int
    merge: int
    mlp: int
    kv_blocks: int


def _tiles(seq, lp):
    return _Tiles(proj=_pick(lp, PROJ_TILES), merge=_pick(lp, MERGE_TILES), mlp=_pick(seq, MLP_TILES),
                  kv_blocks=min(ATTN_BIG_STEP_BLOCKS, lp // ATTN_KEY_BLOCK))


def kernel(x, meta_tokens, norm_mix_w, w_in, rwkv_mu, rwkv_w0, rwkv_w2, rwkv_a0, rwkv_a2, rwkv_g2, rwkv_k_k, rwkv_k_a, rwkv_r_k, rwkv_ln_w, rwkv_ln_b, rwkv_w_o, diff_lq1, diff_lk1, diff_lq2, diff_lk2, diff_subln_w, diff_w_o, w_out, norm_mlp_w, mlp_w1, mlp_w2, final_norm_w):
    bsz, seq, _ = x.shape
    seqlen = seq + N_META_TOKENS
    lp = -(-seqlen // SEQ_ALIGN) * SEQ_ALIGN
    tiles = _tiles(seq, lp)

    meta = jnp.broadcast_to(meta_tokens[None].astype(x.dtype), (bsz, N_META_TOKENS, D_MODEL))
    h_res = jnp.concatenate([meta, x, jnp.zeros((bsz, lp - seqlen, D_MODEL), x.dtype)], axis=1)
    hf = h_res.reshape(bsz * lp, D_MODEL)

    w = w_in.reshape(D_MODEL, -1).astype(BF16)
    c3 = 3 * D_MODEL
    w_rwkv = jnp.concatenate(
        [w[:, :c3], _pad_cols(w[:, c3:c3 + DECAY_LORA], LANE),
         _pad_cols(w[:, c3 + DECAY_LORA:c3 + DECAY_LORA + AAA_LORA], LANE),
         _pad_cols(w[:, c3 + DECAY_LORA + AAA_LORA:c3 + DECAY_LORA + AAA_LORA + GATE_LORA], GATE_LORA_PAD)],
        axis=1)
    rwkv_cols = c3 + DECAY_LORA + AAA_LORA + GATE_LORA
    w_diff = w[:, rwkv_cols:]
    mu = rwkv_mu[0][None, :]
    mu_p = jnp.concatenate(
        [mu[:, :c3], _pad_cols(mu[:, c3:c3 + DECAY_LORA], LANE),
         _pad_cols(mu[:, c3 + DECAY_LORA:c3 + DECAY_LORA + AAA_LORA], LANE),
         _pad_cols(mu[:, c3 + DECAY_LORA + AAA_LORA:], GATE_LORA_PAD)], axis=1)
    nw = norm_mix_w[0][None, :]

    tiles_per_seq = lp // tiles.proj
    head_id = jnp.arange(2 * PAIR) // HEAD_DIM
    ones_bd = (head_id[:, None] == head_id[None, :]).astype(BF16)
    vecs = (rwkv_w0[0][None, :], rwkv_a0[0][None, :], rwkv_k_k[0][None, :], rwkv_k_a[0][None, :],
            rwkv_r_k[0].reshape(1, D_MODEL))
    r, kmod, v, lw, kn, b, bonus, g, xn = _rwkv_proj(
        hf, nw, w_rwkv, mu_p, vecs, _pad_rows(rwkv_w2[0], LANE).astype(BF16),
        _pad_rows(rwkv_a2[0], LANE).astype(BF16), _pad_rows(rwkv_g2[0], GATE_LORA_PAD).astype(BF16),
        ones_bd, bsz=bsz, lp=lp, tm=tiles.proj)
    y = _rwkv_chunk(r, kmod, v, lw, kn, b)
    a5 = _in_proj(xn, w_diff, _rope_tables(lp), tm=tiles.proj, tiles_per_seq=tiles_per_seq).reshape(
        bsz, lp, 5 * D_MODEL)

    lams = (diff_lq1, diff_lk1, diff_lq2, diff_lk2)
    subw = diff_subln_w[0][None, :]
    kv_tiles = tiles.kv_blocks
    o_diff, row0 = jnp.zeros((bsz, lp, D_MODEL), BF16), 0
    for tq in (ATTN_Q_TILE, ATTN_KEY_BLOCK):
        n_tiles = (seqlen - row0) // tq
        if n_tiles:
            o_diff = _diff_attn(a5, lams, subw, o_diff, tq=tq, n_tiles=n_tiles, row0=row0, kv_tiles=kv_tiles)
            row0 += n_tiles * tq
    if row0 < lp:
        tail = max(-(-(seqlen - row0) // ATTN_TAIL_ROWS) * ATTN_TAIL_ROWS, ATTN_TAIL_ROWS)
        o_diff = _diff_attn(a5, lams, subw, o_diff, tq=tail, n_tiles=1, row0=row0, kv_tiles=kv_tiles,
                            out_rows=lp - row0)

    h_mid = _merge(y, bonus, g, o_diff, a5, h_res, rwkv_ln_w[0][None, :], rwkv_ln_b[0][None, :],
                   ones_bd, rwkv_w_o[0].astype(BF16), diff_w_o[0].astype(BF16),
                   w_out[0].astype(BF16), tm=tiles.merge)
    return _mlp(h_mid, norm_mlp_w[0][None, :], mlp_w1[0].astype(BF16), mlp_w2[0].astype(BF16),
                final_norm_w[None, :], seq=seq, row0=N_META_TOKENS, tm=tiles.mlp)
```

```python
import functools
import math
from typing import NamedTuple

import jax
import jax.numpy as jnp
from jax import lax
from jax.experimental import pallas as pl
from jax.experimental.pallas import tpu as pltpu

F32 = jnp.float32
BF16 = jnp.bfloat16

D_MODEL = 1024
N_META_TOKENS = 16
HEAD_DIM = 64
PAIR = 2 * HEAD_DIM
N_PAIRS = D_MODEL // PAIR
DECAY_LORA = 64
AAA_LORA = 64
GATE_LORA = 160
RWKV_LN_EPS = 64e-5
ROPE_THETA = 500000.0
ROPE_DIM = HEAD_DIM // 4
ROPE_HALF = ROPE_DIM // 2
D_FF = 4 * D_MODEL
NORM_EPS = 1e-5
SUBLN_EPS = 1e-5
LAMBDA_INIT = 0.8 - 0.6 * math.exp(-0.3 * 0)

LANE = 128
SUBLANE = 8
PREV_ROWS = 16
CHUNK = 64
CHUNKS_PER_STEP = 2
PROJ_COL_CHUNK = 256
ATTN_KEY_BLOCK = 256
ATTN_Q_TILE = 512
ATTN_TAIL_ROWS = 16
ATTN_HEADS_PER_STEP = 4
ATTN_BIG_STEP_BLOCKS = 4
MLP_FF_CHUNKS = 4
PROJ_TILES = (768, 512, 256)
MERGE_TILES = (768, 384, 256)
MLP_TILES = (512, 256, 128)
Q_SCALE = HEAD_DIM ** -0.5 * math.log2(math.e)
SEQ_ALIGN = 256
VMEM_LIMIT = 56 * 1024 * 1024

LORA_W_OFF = 3 * D_MODEL
LORA_A_OFF = LORA_W_OFF + LANE
LORA_G_OFF = LORA_A_OFF + LANE
GATE_LORA_PAD = 2 * LANE


def _params(sem):
    return pltpu.CompilerParams(dimension_semantics=sem, vmem_limit_bytes=VMEM_LIMIT)


def _rms(x, w, eps):
    return x * lax.rsqrt(jnp.mean(x * x, axis=-1, keepdims=True) + eps) * w


def _sigmoid(x):
    return 1.0 / (1.0 + jnp.exp(-x))


def _dot(a, b):
    return jnp.dot(a.astype(BF16), b.astype(BF16), preferred_element_type=F32)


def _dot_nt(a, b):
    return lax.dot_general(a.astype(BF16), b.astype(BF16), (((1,), (1,)), ((), ())),
                           preferred_element_type=F32)


def _dot_tn(a, b):
    return lax.dot_general(a.astype(BF16), b.astype(BF16), (((0,), (0,)), ((), ())),
                           preferred_element_type=F32)


def _in_proj_kernel(xn_ref, w_ref, cos_ref, s1_ref, s2_ref, o_ref):
    n_cols = o_ref.shape[1]
    cc = PROJ_COL_CHUNK
    xn = xn_ref[...]
    reps = cc // LANE
    cos = jnp.tile(cos_ref[...], (1, reps))
    s1 = jnp.tile(s1_ref[...], (1, reps))
    s2 = jnp.tile(s2_ref[...], (1, reps))
    for c in range(n_cols // cc):
        cols = slice(c * cc, (c + 1) * cc)
        p = jnp.dot(xn, w_ref[:, cols], preferred_element_type=F32)
        if c * cc < 2 * D_MODEL:
            nxt = pltpu.roll(p, cc - ROPE_HALF, axis=1)
            prv = pltpu.roll(p, ROPE_HALF, axis=1)
            scale = Q_SCALE if c * cc < D_MODEL else 1.0
            p = (p * cos + nxt * s1 + prv * s2) * scale
        o_ref[:, cols] = p.astype(o_ref.dtype)


def _in_proj(xn, w, rope, *, tm, tiles_per_seq):
    t_tokens = xn.shape[0]
    n_cols = w.shape[1]
    tab_spec = pl.BlockSpec((tm, LANE), lambda i: (i % tiles_per_seq, 0))
    return pl.pallas_call(
        _in_proj_kernel,
        grid=(t_tokens // tm,),
        in_specs=[pl.BlockSpec((tm, D_MODEL), lambda i: (i, 0)),
                  pl.BlockSpec((D_MODEL, n_cols), lambda i: (0, 0), pipeline_mode=pl.Buffered(1)),
                  tab_spec, tab_spec, tab_spec],
        out_specs=pl.BlockSpec((tm, n_cols), lambda i: (i, 0)),
        out_shape=jax.ShapeDtypeStruct((t_tokens, n_cols), BF16),
        compiler_params=_params(("arbitrary",)),
        name="in_proj_diff",
    )(xn, w, *rope)


def _rwkv_proj_kernel(x_ref, xprev_ref, nw_ref, w_ref, mu_ref, w0_ref, a0_ref, kk_ref, ka_ref, rk_ref,
                      w2_ref, a2_ref, g2_ref, ones_ref, r_o, k_o, v_o, lw_o, kn_o, b_o, bonus_o, g_o, xn_o,
                      *, tiles_per_seq):
    tm = x_ref.shape[0]
    blk = 2 * PAIR
    xn = jnp.concatenate([_rms(x_ref[...], nw_ref[...], NORM_EPS).astype(BF16),
                          _rms(xprev_ref[...], nw_ref[...], NORM_EPS).astype(BF16)], axis=0)
    first = (pl.program_id(0) % tiles_per_seq) == 0
    row = lax.broadcasted_iota(jnp.int32, (tm, blk), 0)
    xn_o[...] = xn[:tm]

    def proj(col0):
        cols = slice(col0, col0 + blk)
        p = jnp.dot(xn, w_ref[:, cols], preferred_element_type=F32)
        prev_row = jnp.where(first, 0.0, p[tm + PREV_ROWS - 1:tm + PREV_ROWS, :])
        p = p[:tm]
        shifted = jnp.where(row == 0, prev_row, pltpu.roll(p, 1, axis=0))
        return p + (shifted - p) * mu_ref[:, cols]

    def rkv(c):
        return proj(c * blk), proj(D_MODEL + c * blk), proj(2 * D_MODEL + c * blk)

    p_wa = proj(LORA_W_OFF)
    tanh_wd = jnp.tanh(p_wa[:, :LANE]).astype(BF16)
    ad = p_wa[:, LANE:].astype(BF16)
    sig_gd = _sigmoid(proj(LORA_G_OFF)).astype(BF16)
    ones_bd = ones_ref[...]
    n_blocks = D_MODEL // blk
    nxt = rkv(0)
    for c in range(n_blocks):
        cols = slice(c * blk, (c + 1) * blk)
        r, k, v = nxt
        if c + 1 < n_blocks:
            nxt = rkv(c + 1)
        z = w0_ref[:, cols] + jnp.dot(tanh_wd, w2_ref[:, cols], preferred_element_type=F32)
        log_decay = -math.exp(-0.5) * _sigmoid(z)
        a_rate = _sigmoid(a0_ref[:, cols] + jnp.dot(ad, a2_ref[:, cols], preferred_element_type=F32))
        gate = jnp.dot(sig_gd, g2_ref[:, cols], preferred_element_type=F32)

        kraw = k * kk_ref[:, cols]
        sumsq = _dot(kraw * kraw, ones_bd)
        knorm = kraw * lax.rsqrt(jnp.maximum(sumsq, 1e-24))
        kmod = k * (1.0 + (a_rate - 1.0) * ka_ref[:, cols])
        bonus = _dot(r * kmod * rk_ref[:, cols], ones_bd) * v

        bonus_o[0, :, cols] = bonus.astype(BF16)
        g_o[0, :, cols] = gate.astype(BF16)
        r16, k16, v16 = r.astype(BF16), kmod.astype(BF16), v.astype(BF16)
        kn16, b16 = knorm.astype(BF16), (knorm * a_rate).astype(BF16)
        for h in range(2):
            pair = 2 * c + h
            sl = slice(h * PAIR, (h + 1) * PAIR)
            r_o[0, pair] = r16[:, sl]
            k_o[0, pair] = k16[:, sl]
            v_o[0, pair] = v16[:, sl]
            lw_o[0, pair] = log_decay[:, sl]
            kn_o[0, pair] = kn16[:, sl]
            b_o[0, pair] = b16[:, sl]


def _rwkv_proj(xf, nw, w, mu, vecs, w2p, a2p, g2p, ones_bd, *, bsz, lp, tm):
    tiles_per_seq = lp // tm
    n_cols = w.shape[1]
    blocks = tm // PREV_ROWS
    const = lambda shape: pl.BlockSpec(shape, lambda i: (0, 0))
    pair_spec = pl.BlockSpec((1, N_PAIRS, tm, PAIR), lambda i: (i // tiles_per_seq, 0, i % tiles_per_seq, 0))
    tok_spec = pl.BlockSpec((1, tm, D_MODEL), lambda i: (i // tiles_per_seq, i % tiles_per_seq, 0))
    pair_shape = jax.ShapeDtypeStruct((bsz, N_PAIRS, lp, PAIR), BF16)
    decay_shape = jax.ShapeDtypeStruct((bsz, N_PAIRS, lp, PAIR), F32)
    tok_shape = jax.ShapeDtypeStruct((bsz, lp, D_MODEL), BF16)
    return pl.pallas_call(
        functools.partial(_rwkv_proj_kernel, tiles_per_seq=tiles_per_seq),
        grid=(bsz * tiles_per_seq,),
        in_specs=[pl.BlockSpec((tm, D_MODEL), lambda i: (i, 0)),
                  pl.BlockSpec((PREV_ROWS, D_MODEL), lambda i: (jnp.maximum(i * blocks - 1, 0), 0)),
                  const((1, D_MODEL)),
                  pl.BlockSpec((D_MODEL, n_cols), lambda i: (0, 0), pipeline_mode=pl.Buffered(1)),
                  const((1, n_cols))]
        + [const((1, D_MODEL))] * 5
        + [const((LANE, D_MODEL)), const((LANE, D_MODEL)), const((GATE_LORA_PAD, D_MODEL)),
           const((2 * PAIR, 2 * PAIR))],
        out_specs=[pair_spec] * 6 + [tok_spec] * 2 + [pl.BlockSpec((tm, D_MODEL), lambda i: (i, 0))],
        out_shape=[pair_shape] * 3 + [decay_shape] + [pair_shape] * 2 + [tok_shape] * 2
        + [jax.ShapeDtypeStruct((bsz * lp, D_MODEL), BF16)],
        compiler_params=_params(("arbitrary",)),
        name="rwkv_proj",
    )(xf, xf, nw, w, mu, *vecs, w2p, a2p, g2p, ones_bd)


def _block_diag(x, lane_head0):
    return jnp.concatenate([jnp.where(lane_head0, x, 0.0), jnp.where(lane_head0, 0.0, x)], axis=0)


def _chunk_unit(r, k, v, lw, kn, b, get_state):
    n = 2 * CHUNK
    row = lax.broadcasted_iota(jnp.int32, (CHUNK, PAIR), 0)
    lane_head0 = lax.broadcasted_iota(jnp.int32, (CHUNK, PAIR), 1) < HEAD_DIM
    c = lw
    for s in (1, 2, 4, 8, 16, 32):
        c = c + jnp.where(row >= s, pltpu.roll(c, s, axis=0), 0.0)
    c_end = c[CHUNK - 1:CHUNK, :]
    e_neg = jnp.exp(-c)
    e_end = jnp.exp(c_end - c)
    a_t = _block_diag(-kn * jnp.exp(c - lw), lane_head0)
    r_t = _block_diag(r * jnp.exp(c), lane_head0)
    b_t = _block_diag(b * e_neg, lane_head0)
    k_t = _block_diag(k * e_neg, lane_head0)
    b_e = _block_diag(b * e_end, lane_head0)
    k_e = _block_diag(k * e_end, lane_head0)
    v_bd = _block_diag(v, lane_head0)

    yield
    g = _dot_nt(jnp.concatenate([a_t, r_t], axis=0), jnp.concatenate([b_t, k_t], axis=0))
    ri = lax.broadcasted_iota(jnp.int32, (n, n), 0)
    ci = lax.broadcasted_iota(jnp.int32, (n, n), 1)
    m_ab = jnp.where(ci < ri, g[:n, :n], 0.0)
    m_ak = jnp.where(ci < ri, g[:n, n:], 0.0)
    a_rb = jnp.where(ci <= ri, g[n:, :n], 0.0)
    a_rk = jnp.where(ci <= ri, g[n:, n:], 0.0)

    t_inv = jnp.where(ci == ri, 1.0, m_ab)
    yield
    power = _dot(m_ab, m_ab)
    w1 = _dot(m_ak, v_bd)
    for _ in range(4):
        yield
        prod = _dot(power, jnp.concatenate([power, t_inv], axis=1))
        power = prod[:, :n]
        t_inv = t_inv + prod[:, n:]
    yield
    t_inv = t_inv + _dot(power, t_inv)

    yield
    tw = _dot(t_inv, jnp.concatenate([w1, a_t], axis=1))
    u0 = tw[:, :n]
    a_hat = tw[:, n:]
    rhs = jnp.concatenate([jnp.concatenate([a_hat, u0], axis=1),
                           jnp.concatenate([jnp.zeros_like(v_bd), v_bd], axis=1)], axis=0)
    yield
    pq = _dot_tn(jnp.concatenate([b_e, k_e], axis=0), rhs)
    ry = _dot(jnp.concatenate([a_rb, a_rk], axis=1), rhs)
    decay_end = jnp.exp(jnp.broadcast_to(c_end, (n, n)))
    p_mat = pq[:, :n] + jnp.where(ci == ri, decay_end, 0.0)
    yield
    state = get_state()
    ys = _dot(jnp.concatenate([r_t + ry[:, :n], p_mat], axis=0), state)
    y_bd = ys[:n] + ry[:, n:]
    y = y_bd[:CHUNK] + y_bd[CHUNK:]
    new_state = ys[n:] + pq[:, n:]
    return y, new_state


def _interleave(generators, results):
    pending = list(range(len(generators)))
    while pending:
        still = []
        for idx in pending:
            try:
                next(generators[idx])
                still.append(idx)
            except StopIteration as stop:
                results[idx] = stop.value
        pending = still


def _rwkv_chunk_kernel(r_ref, k_ref, v_ref, lw_ref, kn_ref, b_ref, y_ref, state_scr, *, n_units):
    @pl.when(pl.program_id(0) == 0)
    def _():
        state_scr[...] = jnp.zeros_like(state_scr)

    jobs = []
    outs = [None] * (CHUNKS_PER_STEP * n_units)
    for sub in range(CHUNKS_PER_STEP):
        rows = slice(sub * CHUNK, (sub + 1) * CHUNK)
        for u in range(n_units):
            bi, pi = u // N_PAIRS, u % N_PAIRS
            if sub == 0:
                get_state = lambda u=u: state_scr[u]
            else:
                get_state = lambda u=u, sub=sub: outs[(sub - 1) * n_units + u][1]
            jobs.append(_chunk_unit(r_ref[bi, pi, rows].astype(F32), k_ref[bi, pi, rows].astype(F32),
                                    v_ref[bi, pi, rows].astype(F32), lw_ref[bi, pi, rows],
                                    kn_ref[bi, pi, rows].astype(F32), b_ref[bi, pi, rows].astype(F32),
                                    get_state))
    _interleave(jobs, outs)
    for sub in range(CHUNKS_PER_STEP):
        rows = slice(sub * CHUNK, (sub + 1) * CHUNK)
        for u in range(n_units):
            bi, pi = u // N_PAIRS, u % N_PAIRS
            y, new_state = outs[sub * n_units + u]
            y_ref[bi, pi, rows] = y.astype(BF16)
            if sub == CHUNKS_PER_STEP - 1:
                state_scr[u] = new_state


def _rwkv_chunk(r, k, v, lw, kn, b):
    bsz, _, lp, _ = r.shape
    n_units = bsz * N_PAIRS
    rows = CHUNKS_PER_STEP * CHUNK
    spec = pl.BlockSpec((bsz, N_PAIRS, rows, PAIR), lambda c: (0, 0, c, 0))
    return pl.pallas_call(
        functools.partial(_rwkv_chunk_kernel, n_units=n_units),
        grid=(lp // rows,),
        in_specs=[spec] * 6,
        out_specs=spec,
        out_shape=jax.ShapeDtypeStruct(r.shape, BF16),
        scratch_shapes=[pltpu.VMEM((n_units, PAIR, PAIR), F32)],
        compiler_params=_params(("arbitrary",)),
        name="rwkv_chunk",
    )(r, k, v, lw, kn, b)


def _diff_attn_kernel(q_ref, k_ref, v_ref, lq1_ref, lk1_ref, lq2_ref, lk2_ref, subw_ref, out_init_ref,
                      o_ref, m_scr, acc_scr, s0_scr, *, tq, kv_tiles, block0):
    del out_init_ref
    kb = ATTN_KEY_BLOCK
    diag = block0 + (pl.program_id(2) * tq) // kb
    head0 = lax.broadcasted_iota(jnp.int32, (tq, PAIR), 1) < HEAD_DIM
    qq = []
    for t in range(ATTN_HEADS_PER_STEP):
        q = q_ref[0, :, t * PAIR:(t + 1) * PAIR]
        zero = jnp.zeros_like(q)
        qq.append(jnp.concatenate([jnp.where(head0, q, zero), jnp.where(head0, zero, q)], axis=0))

    def scores(t, j, width):
        start = pl.multiple_of(j * kb, kb)
        return _dot_nt(qq[t], k_ref[0, pl.ds(start, width), t * PAIR:(t + 1) * PAIR])

    def step(j, width, masked, s_first=None, emit_next_first=None):
        start = pl.multiple_of(j * kb, kb)
        ones = jnp.ones((width, PAIR), BF16)
        s_next = scores(0, j, width) if s_first is None else s_first
        for t in range(ATTN_HEADS_PER_STEP):
            s = s_next
            if t + 1 < ATTN_HEADS_PER_STEP:
                s_next = scores(t + 1, j, width)
            elif emit_next_first is not None:
                emit_next_first()
            v_ext = jnp.concatenate([v_ref[0, pl.ds(start, width), t * PAIR:(t + 1) * PAIR], ones], axis=1)
            for half in range(2):
                rows = slice(half * tq, (half + 1) * tq)
                sh = s[rows]
                if masked:
                    rr = lax.broadcasted_iota(jnp.int32, (tq, width), 0)
                    cc = lax.broadcasted_iota(jnp.int32, (tq, width), 1)
                    sh = jnp.where(cc <= rr, sh, -1e30)
                row_max = jnp.max(sh, axis=1, keepdims=True)
                if masked:
                    m_next = jnp.broadcast_to(row_max, (tq, PAIR))
                else:
                    m_prev = m_scr[t, rows]
                    m_next = jnp.maximum(m_prev, row_max)
                p = jnp.exp2(sh - jnp.tile(m_next, (1, width // LANE))).astype(BF16)
                pv = jnp.dot(p, v_ext, preferred_element_type=F32)
                if masked:
                    acc_scr[t, rows] = pv
                else:
                    acc_scr[t, rows] = jnp.tile(jnp.exp2(m_prev - m_next), (1, 2)) * acc_scr[t, rows] + pv
                m_scr[t, rows] = m_next

    n_big = diag // kv_tiles
    big = kv_tiles * kb

    def big_body(j, carry):
        def emit_next_first():
            j_next = jnp.minimum(j + 1, n_big - 1)
            s0_scr[...] = scores(0, j_next * kv_tiles, big)

        step(j * kv_tiles, big, False, s_first=s0_scr[...], emit_next_first=emit_next_first)
        return carry

    rest_blocks = max(tq // kb, 1)

    def rest_body(j, carry):
        step(n_big * kv_tiles + j * rest_blocks, rest_blocks * kb, False)
        return carry

    def emit_first_big():
        s0_scr[...] = scores(0, 0, big)

    step(diag, max(tq, kb), True, emit_next_first=emit_first_big)
    lax.fori_loop(0, n_big, big_body, 0)
    lax.fori_loop(0, (diag - n_big * kv_tiles) // rest_blocks, rest_body, 0)

    lam = (jnp.exp(jnp.sum(lq1_ref[...] * lk1_ref[...], axis=1, keepdims=True))
           - jnp.exp(jnp.sum(lq2_ref[...] * lk2_ref[...], axis=1, keepdims=True)) + LAMBDA_INIT)
    for t in range(ATTN_HEADS_PER_STEP):
        acc = acc_scr[t]
        o = acc[:, :PAIR] / acc[:, PAIR:]
        od = o[:tq] - lam * o[tq:]
        o_ref[0, :tq, t * PAIR:(t + 1) * PAIR] = (
            _rms(od, subw_ref[...], SUBLN_EPS) * (1.0 - LAMBDA_INIT)).astype(o_ref.dtype)
    if o_ref.shape[1] > tq:
        o_ref[0, tq:, :] = jnp.zeros((o_ref.shape[1] - tq, o_ref.shape[2]), o_ref.dtype)


def _diff_attn(a5, lams, subw, out, *, tq, n_tiles, row0, kv_tiles, out_rows=None):
    bsz, lp, _ = a5.shape
    width = ATTN_HEADS_PER_STEP * PAIR
    n_groups = D_MODEL // width
    out_rows = tq if out_rows is None else out_rows
    assert row0 % tq == 0 and row0 % out_rows == 0 and row0 % ATTN_KEY_BLOCK == 0
    assert tq % ATTN_KEY_BLOCK == 0 or n_tiles == 1
    q_spec = pl.BlockSpec((1, tq, width), lambda b, g, i: (b, row0 // tq + i, g))
    k_spec = pl.BlockSpec((1, lp, width), lambda b, g, i: (b, 0, n_groups + g))
    v_spec = pl.BlockSpec((1, lp, width), lambda b, g, i: (b, 0, 2 * n_groups + g),
                          pipeline_mode=pl.Buffered(1))
    lam_spec = pl.BlockSpec((1, HEAD_DIM), lambda b, g, i: (0, 0))
    in_specs = [q_spec, k_spec, v_spec, lam_spec, lam_spec, lam_spec, lam_spec,
                pl.BlockSpec((1, PAIR), lambda b, g, i: (0, 0))]
    in_specs.append(pl.BlockSpec(memory_space=pl.ANY))
    args = (a5, a5, a5, *lams, subw, out)
    return pl.pallas_call(
        functools.partial(_diff_attn_kernel, tq=tq, kv_tiles=kv_tiles, block0=row0 // ATTN_KEY_BLOCK),
        grid=(bsz, n_groups, n_tiles),
        in_specs=in_specs,
        out_specs=pl.BlockSpec((1, out_rows, width), lambda b, g, i: (b, row0 // out_rows + i, g)),
        out_shape=jax.ShapeDtypeStruct((bsz, lp, D_MODEL), BF16),
        input_output_aliases={len(args) - 1: 0},
        scratch_shapes=[pltpu.VMEM((ATTN_HEADS_PER_STEP, 2 * tq, PAIR), F32),
                        pltpu.VMEM((ATTN_HEADS_PER_STEP, 2 * tq, 2 * PAIR), F32),
                        pltpu.VMEM((2 * tq, kv_tiles * ATTN_KEY_BLOCK), F32)],
        compiler_params=_params(("arbitrary", "arbitrary", "arbitrary")),
        name=f"diff_attn_q{tq}",
    )(*args)


def _merge_kernel(y_ref, bonus_ref, g_ref, od_ref, gate_r_ref, gate_d_ref, res_ref, lnw_ref, lnb_ref,
                  ones_ref, wo_r_ref, wo_d_ref, wout_ref, o_ref):
    ones_bd = ones_ref[...]
    parts = []
    for pair in range(0, N_PAIRS, 2):
        y = jnp.concatenate([y_ref[0, pair], y_ref[0, pair + 1]], axis=1).astype(F32)
        mean = _dot(y, ones_bd) * (1.0 / HEAD_DIM)
        yc = y - mean
        var = _dot(yc * yc, ones_bd) * (1.0 / HEAD_DIM)
        parts.append(yc * lax.rsqrt(var + RWKV_LN_EPS))
    yn = jnp.concatenate(parts, axis=1)
    o_rwkv = (yn * lnw_ref[...] + lnb_ref[...] + bonus_ref[0].astype(F32)) * g_ref[0].astype(F32)
    br_rwkv = _dot(o_rwkv, wo_r_ref[...])
    br_diff = jnp.dot(od_ref[0], wo_d_ref[...], preferred_element_type=F32)
    merged = (_sigmoid(gate_r_ref[0].astype(F32)) * br_rwkv
              + _sigmoid(gate_d_ref[0].astype(F32)) * br_diff)
    o_ref[0] = res_ref[0] + _dot(merged, wout_ref[...])


def _merge(y, bonus, g, od, a5, res, lnw, lnb, ones_pair, wo_r, wo_d, wout, *, tm):
    bsz, lp, _ = res.shape
    grid = (bsz, lp // tm)
    tok = lambda c: pl.BlockSpec((1, tm, c), lambda b, i: (b, i, 0))
    vec = pl.BlockSpec((1, D_MODEL), lambda b, i: (0, 0))
    mat = pl.BlockSpec((D_MODEL, D_MODEL), lambda b, i: (0, 0))
    gate_r_spec = pl.BlockSpec((1, tm, D_MODEL), lambda b, i: (b, i, 3))
    gate_d_spec = pl.BlockSpec((1, tm, D_MODEL), lambda b, i: (b, i, 4))
    return pl.pallas_call(
        _merge_kernel,
        grid=grid,
        in_specs=[pl.BlockSpec((1, N_PAIRS, tm, PAIR), lambda b, i: (b, 0, i, 0)),
                  tok(D_MODEL), tok(D_MODEL), tok(D_MODEL), gate_r_spec, gate_d_spec, tok(D_MODEL), vec, vec,
                  pl.BlockSpec((2 * PAIR, 2 * PAIR), lambda b, i: (0, 0)), mat, mat, mat],
        out_specs=tok(D_MODEL),
        out_shape=jax.ShapeDtypeStruct(res.shape, F32),
        compiler_params=_params(("arbitrary", "arbitrary")),
        name="merge",
    )(y, bonus, g, od, a5, a5, res, lnw, lnb, ones_pair, wo_r, wo_d, wout)


def _mlp_kernel(x_ref, nw_ref, w1_ref, w2_ref, fw_ref, o_ref, *, n_ff_chunks):
    x = x_ref[...]
    h = _rms(x, nw_ref[...], NORM_EPS).astype(BF16)
    acc = x
    ff = D_FF // n_ff_chunks
    for c in range(n_ff_chunks):
        z = jnp.dot(h, w1_ref[:, c * ff:(c + 1) * ff], preferred_element_type=F32)
        act = jnp.square(jnp.maximum(z, 0.0))
        acc = acc + _dot(act, w2_ref[c * ff:(c + 1) * ff, :])
    o_ref[...] = _rms(acc, fw_ref[...], NORM_EPS)


def _mlp(x3, nw, w1, w2, fw, *, seq, row0, tm):
    bsz = x3.shape[0]
    x_spec = pl.BlockSpec((None, pl.Element(tm), pl.Element(D_MODEL)), lambda b, i: (b, pl.multiple_of(row0 + i * tm, SUBLANE), 0))
    vec = pl.BlockSpec((1, D_MODEL), lambda b, i: (0, 0))
    return pl.pallas_call(
        functools.partial(_mlp_kernel, n_ff_chunks=MLP_FF_CHUNKS),
        grid=(bsz, seq // tm),
        in_specs=[x_spec, vec, pl.BlockSpec((D_MODEL, D_FF), lambda b, i: (0, 0)),
                  pl.BlockSpec((D_FF, D_MODEL), lambda b, i: (0, 0)), vec],
        out_specs=pl.BlockSpec((None, tm, D_MODEL), lambda b, i: (b, i, 0)),
        out_shape=jax.ShapeDtypeStruct((bsz, seq, D_MODEL), F32),
        compiler_params=_params(("arbitrary", "arbitrary")),
        name="mlp",
    )(x3, nw, w1, w2, fw)


def _pad_cols(a, width):
    return jnp.pad(a, ((0, 0), (0, width - a.shape[1])))


def _pad_rows(a, height):
    return jnp.pad(a, ((0, height - a.shape[0]), (0, 0)))


def _rope_tables(lp):
    inv = ROPE_THETA ** (-jnp.arange(0, ROPE_DIM, 2, dtype=F32) / ROPE_DIM)
    ang = jnp.arange(lp, dtype=F32)[:, None] * inv[None, :]
    cos, sin = jnp.cos(ang), jnp.sin(ang)
    rest = HEAD_DIM - ROPE_DIM
    ones = jnp.ones((lp, rest), F32)
    zeros = jnp.zeros((lp, rest), F32)
    zh = jnp.zeros_like(sin)
    c = jnp.concatenate([cos, cos, ones], axis=1)
    s1 = jnp.concatenate([-sin, zh, zeros], axis=1)
    s2 = jnp.concatenate([zh, sin, zeros], axis=1)
    return tuple(jnp.tile(t, (1, LANE // HEAD_DIM)) for t in (c, s1, s2))


def _pick(n, candidates):
    for c in candidates:
        if n % c == 0:
            return c
    raise ValueError(n)


class _Tiles(NamedTuple):
    proj: int
    merge: int
    mlp: int
    kv_blocks: int


def _tiles(seq, lp):
    return _Tiles(proj=_pick(lp, PROJ_TILES), merge=_pick(lp, MERGE_TILES), mlp=_pick(seq, MLP_TILES),
                  kv_blocks=min(ATTN_BIG_STEP_BLOCKS, lp // ATTN_KEY_BLOCK))


def kernel(x, meta_tokens, norm_mix_w, w_in, rwkv_mu, rwkv_w0, rwkv_w2, rwkv_a0, rwkv_a2, rwkv_g2, rwkv_k_k, rwkv_k_a, rwkv_r_k, rwkv_ln_w, rwkv_ln_b, rwkv_w_o, diff_lq1, diff_lk1, diff_lq2, diff_lk2, diff_subln_w, diff_w_o, w_out, norm_mlp_w, mlp_w1, mlp_w2, final_norm_w):
    bsz, seq, _ = x.shape
    seqlen = seq + N_META_TOKENS
    lp = -(-seqlen // SEQ_ALIGN) * SEQ_ALIGN
    tiles = _tiles(seq, lp)

    meta = jnp.broadcast_to(meta_tokens[None].astype(x.dtype), (bsz, N_META_TOKENS, D_MODEL))
    h_res = jnp.concatenate([meta, x, jnp.zeros((bsz, lp - seqlen, D_MODEL), x.dtype)], axis=1)
    hf = h_res.reshape(bsz * lp, D_MODEL)

    w = w_in.reshape(D_MODEL, -1).astype(BF16)
    c3 = 3 * D_MODEL
    w_rwkv = jnp.concatenate(
        [w[:, :c3], _pad_cols(w[:, c3:c3 + DECAY_LORA], LANE),
         _pad_cols(w[:, c3 + DECAY_LORA:c3 + DECAY_LORA + AAA_LORA], LANE),
         _pad_cols(w[:, c3 + DECAY_LORA + AAA_LORA:c3 + DECAY_LORA + AAA_LORA + GATE_LORA], GATE_LORA_PAD)],
        axis=1)
    rwkv_cols = c3 + DECAY_LORA + AAA_LORA + GATE_LORA
    w_diff = w[:, rwkv_cols:]
    mu = rwkv_mu[0][None, :]
    mu_p = jnp.concatenate(
        [mu[:, :c3], _pad_cols(mu[:, c3:c3 + DECAY_LORA], LANE),
         _pad_cols(mu[:, c3 + DECAY_LORA:c3 + DECAY_LORA + AAA_LORA], LANE),
         _pad_cols(mu[:, c3 + DECAY_LORA + AAA_LORA:], GATE_LORA_PAD)], axis=1)
    nw = norm_mix_w[0][None, :]

    tiles_per_seq = lp // tiles.proj
    head_id = jnp.arange(2 * PAIR) // HEAD_DIM
    ones_bd = (head_id[:, None] == head_id[None, :]).astype(BF16)
    vecs = (rwkv_w0[0][None, :], rwkv_a0[0][None, :], rwkv_k_k[0][None, :], rwkv_k_a[0][None, :],
            rwkv_r_k[0].reshape(1, D_MODEL))
    r, kmod, v, lw, kn, b, bonus, g, xn = _rwkv_proj(
        hf, nw, w_rwkv, mu_p, vecs, _pad_rows(rwkv_w2[0], LANE).astype(BF16),
        _pad_rows(rwkv_a2[0], LANE).astype(BF16), _pad_rows(rwkv_g2[0], GATE_LORA_PAD).astype(BF16),
        ones_bd, bsz=bsz, lp=lp, tm=tiles.proj)
    y = _rwkv_chunk(r, kmod, v, lw, kn, b)
    a5 = _in_proj(xn, w_diff, _rope_tables(lp), tm=tiles.proj, tiles_per_seq=tiles_per_seq).reshape(
        bsz, lp, 5 * D_MODEL)

    lams = (diff_lq1, diff_lk1, diff_lq2, diff_lk2)
    subw = diff_subln_w[0][None, :]
    kv_tiles = tiles.kv_blocks
    o_diff, row0 = jnp.zeros((bsz, lp, D_MODEL), BF16), 0
    for tq in (ATTN_Q_TILE, ATTN_KEY_BLOCK):
        n_tiles = (seqlen - row0) // tq
        if n_tiles:
            o_diff = _diff_attn(a5, lams, subw, o_diff, tq=tq, n_tiles=n_tiles, row0=row0, kv_tiles=kv_tiles)
            row0 += n_tiles * tq
    if row0 < lp:
        tail = max(-(-(seqlen - row0) // ATTN_TAIL_ROWS) * ATTN_TAIL_ROWS, ATTN_TAIL_ROWS)
        o_diff = _diff_attn(a5, lams, subw, o_diff, tq=tail, n_tiles=1, row0=row0, kv_tiles=kv_tiles,
                            out_rows=lp - row0)

    h_mid = _merge(y, bonus, g, o_diff, a5, h_res, rwkv_ln_w[0][None, :], rwkv_ln_b[0][None, :],
                   ones_bd, rwkv_w_o[0].astype(BF16), diff_w_o[0].astype(BF16),
                   w_out[0].astype(BF16), tm=tiles.merge)
    return _mlp(h_mid, norm_mlp_w[0][None, :], mlp_w1[0].astype(BF16), mlp_w2[0].astype(BF16),
                final_norm_w[None, :], seq=seq, row0=N_META_TOKENS, tm=tiles.mlp)
```
